```python
import jax, jax.numpy as jnp
from jax import lax
import numpy as np

D_MODEL = 2048
BATCH = 1
SEQ = 16384
DEPTH = 2

N_HEADS = 16
HEAD_DIM = D_MODEL // N_HEADS
N_KV_GROUPS = 4
HEADS_PER_GROUP = N_HEADS // N_KV_GROUPS
KV_WIDTH = N_KV_GROUPS * HEAD_DIM
CMP_BLOCK = 32
CMP_STRIDE = 16
SLC_BLOCK = 64
RATIO = SLC_BLOCK // CMP_STRIDE
N_SLC = 16
WINDOW = 512
Q_BLOCK = 128
CMP_HIDDEN = 2 * HEAD_DIM
ROPE_THETA = 10000.0
NSA_IN_WIDTH = N_HEADS * HEAD_DIM + 6 * KV_WIDTH + 3 * N_HEADS

POOL_WINDOWS = (2, 4, 8, 16)
N_POOL_GROUPS = 4
POOL_GROUP = D_MODEL // N_POOL_GROUPS

MEM_LEN = 256
MEM_HEADS = 4
MEM_HEAD_DIM = 128
MEM_WIDTH = MEM_HEADS * MEM_HEAD_DIM

D_FF = 5632
N_EXPERTS = 8
TOP_K = 2
MOE_BLOCK = 512

N_MIXER_A_LAYERS = (DEPTH + 1) // 2
N_MIXER_B_LAYERS = DEPTH // 2
EPS = 1e-6
NEG_INF = -1e30
FORCE = 1e9

kernel_name = "hybrid_nsa_pool_moe_trunk"


def rmsnorm(x, g):
    xf = x.astype(jnp.float32)
    r = lax.rsqrt(jnp.mean(xf * xf, axis=-1, keepdims=True) + EPS)
    return (xf * r).astype(x.dtype) * g


def rope(x, pos):
    half = x.shape[-1] // 2
    inv = ROPE_THETA ** (-jnp.arange(half, dtype=jnp.float32) / half)
    ang = pos.astype(jnp.float32)[:, None] * inv[None, :]
    shape = (1, pos.shape[0]) + (1,) * (x.ndim - 3) + (half,)
    cos = jnp.cos(ang).reshape(shape).astype(x.dtype)
    sin = jnp.sin(ang).reshape(shape).astype(x.dtype)
    x1, x2 = x[..., :half], x[..., half:]
    return jnp.concatenate([x1 * cos - x2 * sin, x2 * cos + x1 * sin], axis=-1)


def masked_softmax(s, mask):
    p = jax.nn.softmax(jnp.where(mask, s, NEG_INF), axis=-1)
    return jnp.where(mask, p, 0.0)


def swiglu(h, wg, wu, wd):
    return (jax.nn.silu(h @ wg) * (h @ wu)) @ wd


def nsa_mixer(h, w_in, gate_b, pe_k, pe_v, ck_w1, ck_w2, cv_w1, cv_w2, w_out):
    B, S, D = h.shape
    G, Hg, HD = N_KV_GROUPS, HEADS_PER_GROUP, HEAD_DIM
    nc = S // CMP_STRIDE - 1
    ns = S // SLC_BLOCK
    n_sel = min(N_SLC, ns)
    nb = S // Q_BLOCK
    pos = jnp.arange(S)

    proj = h @ w_in
    D0 = N_HEADS * HD
    cuts = [D0 + i * KV_WIDTH for i in range(7)]
    q, k_c, v_c, k_s, v_s, k_w, v_w, gates = jnp.split(proj, cuts, axis=-1)
    q = rope(q.reshape(B, S, G, Hg, HD), pos) * (HD ** -0.5)
    k_s = rope(k_s.reshape(B, S, G, HD), pos)
    v_s = v_s.reshape(B, S, G, HD)
    k_w = rope(k_w.reshape(B, S, G, HD), pos)
    v_w = v_w.reshape(B, S, G, HD)
    gates = jax.nn.sigmoid(gates + gate_b).reshape(B, S, 3, G, Hg)

    def compress(t, pe, w1, w2):
        ch = t.reshape(B, S // CMP_STRIDE, CMP_STRIDE, G, HD)
        blk = jnp.concatenate([ch[:, :-1], ch[:, 1:]], axis=2) + pe[None, None, :, None, :]
        blk = blk.transpose(0, 1, 3, 2, 4).reshape(B, nc, G, CMP_BLOCK * HD)
        return jax.nn.gelu(blk @ w1) @ w2

    cmp_pos = jnp.arange(nc) * CMP_STRIDE + CMP_BLOCK - 1
    kc = rope(compress(k_c.reshape(B, S, G, HD), pe_k, ck_w1, ck_w2), cmp_pos)
    vc = compress(v_c.reshape(B, S, G, HD), pe_v, cv_w1, cv_w2)

    kb = k_s.reshape(B, ns, SLC_BLOCK, G, HD).transpose(0, 3, 1, 2, 4)
    vb = v_s.reshape(B, ns, SLC_BLOCK, G, HD).transpose(0, 3, 1, 2, 4)
    kw = jnp.pad(k_w, ((0, 0), (WINDOW, 0), (0, 0), (0, 0)))
    vw = jnp.pad(v_w, ((0, 0), (WINDOW, 0), (0, 0), (0, 0)))
    b_ix = jnp.arange(B)[:, None, None, None]
    g_ix = jnp.arange(G)[None, :, None, None]
    slc_w = jnp.array([1.0, 2.0, 2.0, 2.0], jnp.float32)
    blk_id = jnp.arange(ns)

    def block(qb):
        t0 = qb * Q_BLOCK
        qpos = t0 + jnp.arange(Q_BLOCK)
        qq = lax.dynamic_slice_in_dim(q, t0, Q_BLOCK, axis=1)
        gg = lax.dynamic_slice_in_dim(gates, t0, Q_BLOCK, axis=1)

        s_c = jnp.einsum('bqghd,bcgd->bghqc', qq, kc).astype(jnp.float32)
        m_c = cmp_pos[None, :] <= qpos[:, None]
        p_c = masked_softmax(s_c, m_c)
        o_c = jnp.einsum('bghqc,bcgd->bqghd', p_c.astype(vc.dtype), vc)

        imp = jnp.pad(p_c.sum(axis=2), ((0, 0), (0, 0), (0, 0), (1, RATIO)))
        s_blk = (imp[..., :RATIO * ns].reshape(B, G, Q_BLOCK, ns, RATIO) @ slc_w
                 + imp[..., RATIO:RATIO * ns + 1:RATIO])
        cur = qpos // SLC_BLOCK
        valid = blk_id[None, :] * SLC_BLOCK <= qpos[:, None]
        forced = ((blk_id[None, :] == 0) | (blk_id[None, :] == cur[:, None])
                  | (blk_id[None, :] == cur[:, None] - 1))
        s_blk = jnp.where(forced, FORCE, jnp.where(valid, s_blk, NEG_INF))
        _, idx = lax.top_k(s_blk, n_sel)

        kg = kb[b_ix, g_ix, idx]
        vg = vb[b_ix, g_ix, idx]
        s_s = jnp.einsum('bqghd,bgqnkd->bghqnk', qq, kg).astype(jnp.float32)
        key_pos = idx[..., None] * SLC_BLOCK + jnp.arange(SLC_BLOCK)
        m_s = key_pos <= qpos[None, None, :, None, None]
        p_s = masked_softmax(s_s.reshape(B, G, Hg, Q_BLOCK, n_sel * SLC_BLOCK),
                             m_s[:, :, None].reshape(B, G, 1, Q_BLOCK, n_sel * SLC_BLOCK))
        o_s = jnp.einsum('bghqnk,bgqnkd->bqghd',
                         p_s.reshape(B, G, Hg, Q_BLOCK, n_sel, SLC_BLOCK).astype(vg.dtype), vg)

        kk = lax.dynamic_slice_in_dim(kw, t0, Q_BLOCK + WINDOW, axis=1)
        vv = lax.dynamic_slice_in_dim(vw, t0, Q_BLOCK + WINDOW, axis=1)
        kpos = t0 - WINDOW + jnp.arange(Q_BLOCK + WINDOW)
        diff = qpos[:, None] - kpos[None, :]
        m_w = (diff >= 0) & (diff < WINDOW) & (kpos[None, :] >= 0)
        s_w = jnp.einsum('bqghd,bkgd->bghqk', qq, kk).astype(jnp.float32)
        p_w = masked_softmax(s_w, m_w)
        o_w = jnp.einsum('bghqk,bkgd->bqghd', p_w.astype(vv.dtype), vv)

        o = (gg[:, :, 0, :, :, None] * o_c + gg[:, :, 1, :, :, None] * o_s
             + gg[:, :, 2, :, :, None] * o_w)
        return o.reshape(B, Q_BLOCK, N_HEADS * HD)

    out = lax.map(block, jnp.arange(nb))
    out = out.transpose(1, 0, 2, 3).reshape(B, S, N_HEADS * HD)
    return out @ w_out


def pool_mixer(h, w_pool, b_pool, scale):
    B, S, D = h.shape
    hf = h.astype(jnp.float32)
    c0 = jnp.concatenate([jnp.zeros((B, 1, D), jnp.float32), jnp.cumsum(hf, axis=1)], axis=1)
    pos = jnp.arange(S)
    diffs = []
    for g, w in enumerate(POOL_WINDOWS):
        lo, hi = g * POOL_GROUP, (g + 1) * POOL_GROUP
        cg = c0[:, :, lo:hi]
        lag = jnp.concatenate([jnp.zeros((B, w - 1, POOL_GROUP), jnp.float32), cg[:, :S + 1 - w]], axis=1)
        cnt = jnp.minimum(pos + 1, w).astype(jnp.float32)[None, :, None]
        diffs.append((cg[:, 1:] - lag) / cnt - hf[:, :, lo:hi])
    d = jnp.stack(diffs, axis=2).astype(h.dtype)
    z = jnp.einsum('bsgc,gce->bsge', d, w_pool) + b_pool
    return z.reshape(B, S, D) * scale


def mem_attn(h, memn, wq, wk, wv, wo):
    B, S, _ = h.shape
    M = memn.shape[1]
    q = (h @ wq).reshape(B, S, MEM_HEADS, MEM_HEAD_DIM) * (MEM_HEAD_DIM ** -0.5)
    k = (memn @ wk).reshape(B, M, MEM_HEADS, MEM_HEAD_DIM)
    v = (memn @ wv).reshape(B, M, MEM_HEADS, MEM_HEAD_DIM)
    s = jnp.einsum('bshd,bmhd->bhsm', q, k).astype(jnp.float32)
    p = jax.nn.softmax(s, axis=-1).astype(v.dtype)
    o = jnp.einsum('bhsm,bmhd->bshd', p, v).reshape(B, S, MEM_WIDTH)
    return o @ wo


def moe(h, router, wg, wu, wd):
    B, S, D = h.shape
    N = B * S
    xf = h.reshape(N, D)
    logits = (xf @ router).astype(jnp.float32)
    top_val, top_idx = lax.top_k(logits, TOP_K)
    gate = jax.nn.softmax(top_val, axis=-1)
    e_flat = top_idx.reshape(-1)
    t_flat = jnp.repeat(jnp.arange(N, dtype=jnp.int32), TOP_K)
    w_flat = gate.reshape(-1)
    order = jnp.argsort(e_flat)
    e_sorted, t_sorted, w_sorted = e_flat[order], t_flat[order], w_flat[order]
    counts = jnp.zeros((N_EXPERTS,), jnp.int32).at[e_flat].add(1)
    padded = (counts + MOE_BLOCK - 1) // MOE_BLOCK * MOE_BLOCK
    off = jnp.cumsum(counts) - counts
    poff = jnp.cumsum(padded) - padded
    pend = poff + padded
    dest = poff[e_sorted] + jnp.arange(N * TOP_K, dtype=jnp.int32) - off[e_sorted]
    n_blk = (N * TOP_K + MOE_BLOCK - 1) // MOE_BLOCK + N_EXPERTS
    cap = n_blk * MOE_BLOCK
    tok_buf = jnp.full((cap,), N, jnp.int32).at[dest].set(t_sorted)
    w_buf = jnp.zeros((cap,), jnp.float32).at[dest].set(w_sorted)
    blk_start = jnp.arange(n_blk) * MOE_BLOCK
    blk_exp = jnp.minimum(jnp.sum(pend[None, :] <= blk_start[:, None], axis=1), N_EXPERTS - 1)
    x_pad = jnp.concatenate([xf, jnp.zeros((1, D), xf.dtype)], axis=0)

    def expert_block(args):
        toks, e = args
        return swiglu(x_pad[toks], wg[e], wu[e], wd[e])

    out = lax.map(expert_block, (tok_buf.reshape(n_blk, MOE_BLOCK), blk_exp))
    out = out.reshape(cap, D) * w_buf[:, None].astype(out.dtype)
    y = jnp.zeros((N + 1, D), out.dtype).at[tok_buf].add(out)[:N]
    return y.reshape(B, S, D)


def setup_inputs(seed: int = 0) -> dict:
    key = jax.random.key(seed)
    ks = iter(jax.random.split(key, 32))
    f32 = jnp.float32
    na, nb = N_MIXER_A_LAYERS, N_MIXER_B_LAYERS

    def w(shape, fan_in):
        return jax.random.normal(next(ks), shape, f32) * (fan_in ** -0.5)

    def gain(shape):
        return 1.0 + 0.02 * jax.random.normal(next(ks), shape, f32)

    def small(shape, s):
        return s * jax.random.normal(next(ks), shape, f32)

    return {
        "x": jax.random.normal(next(ks), (BATCH, SEQ, D_MODEL), f32),
        "mem": jax.random.normal(next(ks), (BATCH, MEM_LEN, D_MODEL), f32),
        "norm_mix": gain((DEPTH, D_MODEL)),
        "norm_mem_q": gain((DEPTH, D_MODEL)),
        "norm_mem_kv": gain((DEPTH, D_MODEL)),
        "norm_ffn": gain((DEPTH, D_MODEL)),
        "norm_final": gain((D_MODEL,)),
        "nsa_w_in": w((na, D_MODEL, NSA_IN_WIDTH), D_MODEL),
        "nsa_gate_b": small((na, 3 * N_HEADS), 0.1),
        "nsa_pe_k": small((na, CMP_BLOCK, HEAD_DIM), 0.1),
        "nsa_pe_v": small((na, CMP_BLOCK, HEAD_DIM), 0.1),
        "nsa_cmp_k_w1": w((na, CMP_BLOCK * HEAD_DIM, CMP_HIDDEN), CMP_BLOCK * HEAD_DIM),
        "nsa_cmp_k_w2": w((na, CMP_HIDDEN, HEAD_DIM), CMP_HIDDEN),
        "nsa_cmp_v_w1": w((na, CMP_BLOCK * HEAD_DIM, CMP_HIDDEN), CMP_BLOCK * HEAD_DIM),
        "nsa_cmp_v_w2": w((na, CMP_HIDDEN, HEAD_DIM), CMP_HIDDEN),
        "nsa_w_out": w((na, N_HEADS * HEAD_DIM, D_MODEL), N_HEADS * HEAD_DIM),
        "pool_w": w((nb, N_POOL_GROUPS, POOL_GROUP, POOL_GROUP), POOL_GROUP),
        "pool_b": small((nb, N_POOL_GROUPS, POOL_GROUP), 0.01),
        "pool_scale": gain((nb, D_MODEL)),
        "mem_wq": w((DEPTH, D_MODEL, MEM_WIDTH), D_MODEL),
        "mem_wk": w((DEPTH, D_MODEL, MEM_WIDTH), D_MODEL),
        "mem_wv": w((DEPTH, D_MODEL, MEM_WIDTH), D_MODEL),
        "mem_wo": w((DEPTH, MEM_WIDTH, D_MODEL), MEM_WIDTH),
        "ffn_w_gate": w((na, D_MODEL, D_FF), D_MODEL),
        "ffn_w_up": w((na, D_MODEL, D_FF), D_MODEL),
        "ffn_w_down": w((na, D_FF, D_MODEL), D_FF),
        "moe_router": w((nb, D_MODEL, N_EXPERTS), D_MODEL),
        "moe_w_gate": w((nb, N_EXPERTS, D_MODEL, D_FF), D_MODEL),
        "moe_w_up": w((nb, N_EXPERTS, D_MODEL, D_FF), D_MODEL),
        "moe_w_down": w((nb, N_EXPERTS, D_FF, D_MODEL), D_FF),
    }


def reference(x, mem, norm_mix, norm_mem_q, norm_mem_kv, norm_ffn, norm_final,
              nsa_w_in, nsa_gate_b, nsa_pe_k, nsa_pe_v, nsa_cmp_k_w1, nsa_cmp_k_w2,
              nsa_cmp_v_w1, nsa_cmp_v_w2, nsa_w_out,
              pool_w, pool_b, pool_scale,
              mem_wq, mem_wk, mem_wv, mem_wo,
              ffn_w_gate, ffn_w_up, ffn_w_down,
              moe_router, moe_w_gate, moe_w_up, moe_w_down):
    h = x
    for i in range(DEPTH):
        j = i // 2
        u = rmsnorm(h, norm_mix[i])
        if i % 2 == 0:
            h = h + nsa_mixer(u, nsa_w_in[j], nsa_gate_b[j], nsa_pe_k[j], nsa_pe_v[j],
                              nsa_cmp_k_w1[j], nsa_cmp_k_w2[j], nsa_cmp_v_w1[j], nsa_cmp_v_w2[j],
                              nsa_w_out[j])
        else:
            h = h + pool_mixer(u, pool_w[j], pool_b[j], pool_scale[j])
        memn = rmsnorm(mem, norm_mem_kv[i])
        h = h + mem_attn(rmsnorm(h, norm_mem_q[i]), memn, mem_wq[i], mem_wk[i], mem_wv[i], mem_wo[i])
        u = rmsnorm(h, norm_ffn[i])
        if i % 2 == 0:
            h = h + swiglu(u, ffn_w_gate[j], ffn_w_up[j], ffn_w_down[j])
        else:
            h = h + moe(u, moe_router[j], moe_w_gate[j], moe_w_up[j], moe_w_down[j])
    return rmsnorm(h, norm_final)
```

```python
import functools

import numpy as np
import jax
import jax.numpy as jnp
from jax import lax
from jax.experimental import pallas as pl
from jax.experimental.pallas import tpu as pltpu

F32 = jnp.float32
BF16 = jnp.bfloat16

N_HEADS = 16
HEAD_DIM = 128
N_KV_GROUPS = 4
HEADS_PER_GROUP = 4
KV_WIDTH = N_KV_GROUPS * HEAD_DIM
CMP_BLOCK = 32
CMP_STRIDE = 16
SLC_BLOCK = 64
RATIO = SLC_BLOCK // CMP_STRIDE
N_SLC = 16
WINDOW = 512
Q_BLOCK = 128
ROPE_THETA = 10000.0
POOL_WINDOWS = (2, 4, 8, 16)
POOL_GROUP = 512
MEM_HEADS = 4
MEM_HEAD_DIM = 128
N_EXPERTS = 8
TOP_K = 2
MOE_BLOCK = 512
EPS = 1e-6
NEG_INF = -1e30
FORCE = 1e9
REMOVED = -3e38

LANES = 128
VMEM_LIMIT = 56 * 1024 * 1024

SEL_KEY_TILE = 512
SEL_TILES_PER_CHUNK = LANES * SLC_BLOCK // SEL_KEY_TILE


def _cparams(sem):
    return pltpu.CompilerParams(dimension_semantics=sem, vmem_limit_bytes=VMEM_LIMIT)


def _rms_rows(xf, g):
    r = lax.rsqrt(jnp.mean(xf * xf, axis=-1, keepdims=True) + EPS)
    return (xf * r) * g


def _dot(a, b):
    return jnp.dot(a, b, preferred_element_type=F32)


def _dot_t(a, b):
    return lax.dot_general(a, b, (((1,), (1,)), ((), ())), preferred_element_type=F32)


def _proj_kernel(x_ref, g_ref, w_ref, wgate_ref, bgate_ref, cos_ref, sin_ref, o_ref, gates_ref, u_ref):
    j = pl.program_id(1)

    @pl.when(j == 0)
    def _():
        ub = _rms_rows(x_ref[...], g_ref[...]).astype(BF16)
        u_ref[...] = ub
        gates_ref[...] = jax.nn.sigmoid(_dot(ub, wgate_ref[...]) + bgate_ref[...])

    acc = _dot(u_ref[...], w_ref[...])
    n_q_tiles = N_HEADS * HEAD_DIM // KV_WIDTH
    is_q = j < n_q_tiles
    is_rope = is_q | (j == n_q_tiles + 2) | (j == n_q_tiles + 4)

    @pl.when(is_rope)
    def _():
        scale = jnp.where(is_q, HEAD_DIM ** -0.5, 1.0).astype(F32)
        cos = cos_ref[...]
        sin = sin_ref[...]
        for h in range(KV_WIDTH // HEAD_DIM):
            xh = acc[:, h * HEAD_DIM:(h + 1) * HEAD_DIM]
            rot = pltpu.roll(xh, HEAD_DIM // 2, axis=1)
            o_ref[:, h * HEAD_DIM:(h + 1) * HEAD_DIM] = ((xh * cos + rot * sin) * scale).astype(BF16)

    @pl.when(jnp.logical_not(is_rope))
    def _():
        o_ref[...] = acc.astype(BF16)


def _nsa_proj(x, g, w_main, w_gate, b_gate, cos, sin, tm):
    S, D = x.shape
    n_main = w_main.shape[1]
    tn = KV_WIDTH
    n_gate = w_gate.shape[1]
    return pl.pallas_call(
        _proj_kernel,
        grid=(S // tm, n_main // tn),
        in_specs=[
            pl.BlockSpec((tm, D), lambda i, j: (i, 0)),
            pl.BlockSpec((1, D), lambda i, j: (0, 0)),
            pl.BlockSpec((D, tn), lambda i, j: (0, j)),
            pl.BlockSpec((D, n_gate), lambda i, j: (0, 0)),
            pl.BlockSpec((1, n_gate), lambda i, j: (0, 0)),
            pl.BlockSpec((tm, HEAD_DIM), lambda i, j: (i, 0)),
            pl.BlockSpec((tm, HEAD_DIM), lambda i, j: (i, 0)),
        ],
        out_specs=[
            pl.BlockSpec((tm, tn), lambda i, j: (i, j)),
            pl.BlockSpec((tm, n_gate), lambda i, j: (i, 0)),
        ],
        out_shape=[
            jax.ShapeDtypeStruct((S, n_main), BF16),
            jax.ShapeDtypeStruct((S, n_gate), F32),
        ],
        scratch_shapes=[pltpu.VMEM((tm, D), BF16)],
        compiler_params=_cparams(("arbitrary", "arbitrary")),
        name="nsa_proj",
    )(x, g, w_main, w_gate, b_gate, cos, sin)


def _gelu_tanh(x):
    return 0.5 * x * (1.0 + jnp.tanh(np.sqrt(2.0 / np.pi).astype(np.float32) * (x + 0.044715 * (x * x * x))))


def _compress_kernel(t_ref, pe_ref, w1_ref, w2_ref, cos_ref, sin_ref, o_ref, *, apply_rope):
    half = CMP_STRIDE * HEAD_DIM
    t = t_ref[0].astype(F32)
    n = t.shape[0]
    a = _dot((t + pe_ref[:, :half]).astype(BF16), w1_ref[:half, :])
    b = _dot((t + pe_ref[:, half:]).astype(BF16), w1_ref[half:, :])
    hid = a + pltpu.roll(b, n - 1, axis=0)
    out = _dot(_gelu_tanh(hid).astype(BF16), w2_ref[...])
    if apply_rope:
        rot = pltpu.roll(out, HEAD_DIM // 2, axis=1)
        out = out * cos_ref[...] + rot * sin_ref[...]
    o_ref[0] = out.astype(BF16)


def _compress(t, pe_flat, w1, w2, cos, sin, apply_rope):
    G, n, width = t.shape
    hidden = w1.shape[1]
    return pl.pallas_call(
        functools.partial(_compress_kernel, apply_rope=apply_rope),
        grid=(G,),
        in_specs=[
            pl.BlockSpec((1, n, width), lambda g: (g, 0, 0)),
            pl.BlockSpec((1, 2 * width), lambda g: (0, 0)),
            pl.BlockSpec((2 * width, hidden), lambda g: (0, 0)),
            pl.BlockSpec((hidden, HEAD_DIM), lambda g: (0, 0)),
            pl.BlockSpec((n, HEAD_DIM), lambda g: (0, 0)),
            pl.BlockSpec((n, HEAD_DIM), lambda g: (0, 0)),
        ],
        out_specs=pl.BlockSpec((1, n, HEAD_DIM), lambda g: (g, 0, 0)),
        out_shape=jax.ShapeDtypeStruct((G, n, HEAD_DIM), BF16),
        compiler_params=_cparams(("arbitrary",)),
        name="nsa_compress_k" if apply_rope else "nsa_compress_v",
    )(t, pe_flat, w1, w2, cos, sin)


def _stack_heads(x):
    return jnp.concatenate([x[:, h * HEAD_DIM:(h + 1) * HEAD_DIM] for h in range(HEADS_PER_GROUP)], axis=0)


def _nsa_attn_kernel(q_ref, gates_ref, kc_ref, vc_ref, ks_ref, vs_ref, wsel_ref, et_ref, *rest, n_sel):
    n_win = WINDOW // Q_BLOCK + 1
    kw_refs = rest[:n_win]
    vw_refs = rest[n_win:2 * n_win]
    o_ref = rest[2 * n_win]
    sel_ref = rest[2 * n_win + 1]

    b = pl.program_id(1)
    t0 = b * Q_BLOCK
    Q = Q_BLOCK
    R = HEADS_PER_GROUP * Q
    qh = _stack_heads(q_ref[...])

    kc = kc_ref[0]
    n_cmp = kc.shape[0]
    s_c = _dot_t(qh, kc)
    qpos_c = t0 + (lax.broadcasted_iota(jnp.int32, (R, n_cmp), 0) & (Q - 1))
    cpos = lax.broadcasted_iota(jnp.int32, (R, n_cmp), 1) * CMP_STRIDE + (CMP_BLOCK - 1)
    ok_c = cpos <= qpos_c
    s_c = jnp.where(ok_c, s_c, NEG_INF)
    e_c = jnp.where(ok_c, jnp.exp(s_c - jnp.max(s_c, axis=1, keepdims=True)), 0.0)
    l_c = jnp.sum(e_c, axis=1, keepdims=True)
    p_c = e_c / jnp.where(l_c > 0.0, l_c, 1.0)
    o_c = _dot(p_c.astype(BF16), vc_ref[0])

    imp = (p_c[0:Q] + p_c[Q:2 * Q]) + p_c[2 * Q:3 * Q] + p_c[3 * Q:4 * Q]
    imp_hi = imp.astype(BF16)
    imp_lo = (imp - imp_hi.astype(F32)).astype(BF16)
    wsel = wsel_ref[...]
    s_blk = _dot(imp_hi, wsel) + _dot(imp_lo, wsel)
    nsp = s_blk.shape[1]
    blk = lax.broadcasted_iota(jnp.int32, (Q, nsp), 1)
    qp = t0 + lax.broadcasted_iota(jnp.int32, (Q, nsp), 0)
    cur = qp >> 6
    valid = blk * SLC_BLOCK <= qp
    forced = (blk == 0) | (blk == cur) | (blk == cur - 1)
    s_blk = jnp.where(forced, FORCE, jnp.where(valid, s_blk, NEG_INF))
    blk_f = blk.astype(F32)

    def pick(_, carry):
        s, sel = carry
        m = jnp.max(s, axis=1, keepdims=True)
        first = jnp.min(jnp.where(s == m, blk_f, float(nsp)), axis=1, keepdims=True)
        hit = blk_f == first
        return jnp.where(hit, REMOVED, s), jnp.where(hit, 1.0, sel)

    _, sel = lax.fori_loop(0, n_sel, pick, (s_blk, jnp.zeros((Q, nsp), F32)))
    for c in range(nsp // LANES):
        sel_ref[c] = sel[:, c * LANES:(c + 1) * LANES].astype(BF16)

    T = SEL_KEY_TILE
    n_tiles = (t0 + Q + T - 1) // T
    qp_t = t0 + lax.broadcasted_iota(jnp.int32, (Q, T), 0)
    kofs = lax.broadcasted_iota(jnp.int32, (Q, T), 1)

    def sel_step(kt, carry):
        m, l, acc = carry
        k0 = pl.multiple_of(kt * T, T)
        k = ks_ref[pl.ds(k0, T), :]
        v = vs_ref[pl.ds(k0, T), :]
        s = _dot_t(qh, k)
        chunk = kt // SEL_TILES_PER_CHUNK
        r0 = pl.multiple_of((kt % SEL_TILES_PER_CHUNK) * T, T)
        picked = _dot_t(sel_ref[chunk], et_ref[pl.ds(r0, T), :])
        ok = (picked > 0.5) & (kofs + k0 <= qp_t)
        bias = jnp.where(ok, 0.0, NEG_INF)
        s = s + jnp.concatenate([bias] * HEADS_PER_GROUP, axis=0)
        m_new = jnp.maximum(m, jnp.max(s, axis=1, keepdims=True))
        alpha = jnp.exp(m - m_new)
        p = jnp.exp(s - m_new)
        l = alpha * l + jnp.sum(p, axis=1, keepdims=True)
        acc = alpha * acc + _dot(p.astype(BF16), v)
        return m_new, l, acc

    m_s, l_s, acc_s = lax.fori_loop(
        0, n_tiles, sel_step,
        (jnp.full((R, 1), NEG_INF, F32), jnp.zeros((R, 1), F32), jnp.zeros((R, HEAD_DIM), F32)))
    o_s = acc_s / l_s

    kw = jnp.concatenate([r[...] for r in kw_refs], axis=0)
    vw = jnp.concatenate([r[...] for r in vw_refs], axis=0)
    nw = WINDOW + Q
    s_w = _dot_t(qh, kw)
    qpos_w = t0 + (lax.broadcasted_iota(jnp.int32, (R, nw), 0) & (Q - 1))
    kpos_w = t0 - WINDOW + lax.broadcasted_iota(jnp.int32, (R, nw), 1)
    diff = qpos_w - kpos_w
    ok_w = (diff >= 0) & (diff < WINDOW) & (kpos_w >= 0)
    s_w = jnp.where(ok_w, s_w, NEG_INF)
    e_w = jnp.where(ok_w, jnp.exp(s_w - jnp.max(s_w, axis=1, keepdims=True)), 0.0)
    p_w = e_w / jnp.sum(e_w, axis=1, keepdims=True)
    o_w = _dot(p_w.astype(BF16), vw)

    gt = gates_ref[...]
    for h in range(HEADS_PER_GROUP):
        rows = slice(h * Q, (h + 1) * Q)
        o = (gt[:, h:h + 1] * o_c[rows]
             + gt[:, HEADS_PER_GROUP + h:HEADS_PER_GROUP + h + 1] * o_s[rows]
             + gt[:, 2 * HEADS_PER_GROUP + h:2 * HEADS_PER_GROUP + h + 1] * o_w[rows])
        o_ref[:, h * HEAD_DIM:(h + 1) * HEAD_DIM] = o.astype(BF16)


def _selection_matrices(n_cmp, nsp):
    c = np.arange(n_cmp)[:, None]
    j = np.arange(nsp)[None, :]
    d = c - RATIO * j
    wsel = np.where((d == -1) | (d == RATIO - 1), 1.0, np.where((d >= 0) & (d < RATIO - 1), 2.0, 0.0))
    rows = np.arange(SEL_TILES_PER_CHUNK * SEL_KEY_TILE)
    et = (rows[:, None] // SLC_BLOCK == np.arange(LANES)[None, :]).astype(np.float32)
    return jnp.asarray(wsel, BF16), jnp.asarray(et, BF16)


def _nsa_attention(proj, gates, kc, vc):
    S = proj.shape[0]
    G = N_KV_GROUPS
    nb = S // Q_BLOCK
    n_cmp = kc.shape[1]
    ns = S // SLC_BLOCK
    nsp = -(-ns // LANES) * LANES
    n_sel = min(N_SLC, ns)
    wsel, et = _selection_matrices(n_cmp, nsp)
    n_win = WINDOW // Q_BLOCK + 1
    cpb = KV_WIDTH // HEAD_DIM
    q_cols = N_HEADS * HEAD_DIM // HEAD_DIM
    ks_col, vs_col = q_cols + 2 * cpb, q_cols + 3 * cpb
    kw_col, vw_col = q_cols + 4 * cpb, q_cols + 5 * cpb

    def win_spec(col, i):
        return pl.BlockSpec((Q_BLOCK, HEAD_DIM),
                            lambda g, b: (jnp.maximum(b - (n_win - 1) + i, 0), col + g))

    in_specs = [
        pl.BlockSpec((Q_BLOCK, KV_WIDTH), lambda g, b: (b, g)),
        pl.BlockSpec((Q_BLOCK, LANES), lambda g, b: (b, g)),
        pl.BlockSpec((1, n_cmp, HEAD_DIM), lambda g, b: (g, 0, 0)),
        pl.BlockSpec((1, n_cmp, HEAD_DIM), lambda g, b: (g, 0, 0)),
        pl.BlockSpec((S, HEAD_DIM), lambda g, b: (0, ks_col + g)),
        pl.BlockSpec((S, HEAD_DIM), lambda g, b: (0, vs_col + g)),
        pl.BlockSpec(wsel.shape, lambda g, b: (0, 0)),
        pl.BlockSpec(et.shape, lambda g, b: (0, 0)),
    ]
    in_specs += [win_spec(kw_col, i) for i in range(n_win)]
    in_specs += [win_spec(vw_col, i) for i in range(n_win)]
    args = [proj, gates, kc, vc, proj, proj, wsel, et] + [proj] * (2 * n_win)
    return pl.pallas_call(
        functools.partial(_nsa_attn_kernel, n_sel=n_sel),
        grid=(G, nb),
        in_specs=in_specs,
        out_specs=pl.BlockSpec((Q_BLOCK, KV_WIDTH), lambda g, b: (b, g)),
        out_shape=jax.ShapeDtypeStruct((S, N_HEADS * HEAD_DIM), BF16),
        scratch_shapes=[pltpu.VMEM((nsp // LANES, Q_BLOCK, LANES), BF16)],
        compiler_params=_cparams(("arbitrary", "arbitrary")),
        name="nsa_attention",
    )(*args)


def _matmul_residual_kernel(a_ref, w_ref, res_ref, o_ref):
    o_ref[...] = res_ref[...] + _dot(a_ref[...], w_ref[...])


def _matmul_residual(a, w, res, tm):
    S, K = a.shape
    N = w.shape[1]
    return pl.pallas_call(
        _matmul_residual_kernel,
        grid=(S // tm,),
        in_specs=[
            pl.BlockSpec((tm, K), lambda i: (i, 0)),
            pl.BlockSpec((K, N), lambda i: (0, 0)),
            pl.BlockSpec((tm, N), lambda i: (i, 0)),
        ],
        out_specs=pl.BlockSpec((tm, N), lambda i: (i, 0)),
        out_shape=jax.ShapeDtypeStruct((S, N), F32),
        compiler_params=_cparams(("arbitrary",)),
        name="out_proj_residual",
    )(a, w, res)


def _mem_kv_kernel(mem_ref, g_ref, wk_ref, wv_ref, k_ref, v_ref):
    mb = _rms_rows(mem_ref[...], g_ref[...]).astype(BF16)
    k_ref[...] = _dot(mb, wk_ref[...]).astype(BF16)
    v_ref[...] = _dot(mb, wv_ref[...]).astype(BF16)


def _mem_kv(mem, g, wk, wv):
    M, D = mem.shape
    W = wk.shape[1]
    full = lambda shape: pl.BlockSpec(shape, lambda i: (0, 0))
    return pl.pallas_call(
        _mem_kv_kernel,
        grid=(1,),
        in_specs=[full((M, D)), full((1, D)), full((D, W)), full((D, W))],
        out_specs=[full((M, W)), full((M, W))],
        out_shape=[jax.ShapeDtypeStruct((M, W), BF16)] * 2,
        compiler_params=_cparams(("arbitrary",)),
        name="mem_kv",
    )(mem, g, wk, wv)


def _mem_attn_kernel(h_ref, g_ref, wq_ref, k_ref, v_ref, wo_ref, o_ref):
    h = h_ref[...]
    ub = _rms_rows(h, g_ref[...]).astype(BF16)
    q = (_dot(ub, wq_ref[...]) * (MEM_HEAD_DIM ** -0.5)).astype(BF16)
    outs = []
    for hd in range(MEM_HEADS):
        cols = slice(hd * MEM_HEAD_DIM, (hd + 1) * MEM_HEAD_DIM)
        s = _dot_t(q[:, cols], k_ref[:, cols])
        e = jnp.exp(s - jnp.max(s, axis=1, keepdims=True))
        p = e / jnp.sum(e, axis=1, keepdims=True)
        outs.append(_dot(p.astype(BF16), v_ref[:, cols]).astype(BF16))
    o = jnp.concatenate(outs, axis=1)
    o_ref[...] = h + _dot(o, wo_ref[...])


def _mem_attn(h, g, wq, k, v, wo, tm):
    S, D = h.shape
    W = wq.shape[1]
    M = k.shape[0]
    full = lambda shape: pl.BlockSpec(shape, lambda i: (0, 0))
    return pl.pallas_call(
        _mem_attn_kernel,
        grid=(S // tm,),
        in_specs=[pl.BlockSpec((tm, D), lambda i: (i, 0)), full((1, D)), full((D, W)),
                  full((M, W)), full((M, W)), full((W, D))],
        out_specs=pl.BlockSpec((tm, D), lambda i: (i, 0)),
        out_shape=jax.ShapeDtypeStruct((S, D), F32),
        compiler_params=_cparams(("arbitrary",)),
        name="mem_attention",
    )(h, g, wq, k, v, wo)


def _swiglu_tile(ub, wg, wu, wd):
    gate = _dot(ub, wg)
    up = _dot(ub, wu)
    act = (gate * jax.nn.sigmoid(gate)) * up
    return _dot(act.astype(BF16), wd)


def _ffn_kernel(h_ref, g_ref, wg_ref, wu_ref, wd_ref, o_ref, u_ref):
    f = pl.program_id(1)

    @pl.when(f == 0)
    def _():
        h = h_ref[...]
        u_ref[...] = _rms_rows(h, g_ref[...]).astype(BF16)
        o_ref[...] = h

    o_ref[...] += _swiglu_tile(u_ref[...], wg_ref[...], wu_ref[...], wd_ref[...])


def _ffn(h, g, wg, wu, wd, tm, tf):
    S, D = h.shape
    F = wg.shape[1]
    return pl.pallas_call(
        _ffn_kernel,
        grid=(S // tm, F // tf),
        in_specs=[
            pl.BlockSpec((tm, D), lambda i, f: (i, 0)),
            pl.BlockSpec((1, D), lambda i, f: (0, 0)),
            pl.BlockSpec((D, tf), lambda i, f: (0, f)),
            pl.BlockSpec((D, tf), lambda i, f: (0, f)),
            pl.BlockSpec((tf, D), lambda i, f: (f, 0)),
        ],
        out_specs=pl.BlockSpec((tm, D), lambda i, f: (i, 0)),
        out_shape=jax.ShapeDtypeStruct((S, D), F32),
        scratch_shapes=[pltpu.VMEM((tm, D), BF16)],
        compiler_params=_cparams(("arbitrary", "arbitrary")),
        name="dense_swiglu",
    )(h, g, wg, wu, wd)


POOL_HALO = 16


def _pool_kernel(h_ref, halo_ref, g_ref, w_ref, b_ref, scale_ref, o_ref, ext_ref):
    i = pl.program_id(0)
    tm = h_ref.shape[0]
    h = h_ref[...]
    g = g_ref[...]
    u = _rms_rows(h, g)
    halo = _rms_rows(halo_ref[...], g)
    ext_ref[0:POOL_HALO, :] = jnp.where(i > 0, halo, 0.0)
    ext_ref[POOL_HALO:, :] = u
    pos = i * tm + lax.broadcasted_iota(jnp.int32, (tm, 1), 0)
    for gi, w in enumerate(POOL_WINDOWS):
        cols = slice(gi * POOL_GROUP, (gi + 1) * POOL_GROUP)
        tot = u[:, cols]
        for k in range(1, w):
            tot = tot + ext_ref[POOL_HALO - k:POOL_HALO - k + tm, cols]
        cnt = jnp.minimum(pos + 1, w).astype(F32)
        d = tot / cnt - u[:, cols]
        z = _dot(d.astype(BF16), w_ref[gi]) + b_ref[:, cols]
        o_ref[:, cols] = h[:, cols] + z * scale_ref[:, cols]


def _pool_mixer(h, g, w, b, scale, tm):
    S, D = h.shape
    ratio = tm // POOL_HALO
    return pl.pallas_call(
        _pool_kernel,
        grid=(S // tm,),
        in_specs=[
            pl.BlockSpec((tm, D), lambda i: (i, 0)),
            pl.BlockSpec((POOL_HALO, D), lambda i: (jnp.maximum(i * ratio - 1, 0), 0)),
            pl.BlockSpec((1, D), lambda i: (0, 0)),
            pl.BlockSpec(w.shape, lambda i: (0, 0, 0)),
            pl.BlockSpec((1, D), lambda i: (0, 0)),
            pl.BlockSpec((1, D), lambda i: (0, 0)),
        ],
        out_specs=pl.BlockSpec((tm, D), lambda i: (i, 0)),
        out_shape=jax.ShapeDtypeStruct((S, D), F32),
        scratch_shapes=[pltpu.VMEM((tm + POOL_HALO, D), F32)],
        compiler_params=_cparams(("arbitrary",)),
        name="pool_mixer",
    )(h, h, g, w, b, scale)


def _router_kernel(h_ref, g_ref, rhi_ref, rlo_ref, u_ref, idx_ref, w_ref):
    u = _rms_rows(h_ref[...], g_ref[...])
    u_ref[...] = u
    u_hi = u.astype(BF16)
    u_lo = (u - u_hi.astype(F32)).astype(BF16)
    rhi = rhi_ref[...]
    logits = (_dot(u_hi, rhi) + _dot(u_lo, rhi)) + _dot(u_hi, rlo_ref[...])
    lane = lax.broadcasted_iota(jnp.int32, logits.shape, 1)
    lane_f = lane.astype(F32)
    s = jnp.where(lane < N_EXPERTS, logits, REMOVED)
    m1 = jnp.max(s, axis=1, keepdims=True)
    i1 = jnp.min(jnp.where(s == m1, lane_f, float(LANES)), axis=1, keepdims=True)
    s = jnp.where(lane_f == i1, REMOVED, s)
    m2 = jnp.max(s, axis=1, keepdims=True)
    i2 = jnp.min(jnp.where(s == m2, lane_f, float(LANES)), axis=1, keepdims=True)
    e2 = jnp.exp(m2 - m1)
    den = 1.0 + e2
    idx_ref[...] = jnp.where(lane == 0, i1, jnp.where(lane == 1, i2, 0.0)).astype(jnp.int32)
    w_ref[...] = jnp.where(lane == 0, 1.0 / den, jnp.where(lane == 1, e2 / den, 0.0))


def _router(h, g, r_hi, r_lo, tm):
    S, D = h.shape
    return pl.pallas_call(
        _router_kernel,
        grid=(S // tm,),
        in_specs=[
            pl.BlockSpec((tm, D), lambda i: (i, 0)),
            pl.BlockSpec((1, D), lambda i: (0, 0)),
            pl.BlockSpec((D, LANES), lambda i: (0, 0)),
            pl.BlockSpec((D, LANES), lambda i: (0, 0)),
        ],
        out_specs=[
            pl.BlockSpec((tm, D), lambda i: (i, 0)),
            pl.BlockSpec((tm, LANES), lambda i: (i, 0)),
            pl.BlockSpec((tm, LANES), lambda i: (i, 0)),
        ],
        out_shape=[
            jax.ShapeDtypeStruct((S, D), F32),
            jax.ShapeDtypeStruct((S, LANES), jnp.int32),
            jax.ShapeDtypeStruct((S, LANES), F32),
        ],
        compiler_params=_cparams(("arbitrary",)),
        name="moe_router",
    )(h, g, r_hi, r_lo)


def _row_copy(src_hbm, dst_vmem, sem, src_row, dst_row):
    return pltpu.make_async_copy(src_hbm.at[pl.ds(src_row, 1), :], dst_vmem.at[pl.ds(dst_row, 1), :], sem)


def _moe_expert_kernel(tok_ref, exp_ref, nused_ref, u_hbm, wg_ref, wu_ref, wd_ref, o_ref, x_ref, xb_ref, sem):
    i = pl.program_id(0)
    f = pl.program_id(1)
    rows = x_ref.shape[0]
    active = i < nused_ref[0]

    @pl.when(active & (f == 0))
    def _():
        def start(r, c):
            _row_copy(u_hbm, x_ref, sem, tok_ref[i * rows + r], r).start()
            return c

        lax.fori_loop(0, rows, start, 0)

        def wait(r, c):
            _row_copy(u_hbm, x_ref, sem, 0, r).wait()
            return c

        lax.fori_loop(0, rows, wait, 0)
        xb_ref[...] = x_ref[...].astype(BF16)
        o_ref[...] = jnp.zeros_like(o_ref)

    @pl.when(active)
    def _():
        o_ref[...] += _swiglu_tile(xb_ref[...], wg_ref[0], wu_ref[0], wd_ref[0])

    @pl.when(jnp.logical_not(active) & (f == 0))
    def _():
        o_ref[...] = jnp.zeros_like(o_ref)


def _moe_experts(tok, blk_exp, n_used, u, wg, wu, wd, tf):
    N, D = u.shape
    n_blk = blk_exp.shape[0]
    F = wg.shape[2]
    rows = MOE_BLOCK
    n_f = F // tf

    def ftile(i, f, nu):
        return jnp.where(i < nu[0], f, n_f - 1)

    grid_spec = pltpu.PrefetchScalarGridSpec(
        num_scalar_prefetch=3,
        grid=(n_blk, n_f),
        in_specs=[
            pl.BlockSpec(memory_space=pl.ANY),
            pl.BlockSpec((1, D, tf), lambda i, f, tok, ex, nu: (ex[i], 0, ftile(i, f, nu))),
            pl.BlockSpec((1, D, tf), lambda i, f, tok, ex, nu: (ex[i], 0, ftile(i, f, nu))),
            pl.BlockSpec((1, tf, D), lambda i, f, tok, ex, nu: (ex[i], ftile(i, f, nu), 0)),
        ],
        out_specs=pl.BlockSpec((rows, D), lambda i, f, tok, ex, nu: (i, 0)),
        scratch_shapes=[pltpu.VMEM((rows, D), F32), pltpu.VMEM((rows, D), BF16), pltpu.SemaphoreType.DMA(())],
    )
    return pl.pallas_call(
        _moe_expert_kernel,
        grid_spec=grid_spec,
        out_shape=jax.ShapeDtypeStruct((n_blk * rows, D), F32),
        compiler_params=_cparams(("arbitrary", "arbitrary")),
        name="moe_experts",
    )(tok, blk_exp, n_used, u, wg, wu, wd)


def _moe_combine_kernel(pos_ref, h_ref, w_ref, gfin_ref, y_hbm, o_ref, r_ref, sem):
    i = pl.program_id(0)
    tm = h_ref.shape[0]

    def start(r, c):
        for k in range(TOP_K):
            _row_copy(y_hbm, r_ref.at[k], sem, pos_ref[(i * tm + r) * TOP_K + k], r).start()
        return c

    lax.fori_loop(0, tm, start, 0)

    def wait(r, c):
        for k in range(TOP_K):
            _row_copy(y_hbm, r_ref.at[k], sem, 0, r).wait()
        return c

    lax.fori_loop(0, tm, wait, 0)
    w = w_ref[...]
    y = h_ref[...] + (r_ref[0] * w[:, 0:1] + r_ref[1] * w[:, 1:2])
    o_ref[...] = _rms_rows(y, gfin_ref[...])


def _moe_combine(pos, h, w, g_final, y, tm):
    S, D = h.shape
    grid_spec = pltpu.PrefetchScalarGridSpec(
        num_scalar_prefetch=1,
        grid=(S // tm,),
        in_specs=[
            pl.BlockSpec((tm, D), lambda i, pos: (i, 0)),
            pl.BlockSpec((tm, LANES), lambda i, pos: (i, 0)),
            pl.BlockSpec((1, D), lambda i, pos: (0, 0)),
            pl.BlockSpec(memory_space=pl.ANY),
        ],
        out_specs=pl.BlockSpec((tm, D), lambda i, pos: (i, 0)),
        scratch_shapes=[pltpu.VMEM((TOP_K, tm, D), F32), pltpu.SemaphoreType.DMA(())],
    )
    return pl.pallas_call(
        _moe_combine_kernel,
        grid_spec=grid_spec,
        out_shape=jax.ShapeDtypeStruct((S, D), F32),
        compiler_params=_cparams(("arbitrary",)),
        name="moe_combine_norm",
    )(pos, h, w, g_final, y)


def _moe_dispatch_indices(top_idx):
    N = top_idx.shape[0]
    e_flat = top_idx.reshape(-1)
    onehot = (e_flat[:, None] == jnp.arange(N_EXPERTS, dtype=jnp.int32)[None, :]).astype(jnp.int32)
    csum = jnp.cumsum(onehot, axis=0)
    counts = csum[-1]
    rank = jnp.take_along_axis(csum, e_flat[:, None], axis=1)[:, 0] - 1
    padded = (counts + MOE_BLOCK - 1) // MOE_BLOCK * MOE_BLOCK
    pend = jnp.cumsum(padded)
    poff = pend - padded
    dest = poff[e_flat] + rank
    n_blk = (N * TOP_K + MOE_BLOCK - 1) // MOE_BLOCK + N_EXPERTS
    cap = n_blk * MOE_BLOCK
    t_flat = jnp.repeat(jnp.arange(N, dtype=jnp.int32), TOP_K)
    tok_buf = jnp.zeros((cap,), jnp.int32).at[dest].set(t_flat)
    blk_start = jnp.arange(n_blk, dtype=jnp.int32) * MOE_BLOCK
    blk_exp = jnp.minimum(jnp.sum(pend[None, :] <= blk_start[:, None], axis=1), N_EXPERTS - 1).astype(jnp.int32)
    n_used = (pend[-1] // MOE_BLOCK).astype(jnp.int32).reshape(1)
    return tok_buf, blk_exp, n_used, dest.astype(jnp.int32)


def _rope_tables(pos):
    half = HEAD_DIM // 2
    inv = ROPE_THETA ** (-jnp.arange(half, dtype=F32) / half)
    ang = pos.astype(F32)[:, None] * inv[None, :]
    cos, sin = jnp.cos(ang), jnp.sin(ang)
    return jnp.concatenate([cos, cos], axis=1), jnp.concatenate([-sin, sin], axis=1)


def _gate_params(w_gates, gate_b):
    D = w_gates.shape[0]
    w = w_gates.reshape(D, 3, N_KV_GROUPS, HEADS_PER_GROUP).transpose(0, 2, 1, 3).reshape(D, N_KV_GROUPS, 3 * HEADS_PER_GROUP)
    w = jnp.pad(w, ((0, 0), (0, 0), (0, LANES - 3 * HEADS_PER_GROUP))).reshape(D, N_KV_GROUPS * LANES)
    b = gate_b.reshape(3, N_KV_GROUPS, HEADS_PER_GROUP).transpose(1, 0, 2).reshape(N_KV_GROUPS, 3 * HEADS_PER_GROUP)
    b = jnp.pad(b, ((0, 0), (0, LANES - 3 * HEADS_PER_GROUP))).reshape(1, N_KV_GROUPS * LANES)
    return w.astype(BF16), b


def _chunked_groups(t):
    S = t.shape[0]
    return (t.reshape(S // CMP_STRIDE, CMP_STRIDE, N_KV_GROUPS, HEAD_DIM)
            .transpose(2, 0, 1, 3).reshape(N_KV_GROUPS, S // CMP_STRIDE, CMP_STRIDE * HEAD_DIM))


def _row_tile(S, want):
    t = min(S, want)
    assert S % t == 0
    return t


def kernel(x, mem, norm_mix, norm_mem_q, norm_mem_kv, norm_ffn, norm_final, nsa_w_in, nsa_gate_b, nsa_pe_k, nsa_pe_v, nsa_cmp_k_w1, nsa_cmp_k_w2, nsa_cmp_v_w1, nsa_cmp_v_w2, nsa_w_out, pool_w, pool_b, pool_scale, mem_wq, mem_wk, mem_wv, mem_wo, ffn_w_gate, ffn_w_up, ffn_w_down, moe_router, moe_w_gate, moe_w_up, moe_w_down):
    B, S, D = x.shape
    assert B == 1 and S % WINDOW == 0 and S >= SLC_BLOCK * N_SLC
    h = x.reshape(S, D)
    memf = mem.reshape(mem.shape[1], D)
    row = lambda v: v.reshape(1, -1)

    n_main = N_HEADS * HEAD_DIM + 6 * KV_WIDTH
    w_in = nsa_w_in[0]
    w_gate, b_gate = _gate_params(w_in[:, n_main:], nsa_gate_b[0])
    cos, sin = _rope_tables(jnp.arange(S))
    proj, gates = _nsa_proj(h, row(norm_mix[0]), w_in[:, :n_main].astype(BF16), w_gate, b_gate, cos, sin,
                            _row_tile(S, 1024))
    q_w = N_HEADS * HEAD_DIM
    n_chunks = S // CMP_STRIDE
    cmp_pos = jnp.arange(n_chunks) * CMP_STRIDE + CMP_BLOCK - 1
    ccos, csin = _rope_tables(cmp_pos)
    kc = _compress(_chunked_groups(proj[:, q_w:q_w + KV_WIDTH]), nsa_pe_k[0].reshape(1, -1),
                   nsa_cmp_k_w1[0].astype(BF16), nsa_cmp_k_w2[0].astype(BF16), ccos, csin, True)
    vc = _compress(_chunked_groups(proj[:, q_w + KV_WIDTH:q_w + 2 * KV_WIDTH]), nsa_pe_v[0].reshape(1, -1),
                   nsa_cmp_v_w1[0].astype(BF16), nsa_cmp_v_w2[0].astype(BF16), ccos, csin, False)
    attn = _nsa_attention(proj, gates, kc, vc)
    h = _matmul_residual(attn, nsa_w_out[0].astype(BF16), h, _row_tile(S, 512))

    def mem_layer(h, i):
        k, v = _mem_kv(memf, row(norm_mem_kv[i]), mem_wk[i].astype(BF16), mem_wv[i].astype(BF16))
        return _mem_attn(h, row(norm_mem_q[i]), mem_wq[i].astype(BF16), k, v, mem_wo[i].astype(BF16),
                         _row_tile(S, 512))

    h = mem_layer(h, 0)
    h = _ffn(h, row(norm_ffn[0]), ffn_w_gate[0].astype(BF16), ffn_w_up[0].astype(BF16),
             ffn_w_down[0].astype(BF16), _row_tile(S, 1024), 512)

    h = _pool_mixer(h, row(norm_mix[1]), pool_w[0].astype(BF16), row(pool_b[0]), row(pool_scale[0]),
                    _row_tile(S, 512))
    h = mem_layer(h, 1)

    r = jnp.pad(moe_router[0], ((0, 0), (0, LANES - N_EXPERTS)))
    r_hi = r.astype(BF16)
    r_lo = (r - r_hi.astype(F32)).astype(BF16)
    u, idx, gate_w = _router(h, row(norm_ffn[1]), r_hi, r_lo, _row_tile(S, 512))
    tok_buf, blk_exp, n_used, dest = _moe_dispatch_indices(idx[:, :TOP_K])
    y = _moe_experts(tok_buf, blk_exp, n_used, u, moe_w_gate[0].astype(BF16), moe_w_up[0].astype(BF16),
                     moe_w_down[0].astype(BF16), 512)
    out = _moe_combine(dest, h, gate_w, row(norm_final), y, _row_tile(S, 256))
    return out.reshape(B, S, D)
```

```python
import functools

import numpy as np
import jax
import jax.numpy as jnp
from jax import lax
from jax.experimental import pallas as pl
from jax.experimental.pallas import tpu as pltpu

F32 = jnp.float32
BF16 = jnp.bfloat16

N_HEADS = 16
HEAD_DIM = 128
N_KV_GROUPS = 4
HEADS_PER_GROUP = 4
KV_WIDTH = N_KV_GROUPS * HEAD_DIM
CMP_BLOCK = 32
CMP_STRIDE = 16
SLC_BLOCK = 64
RATIO = SLC_BLOCK // CMP_STRIDE
N_SLC = 16
WINDOW = 512
Q_BLOCK = 128
ROPE_THETA = 10000.0
POOL_WINDOWS = (2, 4, 8, 16)
POOL_GROUP = 512
MEM_HEADS = 4
MEM_HEAD_DIM = 128
N_EXPERTS = 8
TOP_K = 2
MOE_BLOCK = 512
EPS = 1e-6
NEG_INF = -1e30
FORCE = 1e9
REMOVED = -3e38
LOG2E = float(np.log2(np.e))

LANES = 128
VMEM_LIMIT = 56 * 1024 * 1024

SEL_KEY_TILE = 512


def _cparams(sem):
    return pltpu.CompilerParams(dimension_semantics=sem, vmem_limit_bytes=VMEM_LIMIT)


def _rms_rows(xf, g):
    r = lax.rsqrt(jnp.mean(xf * xf, axis=-1, keepdims=True) + EPS)
    return (xf * r) * g


def _dot(a, b):
    return jnp.dot(a, b, preferred_element_type=F32)


def _dot_t(a, b):
    return lax.dot_general(a, b, (((1,), (1,)), ((), ())), preferred_element_type=F32)


def _proj_kernel(x_ref, g_ref, w_ref, wgate_ref, bgate_ref, cos_ref, sin_ref, o_ref, gates_ref, u_ref):
    j = pl.program_id(1)

    @pl.when(j == 0)
    def _():
        ub = _rms_rows(x_ref[...], g_ref[...]).astype(BF16)
        u_ref[...] = ub
        gates_ref[...] = jax.nn.sigmoid(_dot(ub, wgate_ref[...]) + bgate_ref[...])

    acc = _dot(u_ref[...], w_ref[...])
    n_q_tiles = N_HEADS * HEAD_DIM // KV_WIDTH
    is_q = j < n_q_tiles
    is_rope = is_q | (j == n_q_tiles + 2) | (j == n_q_tiles + 4)

    @pl.when(is_rope)
    def _():
        scale = jnp.where(is_q, HEAD_DIM ** -0.5 * LOG2E, 1.0).astype(F32)
        cos = cos_ref[...]
        sin = sin_ref[...]
        for h in range(KV_WIDTH // HEAD_DIM):
            xh = acc[:, h * HEAD_DIM:(h + 1) * HEAD_DIM]
            rot = pltpu.roll(xh, HEAD_DIM // 2, axis=1)
            o_ref[:, h * HEAD_DIM:(h + 1) * HEAD_DIM] = ((xh * cos + rot * sin) * scale).astype(BF16)

    @pl.when(jnp.logical_not(is_rope))
    def _():
        o_ref[...] = acc.astype(BF16)


def _nsa_proj(x, g, w_main, w_gate, b_gate, cos, sin, tm):
    S, D = x.shape
    n_main = w_main.shape[1]
    tn = KV_WIDTH
    n_gate = w_gate.shape[1]
    return pl.pallas_call(
        _proj_kernel,
        grid=(S // tm, n_main // tn),
        in_specs=[
            pl.BlockSpec((tm, D), lambda i, j: (i, 0)),
            pl.BlockSpec((1, D), lambda i, j: (0, 0)),
            pl.BlockSpec((D, tn), lambda i, j: (0, j)),
            pl.BlockSpec((D, n_gate), lambda i, j: (0, 0)),
            pl.BlockSpec((1, n_gate), lambda i, j: (0, 0)),
            pl.BlockSpec((tm, HEAD_DIM), lambda i, j: (i, 0)),
            pl.BlockSpec((tm, HEAD_DIM), lambda i, j: (i, 0)),
        ],
        out_specs=[
            pl.BlockSpec((tm, tn), lambda i, j: (i, j)),
            pl.BlockSpec((tm, n_gate), lambda i, j: (i, 0)),
        ],
        out_shape=[
            jax.ShapeDtypeStruct((S, n_main), BF16),
            jax.ShapeDtypeStruct((S, n_gate), F32),
        ],
        scratch_shapes=[pltpu.VMEM((tm, D), BF16)],
        compiler_params=_cparams(("arbitrary", "arbitrary")),
        name="nsa_proj",
    )(x, g, w_main, w_gate, b_gate, cos, sin)


def _gelu_tanh(x):
    return 0.5 * x * (1.0 + jnp.tanh(np.sqrt(2.0 / np.pi).astype(np.float32) * (x + 0.044715 * (x * x * x))))


def _compress_kernel(t_ref, pe_ref, w1_ref, w2_ref, cos_ref, sin_ref, o_ref, *, apply_rope):
    half = CMP_STRIDE * HEAD_DIM
    t = t_ref[0].astype(F32)
    n = t.shape[0]
    a = _dot((t + pe_ref[:, :half]).astype(BF16), w1_ref[:half, :])
    b = _dot((t + pe_ref[:, half:]).astype(BF16), w1_ref[half:, :])
    hid = a + pltpu.roll(b, n - 1, axis=0)
    out = _dot(_gelu_tanh(hid).astype(BF16), w2_ref[...])
    if apply_rope:
        rot = pltpu.roll(out, HEAD_DIM // 2, axis=1)
        out = out * cos_ref[...] + rot * sin_ref[...]
    o_ref[0] = out.astype(BF16)


def _compress(t, pe_flat, w1, w2, cos, sin, apply_rope):
    G, n, width = t.shape
    hidden = w1.shape[1]
    out_block = (1, n, HEAD_DIM)
    return pl.pallas_call(
        functools.partial(_compress_kernel, apply_rope=apply_rope),
        grid=(G,),
        in_specs=[
            pl.BlockSpec((1, n, width), lambda g: (g, 0, 0)),
            pl.BlockSpec((1, 2 * width), lambda g: (0, 0)),
            pl.BlockSpec((2 * width, hidden), lambda g: (0, 0)),
            pl.BlockSpec((hidden, HEAD_DIM), lambda g: (0, 0)),
            pl.BlockSpec((n, HEAD_DIM), lambda g: (0, 0)),
            pl.BlockSpec((n, HEAD_DIM), lambda g: (0, 0)),
        ],
        out_specs=pl.BlockSpec(out_block, lambda g: (g, 0, 0)),
        out_shape=jax.ShapeDtypeStruct((G,) + out_block[1:], BF16),
        compiler_params=_cparams(("arbitrary",)),
        name="nsa_compress_k" if apply_rope else "nsa_compress_v",
    )(t, pe_flat, w1, w2, cos, sin)


def _softmax_cols(s, ok):
    s = jnp.where(ok, s, NEG_INF)
    e = jnp.where(ok, jnp.exp2(s - jnp.max(s, axis=0, keepdims=True)), 0.0)
    l = jnp.sum(e, axis=0, keepdims=True)
    return e * (1.0 / jnp.where(l > 0.0, l, 1.0))


def _nsa_attn_kernel(q_ref, gates_ref, kc_ref, vct_ref, ks_ref, vst_ref, wselt_ref, *rest, n_sel):
    n_win = WINDOW // Q_BLOCK + 1
    kw_refs = rest[:n_win]
    vwt_refs = rest[n_win:2 * n_win]
    o_ref = rest[2 * n_win]
    sel_ref, s0_ref, s1_ref, p0_ref, p1_ref, acc_ref = rest[2 * n_win + 1:]

    b = pl.program_id(1)
    t0 = b * Q_BLOCK
    Q = Q_BLOCK
    H = HEADS_PER_GROUP
    C = H * Q
    q = q_ref[...]
    qt = jnp.concatenate(
        [q[:, h * HEAD_DIM:(h + 1) * HEAD_DIM].astype(F32).T.astype(BF16) for h in range(H)], axis=1)

    kc = kc_ref[0]
    n_cmp = kc.shape[0]
    s_c = _dot(kc, qt)
    qpos_c = t0 + (lax.broadcasted_iota(jnp.int32, (n_cmp, C), 1) & (Q - 1))
    cpos = lax.broadcasted_iota(jnp.int32, (n_cmp, C), 0) * CMP_STRIDE + (CMP_BLOCK - 1)
    p_c = _softmax_cols(s_c, cpos <= qpos_c)
    o_c = _dot(vct_ref[0], p_c.astype(BF16))

    imp = (p_c[:, 0:Q] + p_c[:, Q:2 * Q]) + p_c[:, 2 * Q:3 * Q] + p_c[:, 3 * Q:4 * Q]
    imp_hi = imp.astype(BF16)
    imp_lo = (imp - imp_hi.astype(F32)).astype(BF16)
    wselt = wselt_ref[...]
    s_blk = _dot(wselt, imp_hi) + _dot(wselt, imp_lo)
    nsp = s_blk.shape[0]
    blk = lax.broadcasted_iota(jnp.int32, (nsp, Q), 0)
    qp = t0 + lax.broadcasted_iota(jnp.int32, (nsp, Q), 1)
    cur = qp >> 6
    valid = blk * SLC_BLOCK <= qp
    forced = (blk == 0) | (blk == cur) | (blk == cur - 1)
    s_blk = jnp.where(forced, FORCE, jnp.where(valid, s_blk, NEG_INF))
    blk_f = blk.astype(F32)

    def pick(_, carry):
        s, sel = carry
        m = jnp.max(s, axis=0, keepdims=True)
        first = jnp.min(jnp.where(s == m, blk_f, float(nsp)), axis=0, keepdims=True)
        hit = blk_f == first
        return jnp.where(hit, REMOVED, s), jnp.where(hit, 1.0, sel)

    _, sel = lax.fori_loop(0, n_sel, pick, (s_blk, jnp.zeros((nsp, Q), F32)))
    sel_ref[...] = sel

    T = SEL_KEY_TILE
    bpt = T // SLC_BLOCK
    n_tiles = (t0 + Q + T - 1) // T
    qp_t = t0 + lax.broadcasted_iota(jnp.int32, (T, Q), 1)
    kofs = lax.broadcasted_iota(jnp.int32, (T, Q), 0)

    last_tile = ks_ref.shape[0] // T - 1
    last_flags = sel_ref.shape[0] - bpt

    def scores(kt):
        k0 = pl.multiple_of(jnp.minimum(kt, last_tile) * T, T)
        return _dot(ks_ref[pl.ds(k0, T), :], qt)

    def tile_step(kt, s_cur, s_nxt, p_cur, p_prv, carry):
        alpha_prev, m, l = carry
        s_nxt[...] = scores(kt + 1)
        acc_ref[...] = alpha_prev * acc_ref[...] + _dot(vst_ref[0, jnp.clip(kt - 1, 0, last_tile)], p_prv[...])
        f0 = pl.multiple_of(jnp.minimum(kt * bpt, last_flags), bpt)
        flags = sel_ref[pl.ds(f0, bpt), :]
        picked = jnp.concatenate(
            [jnp.broadcast_to(flags[j:j + 1, :], (SLC_BLOCK, Q)) for j in range(bpt)], axis=0)
        ok = (picked > 0.5) & (kofs + kt * T <= qp_t)
        bias = jnp.where(ok, 0.0, NEG_INF)
        s = s_cur[...] + jnp.concatenate([bias] * H, axis=1)
        m_new = jnp.maximum(m, jnp.max(s, axis=0, keepdims=True))
        alpha = jnp.exp2(m - m_new)
        p = jnp.exp2(s - m_new)
        l = alpha * l + jnp.sum(p, axis=0, keepdims=True)
        p_cur[...] = p.astype(BF16)
        return alpha, m_new, l

    def pair_step(j, carry):
        carry = tile_step(2 * j, s0_ref, s1_ref, p0_ref, p1_ref, carry)
        return tile_step(2 * j + 1, s1_ref, s0_ref, p1_ref, p0_ref, carry)

    s0_ref[...] = scores(0)
    p1_ref[...] = jnp.zeros_like(p1_ref)
    acc_ref[...] = jnp.zeros_like(acc_ref)
    n_pairs = (n_tiles + 1) // 2
    alpha_last, _, l_s = lax.fori_loop(
        0, n_pairs, pair_step,
        (jnp.ones((1, C), F32), jnp.full((1, C), NEG_INF, F32), jnp.zeros((1, C), F32)))
    acc_s = alpha_last * acc_ref[...] + _dot(vst_ref[0, jnp.minimum(2 * n_pairs - 1, last_tile)], p1_ref[...])
    o_s = acc_s * (1.0 / l_s)

    kw = jnp.concatenate([r[...] for r in kw_refs], axis=0)
    vwt = jnp.concatenate([r[0] for r in vwt_refs], axis=1)
    nw = WINDOW + Q
    qpos_w = t0 + (lax.broadcasted_iota(jnp.int32, (nw, C), 1) & (Q - 1))
    kpos_w = t0 - WINDOW + lax.broadcasted_iota(jnp.int32, (nw, C), 0)
    diff = qpos_w - kpos_w
    p_w = _softmax_cols(_dot(kw, qt), (diff >= 0) & (diff < WINDOW) & (kpos_w >= 0))
    o_w = _dot(vwt, p_w.astype(BF16))

    gt = gates_ref[...].T
    for h in range(H):
        cols = slice(h * Q, (h + 1) * Q)
        o = (gt[h:h + 1] * o_c[:, cols] + gt[H + h:H + h + 1] * o_s[:, cols]
             + gt[2 * H + h:2 * H + h + 1] * o_w[:, cols])
        o_ref[:, h * HEAD_DIM:(h + 1) * HEAD_DIM] = o.T.astype(BF16)


def _selection_matrix_t(n_cmp, nsp):
    j = np.arange(nsp)[:, None]
    c = np.arange(n_cmp)[None, :]
    d = c - RATIO * j
    w = np.where((d == -1) | (d == RATIO - 1), 1.0, np.where((d >= 0) & (d < RATIO - 1), 2.0, 0.0))
    return jnp.asarray(w, BF16)


def _nsa_attention(proj, gates, kc, vct):
    S = proj.shape[0]
    G = N_KV_GROUPS
    T = SEL_KEY_TILE
    nb = S // Q_BLOCK
    n_cmp = kc.shape[1]
    ns = S // SLC_BLOCK
    nsp = -(-ns // LANES) * LANES
    n_sel = min(N_SLC, ns)
    wselt = _selection_matrix_t(n_cmp, nsp)
    n_win = WINDOW // Q_BLOCK + 1
    cpb = KV_WIDTH // HEAD_DIM
    q_w = N_HEADS * HEAD_DIM
    q_cols = q_w // HEAD_DIM
    ks_col, kw_col = q_cols + 2 * cpb, q_cols + 4 * cpb
    vst = proj[:, q_w + 3 * KV_WIDTH:q_w + 4 * KV_WIDTH].reshape(S // T, T, G, HEAD_DIM).transpose(2, 0, 3, 1)
    vwt = proj[:, q_w + 5 * KV_WIDTH:q_w + 6 * KV_WIDTH].reshape(S, G, HEAD_DIM).transpose(1, 2, 0)

    def win_block(b, i):
        return jnp.maximum(b - (n_win - 1) + i, 0)

    in_specs = [
        pl.BlockSpec((Q_BLOCK, KV_WIDTH), lambda g, b: (b, g)),
        pl.BlockSpec((Q_BLOCK, LANES), lambda g, b: (b, g)),
        pl.BlockSpec((1, n_cmp, HEAD_DIM), lambda g, b: (g, 0, 0)),
        pl.BlockSpec((1, HEAD_DIM, n_cmp), lambda g, b: (g, 0, 0)),
        pl.BlockSpec((S, HEAD_DIM), lambda g, b: (0, ks_col + g)),
        pl.BlockSpec((1, S // T, HEAD_DIM, T), lambda g, b: (g, 0, 0, 0)),
        pl.BlockSpec(wselt.shape, lambda g, b: (0, 0)),
    ]
    in_specs += [pl.BlockSpec((Q_BLOCK, HEAD_DIM), functools.partial(lambda g, b, i: (win_block(b, i), kw_col + g), i=i))
                 for i in range(n_win)]
    in_specs += [pl.BlockSpec((1, HEAD_DIM, Q_BLOCK), functools.partial(lambda g, b, i: (g, 0, win_block(b, i)), i=i))
                 for i in range(n_win)]
    args = [proj, gates, kc, vct, proj, vst, wselt] + [proj] * n_win + [vwt] * n_win
    return pl.pallas_call(
        functools.partial(_nsa_attn_kernel, n_sel=n_sel),
        grid=(G, nb),
        in_specs=in_specs,
        out_specs=pl.BlockSpec((Q_BLOCK, KV_WIDTH), lambda g, b: (b, g)),
        out_shape=jax.ShapeDtypeStruct((S, N_HEADS * HEAD_DIM), BF16),
        scratch_shapes=[pltpu.VMEM((nsp, Q_BLOCK), F32),
                        pltpu.VMEM((T, KV_WIDTH), F32), pltpu.VMEM((T, KV_WIDTH), F32),
                        pltpu.VMEM((T, KV_WIDTH), BF16), pltpu.VMEM((T, KV_WIDTH), BF16),
                        pltpu.VMEM((HEAD_DIM, KV_WIDTH), F32)],
        compiler_params=_cparams(("arbitrary", "arbitrary")),
        name="nsa_attention",
    )(*args)


def _matmul_residual_kernel(a_ref, w_ref, res_ref, o_ref):
    o_ref[...] = res_ref[...] + _dot(a_ref[...], w_ref[...])


def _matmul_residual(a, w, res, tm):
    S, K = a.shape
    N = w.shape[1]
    return pl.pallas_call(
        _matmul_residual_kernel,
        grid=(S // tm,),
        in_specs=[
            pl.BlockSpec((tm, K), lambda i: (i, 0)),
            pl.BlockSpec((K, N), lambda i: (0, 0)),
            pl.BlockSpec((tm, N), lambda i: (i, 0)),
        ],
        out_specs=pl.BlockSpec((tm, N), lambda i: (i, 0)),
        out_shape=jax.ShapeDtypeStruct((S, N), F32),
        compiler_params=_cparams(("arbitrary",)),
        name="out_proj_residual",
    )(a, w, res)


def _mem_kv_kernel(mem_ref, g_ref, wk_ref, wv_ref, k_ref, v_ref):
    mb = _rms_rows(mem_ref[...], g_ref[...]).astype(BF16)
    k_ref[...] = _dot(mb, wk_ref[...]).astype(BF16)
    v_ref[...] = _dot(mb, wv_ref[...]).astype(BF16)


def _mem_kv(mem, g, wk, wv):
    M, D = mem.shape
    W = wk.shape[1]
    full = lambda shape: pl.BlockSpec(shape, lambda i: (0, 0))
    return pl.pallas_call(
        _mem_kv_kernel,
        grid=(1,),
        in_specs=[full((M, D)), full((1, D)), full((D, W)), full((D, W))],
        out_specs=[full((M, W)), full((M, W))],
        out_shape=[jax.ShapeDtypeStruct((M, W), BF16)] * 2,
        compiler_params=_cparams(("arbitrary",)),
        name="mem_kv",
    )(mem, g, wk, wv)


def _mem_attn_kernel(h_ref, g_ref, wq_ref, k_ref, v_ref, wo_ref, o_ref):
    h = h_ref[...]
    ub = _rms_rows(h, g_ref[...]).astype(BF16)
    q = (_dot(ub, wq_ref[...]) * (MEM_HEAD_DIM ** -0.5)).astype(BF16)
    outs = []
    for hd in range(MEM_HEADS):
        cols = slice(hd * MEM_HEAD_DIM, (hd + 1) * MEM_HEAD_DIM)
        s = _dot_t(q[:, cols], k_ref[:, cols])
        e = jnp.exp(s - jnp.max(s, axis=1, keepdims=True))
        p = e / jnp.sum(e, axis=1, keepdims=True)
        outs.append(_dot(p.astype(BF16), v_ref[:, cols]).astype(BF16))
    o = jnp.concatenate(outs, axis=1)
    o_ref[...] = h + _dot(o, wo_ref[...])


def _mem_attn(h, g, wq, k, v, wo, tm):
    S, D = h.shape
    W = wq.shape[1]
    M = k.shape[0]
    full = lambda shape: pl.BlockSpec(shape, lambda i: (0, 0))
    return pl.pallas_call(
        _mem_attn_kernel,
        grid=(S // tm,),
        in_specs=[pl.BlockSpec((tm, D), lambda i: (i, 0)), full((1, D)), full((D, W)),
                  full((M, W)), full((M, W)), full((W, D))],
        out_specs=pl.BlockSpec((tm, D), lambda i: (i, 0)),
        out_shape=jax.ShapeDtypeStruct((S, D), F32),
        compiler_params=_cparams(("arbitrary",)),
        name="mem_attention",
    )(h, g, wq, k, v, wo)


def _swiglu_tile(ub, wg, wu, wd):
    gate = _dot(ub, wg)
    up = _dot(ub, wu)
    act = (gate * jax.nn.sigmoid(gate)) * up
    return _dot(act.astype(BF16), wd)


def _ffn_kernel(h_ref, g_ref, wg_ref, wu_ref, wd_ref, o_ref, u_ref):
    f = pl.program_id(1)

    @pl.when(f == 0)
    def _():
        h = h_ref[...]
        u_ref[...] = _rms_rows(h, g_ref[...]).astype(BF16)
        o_ref[...] = h

    o_ref[...] += _swiglu_tile(u_ref[...], wg_ref[...], wu_ref[...], wd_ref[...])


def _ffn(h, g, wg, wu, wd, tm, tf):
    S, D = h.shape
    F = wg.shape[1]
    return pl.pallas_call(
        _ffn_kernel,
        grid=(S // tm, F // tf),
        in_specs=[
            pl.BlockSpec((tm, D), lambda i, f: (i, 0)),
            pl.BlockSpec((1, D), lambda i, f: (0, 0)),
            pl.BlockSpec((D, tf), lambda i, f: (0, f)),
            pl.BlockSpec((D, tf), lambda i, f: (0, f)),
            pl.BlockSpec((tf, D), lambda i, f: (f, 0)),
        ],
        out_specs=pl.BlockSpec((tm, D), lambda i, f: (i, 0)),
        out_shape=jax.ShapeDtypeStruct((S, D), F32),
        scratch_shapes=[pltpu.VMEM((tm, D), BF16)],
        compiler_params=_cparams(("arbitrary", "arbitrary")),
        name="dense_swiglu",
    )(h, g, wg, wu, wd)


POOL_HALO = 16


def _pool_kernel(h_ref, halo_ref, g_ref, w_ref, b_ref, scale_ref, o_ref, ext_ref):
    i = pl.program_id(0)
    tm = h_ref.shape[0]
    h = h_ref[...]
    g = g_ref[...]
    u = _rms_rows(h, g)
    halo = _rms_rows(halo_ref[...], g)
    ext_ref[0:POOL_HALO, :] = jnp.where(i > 0, halo, 0.0)
    ext_ref[POOL_HALO:, :] = u
    pos = i * tm + lax.broadcasted_iota(jnp.int32, (tm, 1), 0)
    for gi, w in enumerate(POOL_WINDOWS):
        cols = slice(gi * POOL_GROUP, (gi + 1) * POOL_GROUP)
        tot = u[:, cols]
        for k in range(1, w):
            tot = tot + ext_ref[POOL_HALO - k:POOL_HALO - k + tm, cols]
        cnt = jnp.minimum(pos + 1, w).astype(F32)
        d = tot / cnt - u[:, cols]
        z = _dot(d.astype(BF16), w_ref[gi]) + b_ref[:, cols]
        o_ref[:, cols] = h[:, cols] + z * scale_ref[:, cols]


def _pool_mixer(h, g, w, b, scale, tm):
    S, D = h.shape
    ratio = tm // POOL_HALO
    return pl.pallas_call(
        _pool_kernel,
        grid=(S // tm,),
        in_specs=[
            pl.BlockSpec((tm, D), lambda i: (i, 0)),
            pl.BlockSpec((POOL_HALO, D), lambda i: (jnp.maximum(i * ratio - 1, 0), 0)),
            pl.BlockSpec((1, D), lambda i: (0, 0)),
            pl.BlockSpec(w.shape, lambda i: (0, 0, 0)),
            pl.BlockSpec((1, D), lambda i: (0, 0)),
            pl.BlockSpec((1, D), lambda i: (0, 0)),
        ],
        out_specs=pl.BlockSpec((tm, D), lambda i: (i, 0)),
        out_shape=jax.ShapeDtypeStruct((S, D), F32),
        scratch_shapes=[pltpu.VMEM((tm + POOL_HALO, D), F32)],
        compiler_params=_cparams(("arbitrary",)),
        name="pool_mixer",
    )(h, h, g, w, b, scale)


def _router_kernel(h_ref, g_ref, rhi_ref, rlo_ref, u_ref, idx_ref, w_ref):
    u = _rms_rows(h_ref[...], g_ref[...])
    u_ref[...] = u
    u_hi = u.astype(BF16)
    u_lo = (u - u_hi.astype(F32)).astype(BF16)
    rhi = rhi_ref[...]
    logits = (_dot(u_hi, rhi) + _dot(u_lo, rhi)) + _dot(u_hi, rlo_ref[...])
    lane = lax.broadcasted_iota(jnp.int32, logits.shape, 1)
    lane_f = lane.astype(F32)
    s = jnp.where(lane < N_EXPERTS, logits, REMOVED)
    m1 = jnp.max(s, axis=1, keepdims=True)
    i1 = jnp.min(jnp.where(s == m1, lane_f, float(LANES)), axis=1, keepdims=True)
    s = jnp.where(lane_f == i1, REMOVED, s)
    m2 = jnp.max(s, axis=1, keepdims=True)
    i2 = jnp.min(jnp.where(s == m2, lane_f, float(LANES)), axis=1, keepdims=True)
    e2 = jnp.exp(m2 - m1)
    den = 1.0 + e2
    idx_ref[...] = jnp.where(lane == 0, i1, jnp.where(lane == 1, i2, 0.0)).astype(jnp.int32)
    w_ref[...] = jnp.where(lane == 0, 1.0 / den, jnp.where(lane == 1, e2 / den, 0.0))


def _router(h, g, r_hi, r_lo, tm):
    S, D = h.shape
    return pl.pallas_call(
        _router_kernel,
        grid=(S // tm,),
        in_specs=[
            pl.BlockSpec((tm, D), lambda i: (i, 0)),
            pl.BlockSpec((1, D), lambda i: (0, 0)),
            pl.BlockSpec((D, LANES), lambda i: (0, 0)),
            pl.BlockSpec((D, LANES), lambda i: (0, 0)),
        ],
        out_specs=[
            pl.BlockSpec((tm, D), lambda i: (i, 0)),
            pl.BlockSpec((tm, LANES), lambda i: (i, 0)),
            pl.BlockSpec((tm, LANES), lambda i: (i, 0)),
        ],
        out_shape=[
            jax.ShapeDtypeStruct((S, D), F32),
            jax.ShapeDtypeStruct((S, LANES), jnp.int32),
            jax.ShapeDtypeStruct((S, LANES), F32),
        ],
        compiler_params=_cparams(("arbitrary",)),
        name="moe_router",
    )(h, g, r_hi, r_lo)


def _row_copy(src_hbm, dst_vmem, sem, src_row, dst_row):
    return pltpu.make_async_copy(src_hbm.at[pl.ds(src_row, 1), :], dst_vmem.at[pl.ds(dst_row, 1), :], sem)


def _moe_expert_kernel(tok_ref, exp_ref, nused_ref, u_hbm, wg_ref, wu_ref, wd_ref, o_ref, x_ref, xb_ref, sem):
    i = pl.program_id(0)
    f = pl.program_id(1)
    rows = x_ref.shape[0]
    active = i < nused_ref[0]

    @pl.when(active & (f == 0))
    def _():
        def start(r, c):
            _row_copy(u_hbm, x_ref, sem, tok_ref[i * rows + r], r).start()
            return c

        lax.fori_loop(0, rows, start, 0)

        def wait(r, c):
            _row_copy(u_hbm, x_ref, sem, 0, r).wait()
            return c

        lax.fori_loop(0, rows, wait, 0)
        xb_ref[...] = x_ref[...].astype(BF16)
        o_ref[...] = jnp.zeros_like(o_ref)

    @pl.when(active)
    def _():
        o_ref[...] += _swiglu_tile(xb_ref[...], wg_ref[0], wu_ref[0], wd_ref[0])

    @pl.when(jnp.logical_not(active) & (f == 0))
    def _():
        o_ref[...] = jnp.zeros_like(o_ref)


def _moe_experts(tok, blk_exp, n_used, u, wg, wu, wd, tf):
    N, D = u.shape
    n_blk = blk_exp.shape[0]
    F = wg.shape[2]
    rows = MOE_BLOCK
    n_f = F // tf

    def ftile(i, f, nu):
        return jnp.where(i < nu[0], f, n_f - 1)

    grid_spec = pltpu.PrefetchScalarGridSpec(
        num_scalar_prefetch=3,
        grid=(n_blk, n_f),
        in_specs=[
            pl.BlockSpec(memory_space=pl.ANY),
            pl.BlockSpec((1, D, tf), lambda i, f, tok, ex, nu: (ex[i], 0, ftile(i, f, nu))),
            pl.BlockSpec((1, D, tf), lambda i, f, tok, ex, nu: (ex[i], 0, ftile(i, f, nu))),
            pl.BlockSpec((1, tf, D), lambda i, f, tok, ex, nu: (ex[i], ftile(i, f, nu), 0)),
        ],
        out_specs=pl.BlockSpec((rows, D), lambda i, f, tok, ex, nu: (i, 0)),
        scratch_shapes=[pltpu.VMEM((rows, D), F32), pltpu.VMEM((rows, D), BF16), pltpu.SemaphoreType.DMA(())],
    )
    return pl.pallas_call(
        _moe_expert_kernel,
        grid_spec=grid_spec,
        out_shape=jax.ShapeDtypeStruct((n_blk * rows, D), F32),
        compiler_params=_cparams(("arbitrary", "arbitrary")),
        name="moe_experts",
    )(tok, blk_exp, n_used, u, wg, wu, wd)


def _moe_combine_kernel(pos_ref, h_ref, w_ref, gfin_ref, y_hbm, o_ref, r_ref, sem):
    i = pl.program_id(0)
    tm = h_ref.shape[0]

    def start(r, c):
        for k in range(TOP_K):
            _row_copy(y_hbm, r_ref.at[k], sem, pos_ref[(i * tm + r) * TOP_K + k], r).start()
        return c

    lax.fori_loop(0, tm, start, 0)

    def wait(r, c):
        for k in range(TOP_K):
            _row_copy(y_hbm, r_ref.at[k], sem, 0, r).wait()
        return c

    lax.fori_loop(0, tm, wait, 0)
    w = w_ref[...]
    y = h_ref[...] + (r_ref[0] * w[:, 0:1] + r_ref[1] * w[:, 1:2])
    o_ref[...] = _rms_rows(y, gfin_ref[...])


def _moe_combine(pos, h, w, g_final, y, tm):
    S, D = h.shape
    grid_spec = pltpu.PrefetchScalarGridSpec(
        num_scalar_prefetch=1,
        grid=(S // tm,),
        in_specs=[
            pl.BlockSpec((tm, D), lambda i, pos: (i, 0)),
            pl.BlockSpec((tm, LANES), lambda i, pos: (i, 0)),
            pl.BlockSpec((1, D), lambda i, pos: (0, 0)),
            pl.BlockSpec(memory_space=pl.ANY),
        ],
        out_specs=pl.BlockSpec((tm, D), lambda i, pos: (i, 0)),
        scratch_shapes=[pltpu.VMEM((TOP_K, tm, D), F32), pltpu.SemaphoreType.DMA(())],
    )
    return pl.pallas_call(
        _moe_combine_kernel,
        grid_spec=grid_spec,
        out_shape=jax.ShapeDtypeStruct((S, D), F32),
        compiler_params=_cparams(("arbitrary",)),
        name="moe_combine_norm",
    )(pos, h, w, g_final, y)


def _moe_dispatch_indices(top_idx):
    N = top_idx.shape[0]
    e_flat = top_idx.reshape(-1)
    onehot = (e_flat[:, None] == jnp.arange(N_EXPERTS, dtype=jnp.int32)[None, :]).astype(jnp.int32)
    csum = jnp.cumsum(onehot, axis=0)
    counts = csum[-1]
    rank = jnp.take_along_axis(csum, e_flat[:, None], axis=1)[:, 0] - 1
    padded = (counts + MOE_BLOCK - 1) // MOE_BLOCK * MOE_BLOCK
    pend = jnp.cumsum(padded)
    poff = pend - padded
    dest = poff[e_flat] + rank
    n_blk = (N * TOP_K + MOE_BLOCK - 1) // MOE_BLOCK + N_EXPERTS
    cap = n_blk * MOE_BLOCK
    t_flat = jnp.repeat(jnp.arange(N, dtype=jnp.int32), TOP_K)
    tok_buf = jnp.zeros((cap,), jnp.int32).at[dest].set(t_flat)
    blk_start = jnp.arange(n_blk, dtype=jnp.int32) * MOE_BLOCK
    blk_exp = jnp.minimum(jnp.sum(pend[None, :] <= blk_start[:, None], axis=1), N_EXPERTS - 1).astype(jnp.int32)
    n_used = (pend[-1] // MOE_BLOCK).astype(jnp.int32).reshape(1)
    return tok_buf, blk_exp, n_used, dest.astype(jnp.int32)


def _rope_tables(pos):
    half = HEAD_DIM // 2
    inv = ROPE_THETA ** (-jnp.arange(half, dtype=F32) / half)
    ang = pos.astype(F32)[:, None] * inv[None, :]
    cos, sin = jnp.cos(ang), jnp.sin(ang)
    return jnp.concatenate([cos, cos], axis=1), jnp.concatenate([-sin, sin], axis=1)


def _gate_params(w_gates, gate_b):
    D = w_gates.shape[0]
    w = w_gates.reshape(D, 3, N_KV_GROUPS, HEADS_PER_GROUP).transpose(0, 2, 1, 3).reshape(D, N_KV_GROUPS, 3 * HEADS_PER_GROUP)
    w = jnp.pad(w, ((0, 0), (0, 0), (0, LANES - 3 * HEADS_PER_GROUP))).reshape(D, N_KV_GROUPS * LANES)
    b = gate_b.reshape(3, N_KV_GROUPS, HEADS_PER_GROUP).transpose(1, 0, 2).reshape(N_KV_GROUPS, 3 * HEADS_PER_GROUP)
    b = jnp.pad(b, ((0, 0), (0, LANES - 3 * HEADS_PER_GROUP))).reshape(1, N_KV_GROUPS * LANES)
    return w.astype(BF16), b


def _chunked_groups(t):
    S = t.shape[0]
    return (t.reshape(S // CMP_STRIDE, CMP_STRIDE, N_KV_GROUPS, HEAD_DIM)
            .transpose(2, 0, 1, 3).reshape(N_KV_GROUPS, S // CMP_STRIDE, CMP_STRIDE * HEAD_DIM))


def _row_tile(S, want):
    t = min(S, want)
    assert S % t == 0
    return t


def kernel(x, mem, norm_mix, norm_mem_q, norm_mem_kv, norm_ffn, norm_final, nsa_w_in, nsa_gate_b, nsa_pe_k, nsa_pe_v, nsa_cmp_k_w1, nsa_cmp_k_w2, nsa_cmp_v_w1, nsa_cmp_v_w2, nsa_w_out, pool_w, pool_b, pool_scale, mem_wq, mem_wk, mem_wv, mem_wo, ffn_w_gate, ffn_w_up, ffn_w_down, moe_router, moe_w_gate, moe_w_up, moe_w_down):
    B, S, D = x.shape
    assert B == 1 and S % WINDOW == 0 and S >= SLC_BLOCK * N_SLC
    h = x.reshape(S, D)
    memf = mem.reshape(mem.shape[1], D)
    row = lambda v: v.reshape(1, -1)

    n_main = N_HEADS * HEAD_DIM + 6 * KV_WIDTH
    w_in = nsa_w_in[0]
    w_gate, b_gate = _gate_params(w_in[:, n_main:], nsa_gate_b[0])
    cos, sin = _rope_tables(jnp.arange(S))
    proj, gates = _nsa_proj(h, row(norm_mix[0]), w_in[:, :n_main].astype(BF16), w_gate, b_gate, cos, sin,
                            _row_tile(S, 1024))
    q_w = N_HEADS * HEAD_DIM
    n_chunks = S // CMP_STRIDE
    cmp_pos = jnp.arange(n_chunks) * CMP_STRIDE + CMP_BLOCK - 1
    ccos, csin = _rope_tables(cmp_pos)
    kc = _compress(_chunked_groups(proj[:, q_w:q_w + KV_WIDTH]), nsa_pe_k[0].reshape(1, -1),
                   nsa_cmp_k_w1[0].astype(BF16), nsa_cmp_k_w2[0].astype(BF16), ccos, csin, True)
    vct = _compress(_chunked_groups(proj[:, q_w + KV_WIDTH:q_w + 2 * KV_WIDTH]), nsa_pe_v[0].reshape(1, -1),
                    nsa_cmp_v_w1[0].astype(BF16), nsa_cmp_v_w2[0].astype(BF16), ccos, csin, False).transpose(0, 2, 1)
    attn = _nsa_attention(proj, gates, kc, vct)
    h = _matmul_residual(attn, nsa_w_out[0].astype(BF16), h, _row_tile(S, 512))

    def mem_layer(h, i):
        k, v = _mem_kv(memf, row(norm_mem_kv[i]), mem_wk[i].astype(BF16), mem_wv[i].astype(BF16))
        return _mem_attn(h, row(norm_mem_q[i]), mem_wq[i].astype(BF16), k, v, mem_wo[i].astype(BF16),
                         _row_tile(S, 512))

    h = mem_layer(h, 0)
    h = _ffn(h, row(norm_ffn[0]), ffn_w_gate[0].astype(BF16), ffn_w_up[0].astype(BF16),
             ffn_w_down[0].astype(BF16), _row_tile(S, 1024), 512)

    h = _pool_mixer(h, row(norm_mix[1]), pool_w[0].astype(BF16), row(pool_b[0]), row(pool_scale[0]),
                    _row_tile(S, 512))
    h = mem_layer(h, 1)

    r = jnp.pad(moe_router[0], ((0, 0), (0, LANES - N_EXPERTS)))
    r_hi = r.astype(BF16)
    r_lo = (r - r_hi.astype(F32)).astype(BF16)
    u, idx, gate_w = _router(h, row(norm_ffn[1]), r_hi, r_lo, _row_tile(S, 512))
    tok_buf, blk_exp, n_used, dest = _moe_dispatch_indices(idx[:, :TOP_K])
    y = _moe_experts(tok_buf, blk_exp, n_used, u, moe_w_gate[0].astype(BF16), moe_w_up[0].astype(BF16),
                     moe_w_down[0].astype(BF16), 512)
    out = _moe_combine(dest, h, gate_w, row(norm_final), y, _row_tile(S, 256))
    return out.reshape(B, S, D)
```

```python
import functools

import numpy as np
import jax
import jax.numpy as jnp
from jax import lax
from jax.experimental import pallas as pl
from jax.experimental.pallas import tpu as pltpu

F32 = jnp.float32
BF16 = jnp.bfloat16

N_HEADS = 16
HEAD_DIM = 128
N_KV_GROUPS = 4
HEADS_PER_GROUP = 4
KV_WIDTH = N_KV_GROUPS * HEAD_DIM
CMP_BLOCK = 32
CMP_STRIDE = 16
SLC_BLOCK = 64
RATIO = SLC_BLOCK // CMP_STRIDE
N_SLC = 16
WINDOW = 512
Q_BLOCK = 128
ROPE_THETA = 10000.0
POOL_WINDOWS = (2, 4, 8, 16)
POOL_GROUP = 512
MEM_HEADS = 4
MEM_HEAD_DIM = 128
N_EXPERTS = 8
TOP_K = 2
MOE_BLOCK = 512
EPS = 1e-6
NEG_INF = -1e30
FORCE = 1e9
REMOVED = -3e38
LOG2E = float(np.log2(np.e))

LANES = 128
VMEM_LIMIT = 56 * 1024 * 1024

SEL_KEY_TILE = 512


def _cparams(sem):
    return pltpu.CompilerParams(dimension_semantics=sem, vmem_limit_bytes=VMEM_LIMIT)


def _rms_rows(xf, g):
    r = lax.rsqrt(jnp.mean(xf * xf, axis=-1, keepdims=True) + EPS)
    return (xf * r) * g


def _dot(a, b):
    return jnp.dot(a, b, preferred_element_type=F32)


def _dot_t(a, b):
    return lax.dot_general(a, b, (((1,), (1,)), ((), ())), preferred_element_type=F32)


def _proj_kernel(x_ref, g_ref, w_ref, wgate_ref, bgate_ref, cos_ref, sin_ref, o_ref, gates_ref, u_ref):
    j = pl.program_id(1)

    @pl.when(j == 0)
    def _():
        ub = _rms_rows(x_ref[...], g_ref[...]).astype(BF16)
        u_ref[...] = ub
        gates_ref[...] = jax.nn.sigmoid(_dot(ub, wgate_ref[...]) + bgate_ref[...])

    acc = _dot(u_ref[...], w_ref[...])
    n_q_tiles = N_HEADS * HEAD_DIM // KV_WIDTH
    is_q = j < n_q_tiles
    is_rope = is_q | (j == n_q_tiles + 2) | (j == n_q_tiles + 4)

    @pl.when(is_rope)
    def _():
        scale = jnp.where(is_q, HEAD_DIM ** -0.5 * LOG2E, 1.0).astype(F32)
        cos = cos_ref[...]
        sin = sin_ref[...]
        for h in range(KV_WIDTH // HEAD_DIM):
            xh = acc[:, h * HEAD_DIM:(h + 1) * HEAD_DIM]
            rot = pltpu.roll(xh, HEAD_DIM // 2, axis=1)
            o_ref[:, h * HEAD_DIM:(h + 1) * HEAD_DIM] = ((xh * cos + rot * sin) * scale).astype(BF16)

    @pl.when(jnp.logical_not(is_rope))
    def _():
        o_ref[...] = acc.astype(BF16)


def _nsa_proj(x, g, w_main, w_gate, b_gate, cos, sin, tm):
    S, D = x.shape
    n_main = w_main.shape[1]
    tn = KV_WIDTH
    n_gate = w_gate.shape[1]
    return pl.pallas_call(
        _proj_kernel,
        grid=(S // tm, n_main // tn),
        in_specs=[
            pl.BlockSpec((tm, D), lambda i, j: (i, 0)),
            pl.BlockSpec((1, D), lambda i, j: (0, 0)),
            pl.BlockSpec((D, tn), lambda i, j: (0, j)),
            pl.BlockSpec((D, n_gate), lambda i, j: (0, 0)),
            pl.BlockSpec((1, n_gate), lambda i, j: (0, 0)),
            pl.BlockSpec((tm, HEAD_DIM), lambda i, j: (i, 0)),
            pl.BlockSpec((tm, HEAD_DIM), lambda i, j: (i, 0)),
        ],
        out_specs=[
            pl.BlockSpec((tm, tn), lambda i, j: (i, j)),
            pl.BlockSpec((tm, n_gate), lambda i, j: (i, 0)),
        ],
        out_shape=[
            jax.ShapeDtypeStruct((S, n_main), BF16),
            jax.ShapeDtypeStruct((S, n_gate), F32),
        ],
        scratch_shapes=[pltpu.VMEM((tm, D), BF16)],
        compiler_params=_cparams(("arbitrary", "arbitrary")),
        name="nsa_proj",
    )(x, g, w_main, w_gate, b_gate, cos, sin)


def _gelu_tanh(x):
    return 0.5 * x * (1.0 + jnp.tanh(np.sqrt(2.0 / np.pi).astype(np.float32) * (x + 0.044715 * (x * x * x))))


def _compress_kernel(t_ref, pe_ref, w1_ref, w2_ref, cos_ref, sin_ref, o_ref, *, apply_rope):
    half = CMP_STRIDE * HEAD_DIM
    t = t_ref[0].astype(F32)
    n = t.shape[0]
    a = _dot((t + pe_ref[:, :half]).astype(BF16), w1_ref[:half, :])
    b = _dot((t + pe_ref[:, half:]).astype(BF16), w1_ref[half:, :])
    hid = a + pltpu.roll(b, n - 1, axis=0)
    out = _dot(_gelu_tanh(hid).astype(BF16), w2_ref[...])
    if apply_rope:
        rot = pltpu.roll(out, HEAD_DIM // 2, axis=1)
        out = out * cos_ref[...] + rot * sin_ref[...]
    o_ref[0] = out.astype(BF16)


def _compress(t, pe_flat, w1, w2, cos, sin, apply_rope):
    G, n, width = t.shape
    hidden = w1.shape[1]
    out_block = (1, n, HEAD_DIM)
    return pl.pallas_call(
        functools.partial(_compress_kernel, apply_rope=apply_rope),
        grid=(G,),
        in_specs=[
            pl.BlockSpec((1, n, width), lambda g: (g, 0, 0)),
            pl.BlockSpec((1, 2 * width), lambda g: (0, 0)),
            pl.BlockSpec((2 * width, hidden), lambda g: (0, 0)),
            pl.BlockSpec((hidden, HEAD_DIM), lambda g: (0, 0)),
            pl.BlockSpec((n, HEAD_DIM), lambda g: (0, 0)),
            pl.BlockSpec((n, HEAD_DIM), lambda g: (0, 0)),
        ],
        out_specs=pl.BlockSpec(out_block, lambda g: (g, 0, 0)),
        out_shape=jax.ShapeDtypeStruct((G,) + out_block[1:], BF16),
        compiler_params=_cparams(("arbitrary",)),
        name="nsa_compress_k" if apply_rope else "nsa_compress_v",
    )(t, pe_flat, w1, w2, cos, sin)


def _softmax_cols(s):
    e = jnp.exp2(s - jnp.max(s, axis=0, keepdims=True))
    return e, jnp.sum(e, axis=0, keepdims=True)


def _tile_heads(x):
    return jnp.concatenate([x] * HEADS_PER_GROUP, axis=1)


SEL_PAD = 16


def _nsa_attn_kernel(q_ref, gates_ref, kc_ref, vct_ref, ks_ref, vst_ref, wselt_ref, onehot_ref, *rest, n_sel):
    n_win = WINDOW // Q_BLOCK + 1
    kw_refs = rest[:n_win]
    vwt_refs = rest[n_win:2 * n_win]
    o_ref = rest[2 * n_win]
    gall_ref, s0_ref, s1_ref, p0_ref, p1_ref, acc_ref = rest[2 * n_win + 1:]

    b = pl.program_id(1)
    t0 = b * Q_BLOCK
    Q = Q_BLOCK
    H = HEADS_PER_GROUP
    C = H * Q
    q = q_ref[...]
    qt = jnp.concatenate(
        [q[:, h * HEAD_DIM:(h + 1) * HEAD_DIM].astype(F32).T.astype(BF16) for h in range(H)], axis=1)

    kc = kc_ref[0]
    n_cmp = kc.shape[0]
    qi_c = lax.broadcasted_iota(jnp.int32, (n_cmp, Q), 1)
    cpos = lax.broadcasted_iota(jnp.int32, (n_cmp, Q), 0) * CMP_STRIDE + (CMP_BLOCK - 1)
    bias_c = jnp.where(cpos <= t0 + qi_c, 0.0, NEG_INF)
    e_c, l_c = _softmax_cols(_dot(kc, qt) + _tile_heads(bias_c))
    qi_row = lax.broadcasted_iota(jnp.int32, (1, C), 1) & (Q - 1)
    p_c = e_c * jnp.where(t0 + qi_row >= CMP_BLOCK - 1, 1.0 / l_c, 0.0)
    o_c = _dot(vct_ref[0], p_c.astype(BF16))

    imp = (p_c[:, 0:Q] + p_c[:, Q:2 * Q]) + p_c[:, 2 * Q:3 * Q] + p_c[:, 3 * Q:4 * Q]
    imp_hi = imp.astype(BF16)
    imp_lo = (imp - imp_hi.astype(F32)).astype(BF16)
    wselt = wselt_ref[...]
    s_blk = _dot(wselt, imp_hi) + _dot(wselt, imp_lo)
    nsp = s_blk.shape[0]
    blk = lax.broadcasted_iota(jnp.int32, (nsp, Q), 0)
    qp = t0 + lax.broadcasted_iota(jnp.int32, (nsp, Q), 1)
    cur = qp >> 6
    valid = blk * SLC_BLOCK <= qp
    forced = (blk == 0) | (blk == cur) | (blk == cur - 1)
    s_blk = jnp.where(forced, FORCE, jnp.where(valid, s_blk, NEG_INF))
    blk_f = blk.astype(F32)

    def pick(_, carry):
        s, sel = carry
        m = jnp.max(s, axis=0, keepdims=True)
        first = jnp.min(jnp.where(s == m, blk_f, float(nsp)), axis=0, keepdims=True)
        hit = blk_f == first
        return jnp.where(hit, REMOVED, s), jnp.where(hit, 1.0, sel)

    _, sel = lax.fori_loop(0, n_sel, pick, (s_blk, jnp.zeros((nsp, Q), F32)))

    T = SEL_KEY_TILE
    bpt = T // SLC_BLOCK
    n_tiles = (t0 + Q + T - 1) // T
    gall_ref[0:SEL_PAD, :] = jnp.full((SEL_PAD, C), NEG_INF, F32)
    gall_ref[SEL_PAD:SEL_PAD + nsp, :] = _tile_heads(jnp.where((sel > 0.5) & valid, 0.0, NEG_INF))
    gall_ref[SEL_PAD + nsp:, :] = jnp.full((SEL_PAD, C), NEG_INF, F32)
    onehot = onehot_ref[(b + 1) & 3]
    zero_rows = jnp.zeros((HEAD_DIM - 2 * bpt, C), BF16)
    k_rows = ks_ref.shape[1]
    v_blocks = vst_ref.shape[1]

    def scores(kt):
        start = pl.multiple_of(jnp.clip(t0 - kt * T, 0, k_rows - T), Q)
        first_blk = 2 * b + 2 - bpt * (kt + 1) + SEL_PAD
        w0 = pl.multiple_of(jnp.maximum((first_blk >> 3) << 3, 0), 8)
        table = gall_ref[pl.ds(w0, 2 * bpt), :].astype(BF16)
        lhs = jnp.concatenate([ks_ref[0, pl.ds(start, T), :], onehot], axis=1)
        rhs = jnp.concatenate([qt, table, zero_rows], axis=0)
        return _dot(lhs, rhs)

    def values(kt):
        vb = jnp.clip(b - (T // Q) * kt, 0, v_blocks - T // Q)
        return jnp.concatenate([vst_ref[0, vb + c] for c in range(T // Q)], axis=1)

    def tile_step(kt, s_cur, s_nxt, p_cur, p_prv, carry):
        alpha_prev, m, l = carry
        acc_ref[...] = alpha_prev * acc_ref[...] + _dot(values(kt - 1), p_prv[...])
        s_nxt[...] = scores(kt + 1)
        s = s_cur[...]
        m_new = jnp.maximum(m, jnp.max(s, axis=0, keepdims=True))
        alpha = jnp.exp2(m - m_new)
        p = jnp.exp2(s - m_new)
        l = alpha * l + jnp.sum(p, axis=0, keepdims=True)
        p_cur[...] = p.astype(BF16)
        return alpha, m_new, l

    def pair_step(j, carry):
        carry = tile_step(2 * j, s0_ref, s1_ref, p0_ref, p1_ref, carry)
        return tile_step(2 * j + 1, s1_ref, s0_ref, p1_ref, p0_ref, carry)

    s_first = scores(0)
    key_j = lax.broadcasted_iota(jnp.int32, (Q, Q), 0)
    qry_i = lax.broadcasted_iota(jnp.int32, (Q, Q), 1)
    s0_ref[0:T - Q, :] = s_first[0:T - Q]
    s0_ref[T - Q:, :] = s_first[T - Q:] + _tile_heads(jnp.where(key_j <= qry_i, 0.0, NEG_INF))
    p1_ref[...] = jnp.zeros_like(p1_ref)
    acc_ref[...] = jnp.zeros_like(acc_ref)
    n_pairs = (n_tiles + 1) // 2
    alpha_last, _, l_s = lax.fori_loop(
        0, n_pairs, pair_step,
        (jnp.ones((1, C), F32), jnp.full((1, C), NEG_INF, F32), jnp.zeros((1, C), F32)))
    acc_s = alpha_last * acc_ref[...] + _dot(values(2 * n_pairs - 1), p1_ref[...])
    o_s = acc_s * (1.0 / l_s)

    kw = jnp.concatenate([r[...] for r in kw_refs], axis=0)
    vwt = jnp.concatenate([r[0] for r in vwt_refs], axis=1)
    nw = WINDOW + Q
    row_w = lax.broadcasted_iota(jnp.int32, (nw, Q), 0)
    diff = lax.broadcasted_iota(jnp.int32, (nw, Q), 1) - row_w + WINDOW
    ok_w = (diff >= 0) & (diff < WINDOW) & (row_w >= WINDOW - t0)
    e_w, l_w = _softmax_cols(_dot(kw, qt) + _tile_heads(jnp.where(ok_w, 0.0, NEG_INF)))
    o_w = _dot(vwt, e_w.astype(BF16)) * (1.0 / l_w)

    gt = gates_ref[...].T
    for h in range(H):
        cols = slice(h * Q, (h + 1) * Q)
        o = (gt[h:h + 1] * o_c[:, cols] + gt[H + h:H + h + 1] * o_s[:, cols]
             + gt[2 * H + h:2 * H + h + 1] * o_w[:, cols])
        o_ref[:, h * HEAD_DIM:(h + 1) * HEAD_DIM] = o.T.astype(BF16)


def _selection_matrix_t(n_cmp, nsp):
    j = np.arange(nsp)[:, None]
    c = np.arange(n_cmp)[None, :]
    d = c - RATIO * j
    w = np.where((d == -1) | (d == RATIO - 1), 1.0, np.where((d >= 0) & (d < RATIO - 1), 2.0, 0.0))
    return jnp.asarray(w, BF16)


def _nsa_attention(proj, gates, kc, vct):
    S = proj.shape[0]
    G = N_KV_GROUPS
    T = SEL_KEY_TILE
    nb = S // Q_BLOCK
    n_cmp = kc.shape[1]
    ns = S // SLC_BLOCK
    nsp = -(-ns // LANES) * LANES
    n_sel = min(N_SLC, ns)
    wselt = _selection_matrix_t(n_cmp, nsp)
    n_win = WINDOW // Q_BLOCK + 1
    cpb = KV_WIDTH // HEAD_DIM
    q_w = N_HEADS * HEAD_DIM
    q_cols = q_w // HEAD_DIM
    kw_col = q_cols + 4 * cpb
    pad = T - Q_BLOCK
    ksp = jnp.pad(proj[:, q_w + 2 * KV_WIDTH:q_w + 3 * KV_WIDTH], ((pad, 0), (0, 0)))
    ksp = ksp.reshape(S + pad, G, HEAD_DIM).transpose(1, 0, 2)
    vsp = jnp.pad(proj[:, q_w + 3 * KV_WIDTH:q_w + 4 * KV_WIDTH], ((pad, 0), (0, 0)))
    vst = vsp.reshape((S + pad) // Q_BLOCK, Q_BLOCK, G, HEAD_DIM).transpose(2, 0, 3, 1)
    vwt = proj[:, q_w + 5 * KV_WIDTH:q_w + 6 * KV_WIDTH].reshape(S, G, HEAD_DIM).transpose(1, 2, 0)
    r = np.arange(T)[None, :, None] // SLC_BLOCK
    onehot = jnp.asarray(np.arange(LANES)[None, None, :] == 2 * np.arange(4)[:, None, None] + r, BF16)

    def win_block(b, i):
        return jnp.maximum(b - (n_win - 1) + i, 0)

    in_specs = [
        pl.BlockSpec((Q_BLOCK, KV_WIDTH), lambda g, b: (b, g)),
        pl.BlockSpec((Q_BLOCK, LANES), lambda g, b: (b, g)),
        pl.BlockSpec((1, n_cmp, HEAD_DIM), lambda g, b: (g, 0, 0)),
        pl.BlockSpec((1, HEAD_DIM, n_cmp), lambda g, b: (g, 0, 0)),
        pl.BlockSpec((1,) + ksp.shape[1:], lambda g, b: (g, 0, 0)),
        pl.BlockSpec((1,) + vst.shape[1:], lambda g, b: (g, 0, 0, 0)),
        pl.BlockSpec(wselt.shape, lambda g, b: (0, 0)),
        pl.BlockSpec(onehot.shape, lambda g, b: (0, 0, 0)),
    ]
    in_specs += [pl.BlockSpec((Q_BLOCK, HEAD_DIM), functools.partial(lambda g, b, i: (win_block(b, i), kw_col + g), i=i))
                 for i in range(n_win)]
    in_specs += [pl.BlockSpec((1, HEAD_DIM, Q_BLOCK), functools.partial(lambda g, b, i: (g, 0, win_block(b, i)), i=i))
                 for i in range(n_win)]
    args = [proj, gates, kc, vct, ksp, vst, wselt, onehot] + [proj] * n_win + [vwt] * n_win
    return pl.pallas_call(
        functools.partial(_nsa_attn_kernel, n_sel=n_sel),
        grid=(G, nb),
        in_specs=in_specs,
        out_specs=pl.BlockSpec((Q_BLOCK, KV_WIDTH), lambda g, b: (b, g)),
        out_shape=jax.ShapeDtypeStruct((S, N_HEADS * HEAD_DIM), BF16),
        scratch_shapes=[pltpu.VMEM((nsp + 2 * SEL_PAD, KV_WIDTH), F32),
                        pltpu.VMEM((T, KV_WIDTH), F32), pltpu.VMEM((T, KV_WIDTH), F32),
                        pltpu.VMEM((T, KV_WIDTH), BF16), pltpu.VMEM((T, KV_WIDTH), BF16),
                        pltpu.VMEM((HEAD_DIM, KV_WIDTH), F32)],
        compiler_params=_cparams(("arbitrary", "arbitrary")),
        name="nsa_attention",
    )(*args)


def _matmul_residual_kernel(a_ref, w_ref, res_ref, o_ref):
    o_ref[...] = res_ref[...] + _dot(a_ref[...], w_ref[...])


def _matmul_residual(a, w, res, tm):
    S, K = a.shape
    N = w.shape[1]
    return pl.pallas_call(
        _matmul_residual_kernel,
        grid=(S // tm,),
        in_specs=[
            pl.BlockSpec((tm, K), lambda i: (i, 0)),
            pl.BlockSpec((K, N), lambda i: (0, 0)),
            pl.BlockSpec((tm, N), lambda i: (i, 0)),
        ],
        out_specs=pl.BlockSpec((tm, N), lambda i: (i, 0)),
        out_shape=jax.ShapeDtypeStruct((S, N), F32),
        compiler_params=_cparams(("arbitrary",)),
        name="out_proj_residual",
    )(a, w, res)


def _mem_kv_kernel(mem_ref, g_ref, wk_ref, wv_ref, k_ref, v_ref):
    mb = _rms_rows(mem_ref[...], g_ref[...]).astype(BF16)
    k_ref[...] = _dot(mb, wk_ref[...]).astype(BF16)
    v_ref[...] = _dot(mb, wv_ref[...]).astype(BF16)


def _mem_kv(mem, g, wk, wv):
    M, D = mem.shape
    W = wk.shape[1]
    full = lambda shape: pl.BlockSpec(shape, lambda i: (0, 0))
    return pl.pallas_call(
        _mem_kv_kernel,
        grid=(1,),
        in_specs=[full((M, D)), full((1, D)), full((D, W)), full((D, W))],
        out_specs=[full((M, W)), full((M, W))],
        out_shape=[jax.ShapeDtypeStruct((M, W), BF16)] * 2,
        compiler_params=_cparams(("arbitrary",)),
        name="mem_kv",
    )(mem, g, wk, wv)


def _mem_attn_kernel(h_ref, g_ref, wq_ref, k_ref, v_ref, wo_ref, o_ref):
    h = h_ref[...]
    ub = _rms_rows(h, g_ref[...]).astype(BF16)
    q = (_dot(ub, wq_ref[...]) * (MEM_HEAD_DIM ** -0.5)).astype(BF16)
    outs = []
    for hd in range(MEM_HEADS):
        cols = slice(hd * MEM_HEAD_DIM, (hd + 1) * MEM_HEAD_DIM)
        s = _dot_t(q[:, cols], k_ref[:, cols])
        e = jnp.exp(s - jnp.max(s, axis=1, keepdims=True))
        p = e / jnp.sum(e, axis=1, keepdims=True)
        outs.append(_dot(p.astype(BF16), v_ref[:, cols]).astype(BF16))
    o = jnp.concatenate(outs, axis=1)
    o_ref[...] = h + _dot(o, wo_ref[...])


def _mem_attn(h, g, wq, k, v, wo, tm):
    S, D = h.shape
    W = wq.shape[1]
    M = k.shape[0]
    full = lambda shape: pl.BlockSpec(shape, lambda i: (0, 0))
    return pl.pallas_call(
        _mem_attn_kernel,
        grid=(S // tm,),
        in_specs=[pl.BlockSpec((tm, D), lambda i: (i, 0)), full((1, D)), full((D, W)),
                  full((M, W)), full((M, W)), full((W, D))],
        out_specs=pl.BlockSpec((tm, D), lambda i: (i, 0)),
        out_shape=jax.ShapeDtypeStruct((S, D), F32),
        compiler_params=_cparams(("arbitrary",)),
        name="mem_attention",
    )(h, g, wq, k, v, wo)


def _swiglu_tile(ub, wg, wu, wd):
    gate = _dot(ub, wg)
    up = _dot(ub, wu)
    act = (gate * jax.nn.sigmoid(gate)) * up
    return _dot(act.astype(BF16), wd)


def _ffn_kernel(h_ref, g_ref, wg_ref, wu_ref, wd_ref, o_ref, u_ref):
    f = pl.program_id(1)

    @pl.when(f == 0)
    def _():
        h = h_ref[...]
        u_ref[...] = _rms_rows(h, g_ref[...]).astype(BF16)
        o_ref[...] = h

    o_ref[...] += _swiglu_tile(u_ref[...], wg_ref[...], wu_ref[...], wd_ref[...])


def _ffn(h, g, wg, wu, wd, tm, tf):
    S, D = h.shape
    F = wg.shape[1]
    return pl.pallas_call(
        _ffn_kernel,
        grid=(S // tm, F // tf),
        in_specs=[
            pl.BlockSpec((tm, D), lambda i, f: (i, 0)),
            pl.BlockSpec((1, D), lambda i, f: (0, 0)),
            pl.BlockSpec((D, tf), lambda i, f: (0, f)),
            pl.BlockSpec((D, tf), lambda i, f: (0, f)),
            pl.BlockSpec((tf, D), lambda i, f: (f, 0)),
        ],
        out_specs=pl.BlockSpec((tm, D), lambda i, f: (i, 0)),
        out_shape=jax.ShapeDtypeStruct((S, D), F32),
        scratch_shapes=[pltpu.VMEM((tm, D), BF16)],
        compiler_params=_cparams(("arbitrary", "arbitrary")),
        name="dense_swiglu",
    )(h, g, wg, wu, wd)


POOL_HALO = 16


def _pool_kernel(h_ref, halo_ref, g_ref, w_ref, b_ref, scale_ref, o_ref, ext_ref):
    i = pl.program_id(0)
    tm = h_ref.shape[0]
    h = h_ref[...]
    g = g_ref[...]
    u = _rms_rows(h, g)
    halo = _rms_rows(halo_ref[...], g)
    ext_ref[0:POOL_HALO, :] = jnp.where(i > 0, halo, 0.0)
    ext_ref[POOL_HALO:, :] = u
    pos = i * tm + lax.broadcasted_iota(jnp.int32, (tm, 1), 0)
    for gi, w in enumerate(POOL_WINDOWS):
        cols = slice(gi * POOL_GROUP, (gi + 1) * POOL_GROUP)
        tot = u[:, cols]
        for k in range(1, w):
            tot = tot + ext_ref[POOL_HALO - k:POOL_HALO - k + tm, cols]
        cnt = jnp.minimum(pos + 1, w).astype(F32)
        d = tot / cnt - u[:, cols]
        z = _dot(d.astype(BF16), w_ref[gi]) + b_ref[:, cols]
        o_ref[:, cols] = h[:, cols] + z * scale_ref[:, cols]


def _pool_mixer(h, g, w, b, scale, tm):
    S, D = h.shape
    ratio = tm // POOL_HALO
    return pl.pallas_call(
        _pool_kernel,
        grid=(S // tm,),
        in_specs=[
            pl.BlockSpec((tm, D), lambda i: (i, 0)),
            pl.BlockSpec((POOL_HALO, D), lambda i: (jnp.maximum(i * ratio - 1, 0), 0)),
            pl.BlockSpec((1, D), lambda i: (0, 0)),
            pl.BlockSpec(w.shape, lambda i: (0, 0, 0)),
            pl.BlockSpec((1, D), lambda i: (0, 0)),
            pl.BlockSpec((1, D), lambda i: (0, 0)),
        ],
        out_specs=pl.BlockSpec((tm, D), lambda i: (i, 0)),
        out_shape=jax.ShapeDtypeStruct((S, D), F32),
        scratch_shapes=[pltpu.VMEM((tm + POOL_HALO, D), F32)],
        compiler_params=_cparams(("arbitrary",)),
        name="pool_mixer",
    )(h, h, g, w, b, scale)


def _router_kernel(h_ref, g_ref, rhi_ref, rlo_ref, u_ref, idx_ref, w_ref):
    u = _rms_rows(h_ref[...], g_ref[...])
    u_ref[...] = u
    u_hi = u.astype(BF16)
    u_lo = (u - u_hi.astype(F32)).astype(BF16)
    rhi = rhi_ref[...]
    logits = (_dot(u_hi, rhi) + _dot(u_lo, rhi)) + _dot(u_hi, rlo_ref[...])
    lane = lax.broadcasted_iota(jnp.int32, logits.shape, 1)
    lane_f = lane.astype(F32)
    s = jnp.where(lane < N_EXPERTS, logits, REMOVED)
    m1 = jnp.max(s, axis=1, keepdims=True)
    i1 = jnp.min(jnp.where(s == m1, lane_f, float(LANES)), axis=1, keepdims=True)
    s = jnp.where(lane_f == i1, REMOVED, s)
    m2 = jnp.max(s, axis=1, keepdims=True)
    i2 = jnp.min(jnp.where(s == m2, lane_f, float(LANES)), axis=1, keepdims=True)
    e2 = jnp.exp(m2 - m1)
    den = 1.0 + e2
    idx_ref[...] = jnp.where(lane == 0, i1, jnp.where(lane == 1, i2, 0.0)).astype(jnp.int32)
    w_ref[...] = jnp.where(lane == 0, 1.0 / den, jnp.where(lane == 1, e2 / den, 0.0))


def _router(h, g, r_hi, r_lo, tm):
    S, D = h.shape
    return pl.pallas_call(
        _router_kernel,
        grid=(S // tm,),
        in_specs=[
            pl.BlockSpec((tm, D), lambda i: (i, 0)),
            pl.BlockSpec((1, D), lambda i: (0, 0)),
            pl.BlockSpec((D, LANES), lambda i: (0, 0)),
            pl.BlockSpec((D, LANES), lambda i: (0, 0)),
        ],
        out_specs=[
            pl.BlockSpec((tm, D), lambda i: (i, 0)),
            pl.BlockSpec((tm, LANES), lambda i: (i, 0)),
            pl.BlockSpec((tm, LANES), lambda i: (i, 0)),
        ],
        out_shape=[
            jax.ShapeDtypeStruct((S, D), F32),
            jax.ShapeDtypeStruct((S, LANES), jnp.int32),
            jax.ShapeDtypeStruct((S, LANES), F32),
        ],
        compiler_params=_cparams(("arbitrary",)),
        name="moe_router",
    )(h, g, r_hi, r_lo)


def _row_copy(src_hbm, dst_vmem, sem, src_row, dst_row):
    return pltpu.make_async_copy(src_hbm.at[pl.ds(src_row, 1), :], dst_vmem.at[pl.ds(dst_row, 1), :], sem)


def _moe_expert_kernel(tok_ref, exp_ref, nused_ref, u_hbm, wg_ref, wu_ref, wd_ref, o_ref, x_ref, xb_ref, sem):
    i = pl.program_id(0)
    f = pl.program_id(1)
    rows = x_ref.shape[0]
    active = i < nused_ref[0]

    @pl.when(active & (f == 0))
    def _():
        def start(r, c):
            _row_copy(u_hbm, x_ref, sem, tok_ref[i * rows + r], r).start()
            return c

        lax.fori_loop(0, rows, start, 0)

        def wait(r, c):
            _row_copy(u_hbm, x_ref, sem, 0, r).wait()
            return c

        lax.fori_loop(0, rows, wait, 0)
        xb_ref[...] = x_ref[...].astype(BF16)
        o_ref[...] = jnp.zeros_like(o_ref)

    @pl.when(active)
    def _():
        o_ref[...] += _swiglu_tile(xb_ref[...], wg_ref[0], wu_ref[0], wd_ref[0])

    @pl.when(jnp.logical_not(active) & (f == 0))
    def _():
        o_ref[...] = jnp.zeros_like(o_ref)


def _moe_experts(tok, blk_exp, n_used, u, wg, wu, wd, tf):
    N, D = u.shape
    n_blk = blk_exp.shape[0]
    F = wg.shape[2]
    rows = MOE_BLOCK
    n_f = F // tf

    def ftile(i, f, nu):
        return jnp.where(i < nu[0], f, n_f - 1)

    grid_spec = pltpu.PrefetchScalarGridSpec(
        num_scalar_prefetch=3,
        grid=(n_blk, n_f),
        in_specs=[
            pl.BlockSpec(memory_space=pl.ANY),
            pl.BlockSpec((1, D, tf), lambda i, f, tok, ex, nu: (ex[i], 0, ftile(i, f, nu))),
            pl.BlockSpec((1, D, tf), lambda i, f, tok, ex, nu: (ex[i], 0, ftile(i, f, nu))),
            pl.BlockSpec((1, tf, D), lambda i, f, tok, ex, nu: (ex[i], ftile(i, f, nu), 0)),
        ],
        out_specs=pl.BlockSpec((rows, D), lambda i, f, tok, ex, nu: (i, 0)),
        scratch_shapes=[pltpu.VMEM((rows, D), F32), pltpu.VMEM((rows, D), BF16), pltpu.SemaphoreType.DMA(())],
    )
    return pl.pallas_call(
        _moe_expert_kernel,
        grid_spec=grid_spec,
        out_shape=jax.ShapeDtypeStruct((n_blk * rows, D), F32),
        compiler_params=_cparams(("arbitrary", "arbitrary")),
        name="moe_experts",
    )(tok, blk_exp, n_used, u, wg, wu, wd)


def _moe_combine_kernel(pos_ref, h_ref, w_ref, gfin_ref, y_hbm, o_ref, r_ref, sem):
    i = pl.program_id(0)
    tm = h_ref.shape[0]

    def start(r, c):
        for k in range(TOP_K):
            _row_copy(y_hbm, r_ref.at[k], sem, pos_ref[(i * tm + r) * TOP_K + k], r).start()
        return c

    lax.fori_loop(0, tm, start, 0)

    def wait(r, c):
        for k in range(TOP_K):
            _row_copy(y_hbm, r_ref.at[k], sem, 0, r).wait()
        return c

    lax.fori_loop(0, tm, wait, 0)
    w = w_ref[...]
    y = h_ref[...] + (r_ref[0] * w[:, 0:1] + r_ref[1] * w[:, 1:2])
    o_ref[...] = _rms_rows(y, gfin_ref[...])


def _moe_combine(pos, h, w, g_final, y, tm):
    S, D = h.shape
    grid_spec = pltpu.PrefetchScalarGridSpec(
        num_scalar_prefetch=1,
        grid=(S // tm,),
        in_specs=[
            pl.BlockSpec((tm, D), lambda i, pos: (i, 0)),
            pl.BlockSpec((tm, LANES), lambda i, pos: (i, 0)),
            pl.BlockSpec((1, D), lambda i, pos: (0, 0)),
            pl.BlockSpec(memory_space=pl.ANY),
        ],
        out_specs=pl.BlockSpec((tm, D), lambda i, pos: (i, 0)),
        scratch_shapes=[pltpu.VMEM((TOP_K, tm, D), F32), pltpu.SemaphoreType.DMA(())],
    )
    return pl.pallas_call(
        _moe_combine_kernel,
        grid_spec=grid_spec,
        out_shape=jax.ShapeDtypeStruct((S, D), F32),
        compiler_params=_cparams(("arbitrary",)),
        name="moe_combine_norm",
    )(pos, h, w, g_final, y)


def _moe_dispatch_indices(top_idx):
    N = top_idx.shape[0]
    e_flat = top_idx.reshape(-1)
    onehot = (e_flat[:, None] == jnp.arange(N_EXPERTS, dtype=jnp.int32)[None, :]).astype(jnp.int32)
    csum = jnp.cumsum(onehot, axis=0)
    counts = csum[-1]
    rank = jnp.take_along_axis(csum, e_flat[:, None], axis=1)[:, 0] - 1
    padded = (counts + MOE_BLOCK - 1) // MOE_BLOCK * MOE_BLOCK
    pend = jnp.cumsum(padded)
    poff = pend - padded
    dest = poff[e_flat] + rank
    n_blk = (N * TOP_K + MOE_BLOCK - 1) // MOE_BLOCK + N_EXPERTS
    cap = n_blk * MOE_BLOCK
    t_flat = jnp.repeat(jnp.arange(N, dtype=jnp.int32), TOP_K)
    tok_buf = jnp.zeros((cap,), jnp.int32).at[dest].set(t_flat)
    blk_start = jnp.arange(n_blk, dtype=jnp.int32) * MOE_BLOCK
    blk_exp = jnp.minimum(jnp.sum(pend[None, :] <= blk_start[:, None], axis=1), N_EXPERTS - 1).astype(jnp.int32)
    n_used = (pend[-1] // MOE_BLOCK).astype(jnp.int32).reshape(1)
    return tok_buf, blk_exp, n_used, dest.astype(jnp.int32)


def _rope_tables(pos):
    half = HEAD_DIM // 2
    inv = ROPE_THETA ** (-jnp.arange(half, dtype=F32) / half)
    ang = pos.astype(F32)[:, None] * inv[None, :]
    cos, sin = jnp.cos(ang), jnp.sin(ang)
    return jnp.concatenate([cos, cos], axis=1), jnp.concatenate([-sin, sin], axis=1)


def _gate_params(w_gates, gate_b):
    D = w_gates.shape[0]
    w = w_gates.reshape(D, 3, N_KV_GROUPS, HEADS_PER_GROUP).transpose(0, 2, 1, 3).reshape(D, N_KV_GROUPS, 3 * HEADS_PER_GROUP)
    w = jnp.pad(w, ((0, 0), (0, 0), (0, LANES - 3 * HEADS_PER_GROUP))).reshape(D, N_KV_GROUPS * LANES)
    b = gate_b.reshape(3, N_KV_GROUPS, HEADS_PER_GROUP).transpose(1, 0, 2).reshape(N_KV_GROUPS, 3 * HEADS_PER_GROUP)
    b = jnp.pad(b, ((0, 0), (0, LANES - 3 * HEADS_PER_GROUP))).reshape(1, N_KV_GROUPS * LANES)
    return w.astype(BF16), b


def _chunked_groups(t):
    S = t.shape[0]
    return (t.reshape(S // CMP_STRIDE, CMP_STRIDE, N_KV_GROUPS, HEAD_DIM)
            .transpose(2, 0, 1, 3).reshape(N_KV_GROUPS, S // CMP_STRIDE, CMP_STRIDE * HEAD_DIM))


def _row_tile(S, want):
    t = min(S, want)
    assert S % t == 0
    return t


def kernel(x, mem, norm_mix, norm_mem_q, norm_mem_kv, norm_ffn, norm_final, nsa_w_in, nsa_gate_b, nsa_pe_k, nsa_pe_v, nsa_cmp_k_w1, nsa_cmp_k_w2, nsa_cmp_v_w1, nsa_cmp_v_w2, nsa_w_out, pool_w, pool_b, pool_scale, mem_wq, mem_wk, mem_wv, mem_wo, ffn_w_gate, ffn_w_up, ffn_w_down, moe_router, moe_w_gate, moe_w_up, moe_w_down):
    B, S, D = x.shape
    assert B == 1 and S % WINDOW == 0 and S >= SLC_BLOCK * N_SLC
    h = x.reshape(S, D)
    memf = mem.reshape(mem.shape[1], D)
    row = lambda v: v.reshape(1, -1)

    n_main = N_HEADS * HEAD_DIM + 6 * KV_WIDTH
    w_in = nsa_w_in[0]
    w_gate, b_gate = _gate_params(w_in[:, n_main:], nsa_gate_b[0])
    cos, sin = _rope_tables(jnp.arange(S))
    proj, gates = _nsa_proj(h, row(norm_mix[0]), w_in[:, :n_main].astype(BF16), w_gate, b_gate, cos, sin,
                            _row_tile(S, 1024))
    q_w = N_HEADS * HEAD_DIM
    n_chunks = S // CMP_STRIDE
    cmp_pos = jnp.arange(n_chunks) * CMP_STRIDE + CMP_BLOCK - 1
    ccos, csin = _rope_tables(cmp_pos)
    kc = _compress(_chunked_groups(proj[:, q_w:q_w + KV_WIDTH]), nsa_pe_k[0].reshape(1, -1),
                   nsa_cmp_k_w1[0].astype(BF16), nsa_cmp_k_w2[0].astype(BF16), ccos, csin, True)
    vct = _compress(_chunked_groups(proj[:, q_w + KV_WIDTH:q_w + 2 * KV_WIDTH]), nsa_pe_v[0].reshape(1, -1),
                    nsa_cmp_v_w1[0].astype(BF16), nsa_cmp_v_w2[0].astype(BF16), ccos, csin, False).transpose(0, 2, 1)
    attn = _nsa_attention(proj, gates, kc, vct)
    h = _matmul_residual(attn, nsa_w_out[0].astype(BF16), h, _row_tile(S, 512))

    def mem_layer(h, i):
        k, v = _mem_kv(memf, row(norm_mem_kv[i]), mem_wk[i].astype(BF16), mem_wv[i].astype(BF16))
        return _mem_attn(h, row(norm_mem_q[i]), mem_wq[i].astype(BF16), k, v, mem_wo[i].astype(BF16),
                         _row_tile(S, 512))

    h = mem_layer(h, 0)
    h = _ffn(h, row(norm_ffn[0]), ffn_w_gate[0].astype(BF16), ffn_w_up[0].astype(BF16),
             ffn_w_down[0].astype(BF16), _row_tile(S, 1024), 512)

    h = _pool_mixer(h, row(norm_mix[1]), pool_w[0].astype(BF16), row(pool_b[0]), row(pool_scale[0]),
                    _row_tile(S, 512))
    h = mem_layer(h, 1)

    r = jnp.pad(moe_router[0], ((0, 0), (0, LANES - N_EXPERTS)))
    r_hi = r.astype(BF16)
    r_lo = (r - r_hi.astype(F32)).astype(BF16)
    u, idx, gate_w = _router(h, row(norm_ffn[1]), r_hi, r_lo, _row_tile(S, 512))
    tok_buf, blk_exp, n_used, dest = _moe_dispatch_indices(idx[:, :TOP_K])
    y = _moe_experts(tok_buf, blk_exp, n_used, u, moe_w_gate[0].astype(BF16), moe_w_up[0].astype(BF16),
                     moe_w_down[0].astype(BF16), 512)
    out = _moe_combine(dest, h, gate_w, row(norm_final), y, _row_tile(S, 256))
    return out.reshape(B, S, D)
```

```python
import functools

import numpy as np
import jax
import jax.numpy as jnp
from jax import lax
from jax.experimental import pallas as pl
from jax.experimental.pallas import tpu as pltpu

F32 = jnp.float32
BF16 = jnp.bfloat16

N_HEADS = 16
HEAD_DIM = 128
N_KV_GROUPS = 4
HEADS_PER_GROUP = 4
KV_WIDTH = N_KV_GROUPS * HEAD_DIM
CMP_BLOCK = 32
CMP_STRIDE = 16
SLC_BLOCK = 64
RATIO = SLC_BLOCK // CMP_STRIDE
N_SLC = 16
WINDOW = 512
Q_BLOCK = 128
ROPE_THETA = 10000.0
POOL_WINDOWS = (2, 4, 8, 16)
POOL_GROUP = 512
MEM_HEADS = 4
MEM_HEAD_DIM = 128
N_EXPERTS = 8
TOP_K = 2
MOE_BLOCK = 512
EPS = 1e-6
NEG_INF = -1e30
FORCE = 1e9
REMOVED = -3e38
LOG2E = float(np.log2(np.e))

LANES = 128
VMEM_LIMIT = 56 * 1024 * 1024

SEL_KEY_TILE = 512


def _cparams(sem):
    return pltpu.CompilerParams(dimension_semantics=sem, vmem_limit_bytes=VMEM_LIMIT)


def _rms_rows(xf, g):
    r = lax.rsqrt(jnp.mean(xf * xf, axis=-1, keepdims=True) + EPS)
    return (xf * r) * g


def _dot(a, b):
    return jnp.dot(a, b, preferred_element_type=F32)


def _dot_t(a, b):
    return lax.dot_general(a, b, (((1,), (1,)), ((), ())), preferred_element_type=F32)


def _proj_kernel(x_ref, g_ref, w_ref, wgate_ref, bgate_ref, cos_ref, sin_ref, o_ref, gates_ref, u_ref):
    j = pl.program_id(1)

    @pl.when(j == 0)
    def _():
        ub = _rms_rows(x_ref[...], g_ref[...]).astype(BF16)
        u_ref[...] = ub
        gates_ref[...] = jax.nn.sigmoid(_dot(ub, wgate_ref[...]) + bgate_ref[...])

    acc = _dot(u_ref[...], w_ref[...])
    n_q_tiles = N_HEADS * HEAD_DIM // KV_WIDTH
    is_q = j < n_q_tiles
    is_rope = is_q | (j == n_q_tiles + 2) | (j == n_q_tiles + 4)

    @pl.when(is_rope)
    def _():
        scale = jnp.where(is_q, HEAD_DIM ** -0.5 * LOG2E, 1.0).astype(F32)
        cos = cos_ref[...]
        sin = sin_ref[...]
        for h in range(KV_WIDTH // HEAD_DIM):
            xh = acc[:, h * HEAD_DIM:(h + 1) * HEAD_DIM]
            rot = pltpu.roll(xh, HEAD_DIM // 2, axis=1)
            o_ref[:, h * HEAD_DIM:(h + 1) * HEAD_DIM] = ((xh * cos + rot * sin) * scale).astype(BF16)

    @pl.when(jnp.logical_not(is_rope))
    def _():
        o_ref[...] = acc.astype(BF16)


def _nsa_proj(x, g, w_main, w_gate, b_gate, cos, sin, tm):
    S, D = x.shape
    n_main = w_main.shape[1]
    tn = KV_WIDTH
    n_gate = w_gate.shape[1]
    return pl.pallas_call(
        _proj_kernel,
        grid=(S // tm, n_main // tn),
        in_specs=[
            pl.BlockSpec((tm, D), lambda i, j: (i, 0)),
            pl.BlockSpec((1, D), lambda i, j: (0, 0)),
            pl.BlockSpec((D, tn), lambda i, j: (0, j)),
            pl.BlockSpec((D, n_gate), lambda i, j: (0, 0)),
            pl.BlockSpec((1, n_gate), lambda i, j: (0, 0)),
            pl.BlockSpec((tm, HEAD_DIM), lambda i, j: (i, 0)),
            pl.BlockSpec((tm, HEAD_DIM), lambda i, j: (i, 0)),
        ],
        out_specs=[
            pl.BlockSpec((tm, tn), lambda i, j: (i, j)),
            pl.BlockSpec((tm, n_gate), lambda i, j: (i, 0)),
        ],
        out_shape=[
            jax.ShapeDtypeStruct((S, n_main), BF16),
            jax.ShapeDtypeStruct((S, n_gate), F32),
        ],
        scratch_shapes=[pltpu.VMEM((tm, D), BF16)],
        compiler_params=_cparams(("arbitrary", "arbitrary")),
        name="nsa_proj",
    )(x, g, w_main, w_gate, b_gate, cos, sin)


def _gelu_tanh(x):
    return 0.5 * x * (1.0 + jnp.tanh(np.sqrt(2.0 / np.pi).astype(np.float32) * (x + 0.044715 * (x * x * x))))


def _compress_kernel(t_ref, pe_ref, w1_ref, w2_ref, cos_ref, sin_ref, o_ref, *, apply_rope):
    half = CMP_STRIDE * HEAD_DIM
    t = t_ref[0].astype(F32)
    n = t.shape[0]
    a = _dot((t + pe_ref[:, :half]).astype(BF16), w1_ref[:half, :])
    b = _dot((t + pe_ref[:, half:]).astype(BF16), w1_ref[half:, :])
    hid = a + pltpu.roll(b, n - 1, axis=0)
    out = _dot(_gelu_tanh(hid).astype(BF16), w2_ref[...])
    if apply_rope:
        rot = pltpu.roll(out, HEAD_DIM // 2, axis=1)
        out = out * cos_ref[...] + rot * sin_ref[...]
    o_ref[0] = out.astype(BF16)


def _compress(t, pe_flat, w1, w2, cos, sin, apply_rope):
    G, n, width = t.shape
    hidden = w1.shape[1]
    out_block = (1, n, HEAD_DIM)
    return pl.pallas_call(
        functools.partial(_compress_kernel, apply_rope=apply_rope),
        grid=(G,),
        in_specs=[
            pl.BlockSpec((1, n, width), lambda g: (g, 0, 0)),
            pl.BlockSpec((1, 2 * width), lambda g: (0, 0)),
            pl.BlockSpec((2 * width, hidden), lambda g: (0, 0)),
            pl.BlockSpec((hidden, HEAD_DIM), lambda g: (0, 0)),
            pl.BlockSpec((n, HEAD_DIM), lambda g: (0, 0)),
            pl.BlockSpec((n, HEAD_DIM), lambda g: (0, 0)),
        ],
        out_specs=pl.BlockSpec(out_block, lambda g: (g, 0, 0)),
        out_shape=jax.ShapeDtypeStruct((G,) + out_block[1:], BF16),
        compiler_params=_cparams(("arbitrary",)),
        name="nsa_compress_k" if apply_rope else "nsa_compress_v",
    )(t, pe_flat, w1, w2, cos, sin)


def _softmax_cols(s):
    e = jnp.exp2(s - jnp.max(s, axis=0, keepdims=True))
    return e, jnp.sum(e, axis=0, keepdims=True)


def _tile_heads(x):
    return jnp.concatenate([x] * HEADS_PER_GROUP, axis=1)


SEL_PAD = 16


def _nsa_attn_kernel(q_ref, gates_ref, kc_ref, vct_ref, ks_ref, vst_ref, wselt_ref, onehot_ref, *rest, n_sel):
    n_win = WINDOW // Q_BLOCK + 1
    kw_refs = rest[:n_win]
    vwt_refs = rest[n_win:2 * n_win]
    o_ref = rest[2 * n_win]
    gall_ref, s0_ref, s1_ref, p0_ref, p1_ref, acc_ref = rest[2 * n_win + 1:]

    b = pl.program_id(1)
    t0 = b * Q_BLOCK
    Q = Q_BLOCK
    H = HEADS_PER_GROUP
    C = H * Q
    q = q_ref[...]
    qt = jnp.concatenate(
        [q[:, h * HEAD_DIM:(h + 1) * HEAD_DIM].astype(F32).T.astype(BF16) for h in range(H)], axis=1)

    kc = kc_ref[0]
    n_cmp = kc.shape[0]
    qi_c = lax.broadcasted_iota(jnp.int32, (n_cmp, Q), 1)
    cpos = lax.broadcasted_iota(jnp.int32, (n_cmp, Q), 0) * CMP_STRIDE + (CMP_BLOCK - 1)
    bias_c = jnp.where(cpos <= t0 + qi_c, 0.0, NEG_INF)
    e_c, l_c = _softmax_cols(_dot(kc, qt) + _tile_heads(bias_c))
    qi_row = lax.broadcasted_iota(jnp.int32, (1, C), 1) & (Q - 1)
    p_c = e_c * jnp.where(t0 + qi_row >= CMP_BLOCK - 1, 1.0 / l_c, 0.0)
    o_c = _dot(vct_ref[0], p_c.astype(BF16))

    kw = jnp.concatenate([r[...] for r in kw_refs], axis=0)
    vwt = jnp.concatenate([r[0] for r in vwt_refs], axis=1)
    nw = WINDOW + Q
    row_w = lax.broadcasted_iota(jnp.int32, (nw, Q), 0)
    diff = lax.broadcasted_iota(jnp.int32, (nw, Q), 1) - row_w + WINDOW
    ok_w = (diff >= 0) & (diff < WINDOW) & (row_w >= WINDOW - t0)
    e_w, l_w = _softmax_cols(_dot(kw, qt) + _tile_heads(jnp.where(ok_w, 0.0, NEG_INF)))
    o_w = _dot(vwt, e_w.astype(BF16)) * (1.0 / l_w)

    imp = (p_c[:, 0:Q] + p_c[:, Q:2 * Q]) + p_c[:, 2 * Q:3 * Q] + p_c[:, 3 * Q:4 * Q]
    imp_hi = imp.astype(BF16)
    imp_lo = (imp - imp_hi.astype(F32)).astype(BF16)
    wselt = wselt_ref[...]
    s_blk = _dot(wselt, imp_hi) + _dot(wselt, imp_lo)
    nsp = s_blk.shape[0]
    blk = lax.broadcasted_iota(jnp.int32, (nsp, Q), 0)
    qp = t0 + lax.broadcasted_iota(jnp.int32, (nsp, Q), 1)
    cur = qp >> 6
    valid = blk * SLC_BLOCK <= qp
    forced = (blk == 0) | (blk == cur) | (blk == cur - 1)
    s_blk = jnp.where(forced, FORCE, jnp.where(valid, s_blk, NEG_INF))
    blk_f = blk.astype(F32)

    def pick(_, carry):
        s, sel = carry
        m = jnp.max(s, axis=0, keepdims=True)
        first = jnp.min(jnp.where(s == m, blk_f, float(nsp)), axis=0, keepdims=True)
        hit = blk_f == first
        return jnp.where(hit, REMOVED, s), jnp.where(hit, 1.0, sel)

    _, sel = lax.fori_loop(0, n_sel, pick, (s_blk, jnp.zeros((nsp, Q), F32)))

    T = SEL_KEY_TILE
    bpt = T // SLC_BLOCK
    n_tiles = (t0 + Q + T - 1) // T
    gall_ref[0:SEL_PAD, :] = jnp.full((SEL_PAD, C), NEG_INF, F32)
    gall_ref[SEL_PAD:SEL_PAD + nsp, :] = _tile_heads(jnp.where((sel > 0.5) & valid, 0.0, NEG_INF))
    gall_ref[SEL_PAD + nsp:, :] = jnp.full((SEL_PAD, C), NEG_INF, F32)
    onehot = onehot_ref[(b + 1) & 3]
    zero_rows = jnp.zeros((HEAD_DIM - 2 * bpt, C), BF16)
    k_rows = ks_ref.shape[1]
    v_blocks = vst_ref.shape[1]

    def scores(kt):
        start = pl.multiple_of(jnp.clip(t0 - kt * T, 0, k_rows - T), Q)
        first_blk = 2 * b + 2 - bpt * (kt + 1) + SEL_PAD
        w0 = pl.multiple_of(jnp.maximum((first_blk >> 3) << 3, 0), 8)
        table = gall_ref[pl.ds(w0, 2 * bpt), :].astype(BF16)
        lhs = jnp.concatenate([ks_ref[0, pl.ds(start, T), :], onehot], axis=1)
        rhs = jnp.concatenate([qt, table, zero_rows], axis=0)
        return _dot(lhs, rhs)

    def values(kt):
        vb = jnp.clip(b - (T // Q) * kt, 0, v_blocks - T // Q)
        return jnp.concatenate([vst_ref[0, vb + c] for c in range(T // Q)], axis=1)

    def tile_step(kt, s_cur, s_nxt, p_cur, p_prv, carry):
        alpha_prev, m, l = carry
        acc_ref[...] = alpha_prev * acc_ref[...] + _dot(values(kt - 1), p_prv[...])
        s_nxt[...] = scores(kt + 1)
        s = s_cur[...]
        m_new = jnp.maximum(m, jnp.max(s, axis=0, keepdims=True))
        alpha = jnp.exp2(m - m_new)
        p = jnp.exp2(s - m_new)
        l = alpha * l + jnp.sum(p, axis=0, keepdims=True)
        p_cur[...] = p.astype(BF16)
        return alpha, m_new, l

    def pair_step(j, carry):
        carry = tile_step(2 * j, s0_ref, s1_ref, p0_ref, p1_ref, carry)
        return tile_step(2 * j + 1, s1_ref, s0_ref, p1_ref, p0_ref, carry)

    s_first = scores(0)
    key_j = lax.broadcasted_iota(jnp.int32, (Q, Q), 0)
    qry_i = lax.broadcasted_iota(jnp.int32, (Q, Q), 1)
    s0_ref[0:T - Q, :] = s_first[0:T - Q]
    s0_ref[T - Q:, :] = s_first[T - Q:] + _tile_heads(jnp.where(key_j <= qry_i, 0.0, NEG_INF))
    p1_ref[...] = jnp.zeros_like(p1_ref)
    acc_ref[...] = jnp.zeros_like(acc_ref)
    n_pairs = (n_tiles + 1) // 2
    alpha_last, _, l_s = lax.fori_loop(
        0, n_pairs, pair_step,
        (jnp.ones((1, C), F32), jnp.full((1, C), NEG_INF, F32), jnp.zeros((1, C), F32)))
    acc_s = alpha_last * acc_ref[...] + _dot(values(2 * n_pairs - 1), p1_ref[...])
    o_s = acc_s * (1.0 / l_s)

    gt = gates_ref[...].T
    for h in range(H):
        cols = slice(h * Q, (h + 1) * Q)
        o = (gt[h:h + 1] * o_c[:, cols] + gt[H + h:H + h + 1] * o_s[:, cols]
             + gt[2 * H + h:2 * H + h + 1] * o_w[:, cols])
        o_ref[:, h * HEAD_DIM:(h + 1) * HEAD_DIM] = o.T.astype(BF16)


def _selection_matrix_t(n_cmp, nsp):
    j = np.arange(nsp)[:, None]
    c = np.arange(n_cmp)[None, :]
    d = c - RATIO * j
    w = np.where((d == -1) | (d == RATIO - 1), 1.0, np.where((d >= 0) & (d < RATIO - 1), 2.0, 0.0))
    return jnp.asarray(w, BF16)


def _nsa_attention(proj, gates, kc, vct):
    S = proj.shape[0]
    G = N_KV_GROUPS
    T = SEL_KEY_TILE
    nb = S // Q_BLOCK
    n_cmp = kc.shape[1]
    ns = S // SLC_BLOCK
    nsp = -(-ns // LANES) * LANES
    n_sel = min(N_SLC, ns)
    wselt = _selection_matrix_t(n_cmp, nsp)
    n_win = WINDOW // Q_BLOCK + 1
    cpb = KV_WIDTH // HEAD_DIM
    q_w = N_HEADS * HEAD_DIM
    q_cols = q_w // HEAD_DIM
    kw_col = q_cols + 4 * cpb
    pad = T - Q_BLOCK
    ksp = jnp.pad(proj[:, q_w + 2 * KV_WIDTH:q_w + 3 * KV_WIDTH], ((pad, 0), (0, 0)))
    ksp = ksp.reshape(S + pad, G, HEAD_DIM).transpose(1, 0, 2)
    vsp = jnp.pad(proj[:, q_w + 3 * KV_WIDTH:q_w + 4 * KV_WIDTH], ((pad, 0), (0, 0)))
    vst = vsp.reshape((S + pad) // Q_BLOCK, Q_BLOCK, G, HEAD_DIM).transpose(2, 0, 3, 1)
    vwt = proj[:, q_w + 5 * KV_WIDTH:q_w + 6 * KV_WIDTH].reshape(S, G, HEAD_DIM).transpose(1, 2, 0)
    r = np.arange(T)[None, :, None] // SLC_BLOCK
    onehot = jnp.asarray(np.arange(LANES)[None, None, :] == 2 * np.arange(4)[:, None, None] + r, BF16)

    def win_block(b, i):
        return jnp.maximum(b - (n_win - 1) + i, 0)

    in_specs = [
        pl.BlockSpec((Q_BLOCK, KV_WIDTH), lambda g, b: (b, g)),
        pl.BlockSpec((Q_BLOCK, LANES), lambda g, b: (b, g)),
        pl.BlockSpec((1, n_cmp, HEAD_DIM), lambda g, b: (g, 0, 0)),
        pl.BlockSpec((1, HEAD_DIM, n_cmp), lambda g, b: (g, 0, 0)),
        pl.BlockSpec((1,) + ksp.shape[1:], lambda g, b: (g, 0, 0)),
        pl.BlockSpec((1,) + vst.shape[1:], lambda g, b: (g, 0, 0, 0)),
        pl.BlockSpec(wselt.shape, lambda g, b: (0, 0)),
        pl.BlockSpec(onehot.shape, lambda g, b: (0, 0, 0)),
    ]
    in_specs += [pl.BlockSpec((Q_BLOCK, HEAD_DIM), functools.partial(lambda g, b, i: (win_block(b, i), kw_col + g), i=i))
                 for i in range(n_win)]
    in_specs += [pl.BlockSpec((1, HEAD_DIM, Q_BLOCK), functools.partial(lambda g, b, i: (g, 0, win_block(b, i)), i=i))
                 for i in range(n_win)]
    args = [proj, gates, kc, vct, ksp, vst, wselt, onehot] + [proj] * n_win + [vwt] * n_win
    return pl.pallas_call(
        functools.partial(_nsa_attn_kernel, n_sel=n_sel),
        grid=(G, nb),
        in_specs=in_specs,
        out_specs=pl.BlockSpec((Q_BLOCK, KV_WIDTH), lambda g, b: (b, g)),
        out_shape=jax.ShapeDtypeStruct((S, N_HEADS * HEAD_DIM), BF16),
        scratch_shapes=[pltpu.VMEM((nsp + 2 * SEL_PAD, KV_WIDTH), F32),
                        pltpu.VMEM((T, KV_WIDTH), F32), pltpu.VMEM((T, KV_WIDTH), F32),
                        pltpu.VMEM((T, KV_WIDTH), BF16), pltpu.VMEM((T, KV_WIDTH), BF16),
                        pltpu.VMEM((HEAD_DIM, KV_WIDTH), F32)],
        compiler_params=_cparams(("arbitrary", "arbitrary")),
        name="nsa_attention",
    )(*args)


def _matmul_residual_kernel(a_ref, w_ref, res_ref, o_ref):
    o_ref[...] = res_ref[...] + _dot(a_ref[...], w_ref[...])


def _matmul_residual(a, w, res, tm):
    S, K = a.shape
    N = w.shape[1]
    return pl.pallas_call(
        _matmul_residual_kernel,
        grid=(S // tm,),
        in_specs=[
            pl.BlockSpec((tm, K), lambda i: (i, 0)),
            pl.BlockSpec((K, N), lambda i: (0, 0)),
            pl.BlockSpec((tm, N), lambda i: (i, 0)),
        ],
        out_specs=pl.BlockSpec((tm, N), lambda i: (i, 0)),
        out_shape=jax.ShapeDtypeStruct((S, N), F32),
        compiler_params=_cparams(("arbitrary",)),
        name="out_proj_residual",
    )(a, w, res)


def _mem_kv_kernel(mem_ref, g_ref, wk_ref, wv_ref, k_ref, v_ref):
    mb = _rms_rows(mem_ref[...], g_ref[...]).astype(BF16)
    k_ref[...] = _dot(mb, wk_ref[...]).astype(BF16)
    v_ref[...] = _dot(mb, wv_ref[...]).astype(BF16)


def _mem_kv(mem, g, wk, wv):
    M, D = mem.shape
    W = wk.shape[1]
    full = lambda shape: pl.BlockSpec(shape, lambda i: (0, 0))
    return pl.pallas_call(
        _mem_kv_kernel,
        grid=(1,),
        in_specs=[full((M, D)), full((1, D)), full((D, W)), full((D, W))],
        out_specs=[full((M, W)), full((M, W))],
        out_shape=[jax.ShapeDtypeStruct((M, W), BF16)] * 2,
        compiler_params=_cparams(("arbitrary",)),
        name="mem_kv",
    )(mem, g, wk, wv)


def _mem_attn_kernel(h_ref, g_ref, wq_ref, k_ref, v_ref, wo_ref, o_ref):
    h = h_ref[...]
    ub = _rms_rows(h, g_ref[...]).astype(BF16)
    q = (_dot(ub, wq_ref[...]) * (MEM_HEAD_DIM ** -0.5)).astype(BF16)
    outs = []
    for hd in range(MEM_HEADS):
        cols = slice(hd * MEM_HEAD_DIM, (hd + 1) * MEM_HEAD_DIM)
        s = _dot_t(q[:, cols], k_ref[:, cols])
        e = jnp.exp(s - jnp.max(s, axis=1, keepdims=True))
        p = e / jnp.sum(e, axis=1, keepdims=True)
        outs.append(_dot(p.astype(BF16), v_ref[:, cols]).astype(BF16))
    o = jnp.concatenate(outs, axis=1)
    o_ref[...] = h + _dot(o, wo_ref[...])


def _mem_attn(h, g, wq, k, v, wo, tm):
    S, D = h.shape
    W = wq.shape[1]
    M = k.shape[0]
    full = lambda shape: pl.BlockSpec(shape, lambda i: (0, 0))
    return pl.pallas_call(
        _mem_attn_kernel,
        grid=(S // tm,),
        in_specs=[pl.BlockSpec((tm, D), lambda i: (i, 0)), full((1, D)), full((D, W)),
                  full((M, W)), full((M, W)), full((W, D))],
        out_specs=pl.BlockSpec((tm, D), lambda i: (i, 0)),
        out_shape=jax.ShapeDtypeStruct((S, D), F32),
        compiler_params=_cparams(("arbitrary",)),
        name="mem_attention",
    )(h, g, wq, k, v, wo)


def _swiglu_tile(ub, wg, wu, wd):
    gate = _dot(ub, wg)
    up = _dot(ub, wu)
    act = (gate * jax.nn.sigmoid(gate)) * up
    return _dot(act.astype(BF16), wd)


def _ffn_kernel(h_ref, g_ref, wg_ref, wu_ref, wd_ref, o_ref, u_ref):
    f = pl.program_id(1)

    @pl.when(f == 0)
    def _():
        h = h_ref[...]
        u_ref[...] = _rms_rows(h, g_ref[...]).astype(BF16)
        o_ref[...] = h

    o_ref[...] += _swiglu_tile(u_ref[...], wg_ref[...], wu_ref[...], wd_ref[...])


def _ffn(h, g, wg, wu, wd, tm, tf):
    S, D = h.shape
    F = wg.shape[1]
    return pl.pallas_call(
        _ffn_kernel,
        grid=(S // tm, F // tf),
        in_specs=[
            pl.BlockSpec((tm, D), lambda i, f: (i, 0)),
            pl.BlockSpec((1, D), lambda i, f: (0, 0)),
            pl.BlockSpec((D, tf), lambda i, f: (0, f)),
            pl.BlockSpec((D, tf), lambda i, f: (0, f)),
            pl.BlockSpec((tf, D), lambda i, f: (f, 0)),
        ],
        out_specs=pl.BlockSpec((tm, D), lambda i, f: (i, 0)),
        out_shape=jax.ShapeDtypeStruct((S, D), F32),
        scratch_shapes=[pltpu.VMEM((tm, D), BF16)],
        compiler_params=_cparams(("arbitrary", "arbitrary")),
        name="dense_swiglu",
    )(h, g, wg, wu, wd)


POOL_HALO = 16


def _pool_kernel(h_ref, halo_ref, g_ref, w_ref, b_ref, scale_ref, o_ref, ext_ref):
    i = pl.program_id(0)
    tm = h_ref.shape[0]
    h = h_ref[...]
    g = g_ref[...]
    u = _rms_rows(h, g)
    halo = _rms_rows(halo_ref[...], g)
    ext_ref[0:POOL_HALO, :] = jnp.where(i > 0, halo, 0.0)
    ext_ref[POOL_HALO:, :] = u
    pos = i * tm + lax.broadcasted_iota(jnp.int32, (tm, 1), 0)
    for gi, w in enumerate(POOL_WINDOWS):
        cols = slice(gi * POOL_GROUP, (gi + 1) * POOL_GROUP)
        tot = u[:, cols]
        for k in range(1, w):
            tot = tot + ext_ref[POOL_HALO - k:POOL_HALO - k + tm, cols]
        cnt = jnp.minimum(pos + 1, w).astype(F32)
        d = tot / cnt - u[:, cols]
        z = _dot(d.astype(BF16), w_ref[gi]) + b_ref[:, cols]
        o_ref[:, cols] = h[:, cols] + z * scale_ref[:, cols]


def _pool_mixer(h, g, w, b, scale, tm):
    S, D = h.shape
    ratio = tm // POOL_HALO
    return pl.pallas_call(
        _pool_kernel,
        grid=(S // tm,),
        in_specs=[
            pl.BlockSpec((tm, D), lambda i: (i, 0)),
            pl.BlockSpec((POOL_HALO, D), lambda i: (jnp.maximum(i * ratio - 1, 0), 0)),
            pl.BlockSpec((1, D), lambda i: (0, 0)),
            pl.BlockSpec(w.shape, lambda i: (0, 0, 0)),
            pl.BlockSpec((1, D), lambda i: (0, 0)),
            pl.BlockSpec((1, D), lambda i: (0, 0)),
        ],
        out_specs=pl.BlockSpec((tm, D), lambda i: (i, 0)),
        out_shape=jax.ShapeDtypeStruct((S, D), F32),
        scratch_shapes=[pltpu.VMEM((tm + POOL_HALO, D), F32)],
        compiler_params=_cparams(("arbitrary",)),
        name="pool_mixer",
    )(h, h, g, w, b, scale)


def _router_kernel(h_ref, g_ref, rhi_ref, rlo_ref, u_ref, idx_ref, w_ref):
    u = _rms_rows(h_ref[...], g_ref[...])
    u_ref[...] = u
    u_hi = u.astype(BF16)
    u_lo = (u - u_hi.astype(F32)).astype(BF16)
    rhi = rhi_ref[...]
    logits = (_dot(u_hi, rhi) + _dot(u_lo, rhi)) + _dot(u_hi, rlo_ref[...])
    lane = lax.broadcasted_iota(jnp.int32, logits.shape, 1)
    lane_f = lane.astype(F32)
    s = jnp.where(lane < N_EXPERTS, logits, REMOVED)
    m1 = jnp.max(s, axis=1, keepdims=True)
    i1 = jnp.min(jnp.where(s == m1, lane_f, float(LANES)), axis=1, keepdims=True)
    s = jnp.where(lane_f == i1, REMOVED, s)
    m2 = jnp.max(s, axis=1, keepdims=True)
    i2 = jnp.min(jnp.where(s == m2, lane_f, float(LANES)), axis=1, keepdims=True)
    e2 = jnp.exp(m2 - m1)
    den = 1.0 + e2
    idx_ref[...] = jnp.where(lane == 0, i1, jnp.where(lane == 1, i2, 0.0)).astype(jnp.int32)
    w_ref[...] = jnp.where(lane == 0, 1.0 / den, jnp.where(lane == 1, e2 / den, 0.0))


def _router(h, g, r_hi, r_lo, tm):
    S, D = h.shape
    return pl.pallas_call(
        _router_kernel,
        grid=(S // tm,),
        in_specs=[
            pl.BlockSpec((tm, D), lambda i: (i, 0)),
            pl.BlockSpec((1, D), lambda i: (0, 0)),
            pl.BlockSpec((D, LANES), lambda i: (0, 0)),
            pl.BlockSpec((D, LANES), lambda i: (0, 0)),
        ],
        out_specs=[
            pl.BlockSpec((tm, D), lambda i: (i, 0)),
            pl.BlockSpec((tm, LANES), lambda i: (i, 0)),
            pl.BlockSpec((tm, LANES), lambda i: (i, 0)),
        ],
        out_shape=[
            jax.ShapeDtypeStruct((S, D), F32),
            jax.ShapeDtypeStruct((S, LANES), jnp.int32),
            jax.ShapeDtypeStruct((S, LANES), F32),
        ],
        compiler_params=_cparams(("arbitrary",)),
        name="moe_router",
    )(h, g, r_hi, r_lo)


GATHER_UNROLL = 8


def _start_row_gather(src_hbm, dst_vmem, sem, n_rows, src_row_of):
    def body(r, c):
        pltpu.make_async_copy(src_hbm.at[pl.ds(src_row_of(r), 1), :], dst_vmem.at[pl.ds(r, 1), :], sem).start()
        return c

    lax.fori_loop(0, n_rows, body, 0, unroll=GATHER_UNROLL)


def _wait_row_gather(src_hbm, dst_vmem, sem):
    pltpu.make_async_copy(src_hbm.at[pl.ds(0, dst_vmem.shape[0]), :], dst_vmem, sem).wait()


def _moe_expert_kernel(tok_ref, exp_ref, nused_ref, u_hbm, wg_ref, wu_ref, wd_ref, o_ref, x_ref, xb_ref, sem):
    i = pl.program_id(0)
    f = pl.program_id(1)
    rows = x_ref.shape[1]
    n_used = nused_ref[0]
    active = i < n_used
    slot = i & 1

    def gather(block, into):
        _start_row_gather(u_hbm, x_ref.at[into], sem.at[into], rows, lambda r: tok_ref[block * rows + r])

    @pl.when((i == 0) & (f == 0) & active)
    def _():
        gather(0, 0)

    @pl.when(active & (f == 0))
    def _():
        _wait_row_gather(u_hbm, x_ref.at[slot], sem.at[slot])
        xb_ref[...] = x_ref[slot].astype(BF16)
        o_ref[...] = jnp.zeros_like(o_ref)

        @pl.when(i + 1 < n_used)
        def _():
            gather(i + 1, 1 - slot)

    @pl.when(active)
    def _():
        o_ref[...] += _swiglu_tile(xb_ref[...], wg_ref[0], wu_ref[0], wd_ref[0])

    @pl.when(jnp.logical_not(active) & (f == 0))
    def _():
        o_ref[...] = jnp.zeros_like(o_ref)


def _moe_experts(tok, blk_exp, n_used, u, wg, wu, wd, tf):
    N, D = u.shape
    n_blk = blk_exp.shape[0]
    F = wg.shape[2]
    rows = MOE_BLOCK
    n_f = F // tf

    def ftile(i, f, nu):
        return jnp.where(i < nu[0], f, n_f - 1)

    grid_spec = pltpu.PrefetchScalarGridSpec(
        num_scalar_prefetch=3,
        grid=(n_blk, n_f),
        in_specs=[
            pl.BlockSpec(memory_space=pl.ANY),
            pl.BlockSpec((1, D, tf), lambda i, f, tok, ex, nu: (ex[i], 0, ftile(i, f, nu))),
            pl.BlockSpec((1, D, tf), lambda i, f, tok, ex, nu: (ex[i], 0, ftile(i, f, nu))),
            pl.BlockSpec((1, tf, D), lambda i, f, tok, ex, nu: (ex[i], ftile(i, f, nu), 0)),
        ],
        out_specs=pl.BlockSpec((rows, D), lambda i, f, tok, ex, nu: (i, 0)),
        scratch_shapes=[pltpu.VMEM((2, rows, D), F32), pltpu.VMEM((rows, D), BF16), pltpu.SemaphoreType.DMA((2,))],
    )
    return pl.pallas_call(
        _moe_expert_kernel,
        grid_spec=grid_spec,
        out_shape=jax.ShapeDtypeStruct((n_blk * rows, D), F32),
        compiler_params=_cparams(("arbitrary", "arbitrary")),
        name="moe_experts",
    )(tok, blk_exp, n_used, u, wg, wu, wd)


def _moe_combine_kernel(pos_ref, h_ref, w_ref, gfin_ref, y_hbm, o_ref, r_ref, sem):
    i = pl.program_id(0)
    n_steps = pl.num_programs(0)
    tm = h_ref.shape[0]
    slot = i & 1

    def gather(step, into):
        for k in range(TOP_K):
            _start_row_gather(y_hbm, r_ref.at[into, k], sem.at[into, k], tm,
                              lambda r: pos_ref[(step * tm + r) * TOP_K + k])

    @pl.when(i == 0)
    def _():
        gather(0, 0)

    for k in range(TOP_K):
        _wait_row_gather(y_hbm, r_ref.at[slot, k], sem.at[slot, k])

    @pl.when(i + 1 < n_steps)
    def _():
        gather(i + 1, 1 - slot)

    w = w_ref[...]
    y = h_ref[...] + (r_ref[slot, 0] * w[:, 0:1] + r_ref[slot, 1] * w[:, 1:2])
    o_ref[...] = _rms_rows(y, gfin_ref[...])


def _moe_combine(pos, h, w, g_final, y, tm):
    S, D = h.shape
    grid_spec = pltpu.PrefetchScalarGridSpec(
        num_scalar_prefetch=1,
        grid=(S // tm,),
        in_specs=[
            pl.BlockSpec((tm, D), lambda i, pos: (i, 0)),
            pl.BlockSpec((tm, LANES), lambda i, pos: (i, 0)),
            pl.BlockSpec((1, D), lambda i, pos: (0, 0)),
            pl.BlockSpec(memory_space=pl.ANY),
        ],
        out_specs=pl.BlockSpec((tm, D), lambda i, pos: (i, 0)),
        scratch_shapes=[pltpu.VMEM((2, TOP_K, tm, D), F32), pltpu.SemaphoreType.DMA((2, TOP_K))],
    )
    return pl.pallas_call(
        _moe_combine_kernel,
        grid_spec=grid_spec,
        out_shape=jax.ShapeDtypeStruct((S, D), F32),
        compiler_params=_cparams(("arbitrary",)),
        name="moe_combine_norm",
    )(pos, h, w, g_final, y)


def _moe_dispatch_indices(top_idx):
    N = top_idx.shape[0]
    e_flat = top_idx.reshape(-1)
    onehot = (e_flat[:, None] == jnp.arange(N_EXPERTS, dtype=jnp.int32)[None, :]).astype(jnp.int32)
    csum = jnp.cumsum(onehot, axis=0)
    counts = csum[-1]
    rank = jnp.take_along_axis(csum, e_flat[:, None], axis=1)[:, 0] - 1
    padded = (counts + MOE_BLOCK - 1) // MOE_BLOCK * MOE_BLOCK
    pend = jnp.cumsum(padded)
    poff = pend - padded
    dest = poff[e_flat] + rank
    n_blk = (N * TOP_K + MOE_BLOCK - 1) // MOE_BLOCK + N_EXPERTS
    cap = n_blk * MOE_BLOCK
    t_flat = jnp.repeat(jnp.arange(N, dtype=jnp.int32), TOP_K)
    tok_buf = jnp.zeros((cap,), jnp.int32).at[dest].set(t_flat)
    blk_start = jnp.arange(n_blk, dtype=jnp.int32) * MOE_BLOCK
    blk_exp = jnp.minimum(jnp.sum(pend[None, :] <= blk_start[:, None], axis=1), N_EXPERTS - 1).astype(jnp.int32)
    n_used = (pend[-1] // MOE_BLOCK).astype(jnp.int32).reshape(1)
    return tok_buf, blk_exp, n_used, dest.astype(jnp.int32)


def _rope_tables(pos):
    half = HEAD_DIM // 2
    inv = ROPE_THETA ** (-jnp.arange(half, dtype=F32) / half)
    ang = pos.astype(F32)[:, None] * inv[None, :]
    cos, sin = jnp.cos(ang), jnp.sin(ang)
    return jnp.concatenate([cos, cos], axis=1), jnp.concatenate([-sin, sin], axis=1)


def _gate_params(w_gates, gate_b):
    D = w_gates.shape[0]
    w = w_gates.reshape(D, 3, N_KV_GROUPS, HEADS_PER_GROUP).transpose(0, 2, 1, 3).reshape(D, N_KV_GROUPS, 3 * HEADS_PER_GROUP)
    w = jnp.pad(w, ((0, 0), (0, 0), (0, LANES - 3 * HEADS_PER_GROUP))).reshape(D, N_KV_GROUPS * LANES)
    b = gate_b.reshape(3, N_KV_GROUPS, HEADS_PER_GROUP).transpose(1, 0, 2).reshape(N_KV_GROUPS, 3 * HEADS_PER_GROUP)
    b = jnp.pad(b, ((0, 0), (0, LANES - 3 * HEADS_PER_GROUP))).reshape(1, N_KV_GROUPS * LANES)
    return w.astype(BF16), b


def _chunked_groups(t):
    S = t.shape[0]
    return (t.reshape(S // CMP_STRIDE, CMP_STRIDE, N_KV_GROUPS, HEAD_DIM)
            .transpose(2, 0, 1, 3).reshape(N_KV_GROUPS, S // CMP_STRIDE, CMP_STRIDE * HEAD_DIM))


def _row_tile(S, want):
    t = min(S, want)
    assert S % t == 0
    return t


def kernel(x, mem, norm_mix, norm_mem_q, norm_mem_kv, norm_ffn, norm_final, nsa_w_in, nsa_gate_b, nsa_pe_k, nsa_pe_v, nsa_cmp_k_w1, nsa_cmp_k_w2, nsa_cmp_v_w1, nsa_cmp_v_w2, nsa_w_out, pool_w, pool_b, pool_scale, mem_wq, mem_wk, mem_wv, mem_wo, ffn_w_gate, ffn_w_up, ffn_w_down, moe_router, moe_w_gate, moe_w_up, moe_w_down):
    B, S, D = x.shape
    assert B == 1 and S % WINDOW == 0 and S >= SLC_BLOCK * N_SLC
    h = x.reshape(S, D)
    memf = mem.reshape(mem.shape[1], D)
    row = lambda v: v.reshape(1, -1)

    n_main = N_HEADS * HEAD_DIM + 6 * KV_WIDTH
    w_in = nsa_w_in[0]
    w_gate, b_gate = _gate_params(w_in[:, n_main:], nsa_gate_b[0])
    cos, sin = _rope_tables(jnp.arange(S))
    proj, gates = _nsa_proj(h, row(norm_mix[0]), w_in[:, :n_main].astype(BF16), w_gate, b_gate, cos, sin,
                            _row_tile(S, 1024))
    q_w = N_HEADS * HEAD_DIM
    n_chunks = S // CMP_STRIDE
    cmp_pos = jnp.arange(n_chunks) * CMP_STRIDE + CMP_BLOCK - 1
    ccos, csin = _rope_tables(cmp_pos)
    kc = _compress(_chunked_groups(proj[:, q_w:q_w + KV_WIDTH]), nsa_pe_k[0].reshape(1, -1),
                   nsa_cmp_k_w1[0].astype(BF16), nsa_cmp_k_w2[0].astype(BF16), ccos, csin, True)
    vct = _compress(_chunked_groups(proj[:, q_w + KV_WIDTH:q_w + 2 * KV_WIDTH]), nsa_pe_v[0].reshape(1, -1),
                    nsa_cmp_v_w1[0].astype(BF16), nsa_cmp_v_w2[0].astype(BF16), ccos, csin, False).transpose(0, 2, 1)
    attn = _nsa_attention(proj, gates, kc, vct)
    h = _matmul_residual(attn, nsa_w_out[0].astype(BF16), h, _row_tile(S, 512))

    def mem_layer(h, i):
        k, v = _mem_kv(memf, row(norm_mem_kv[i]), mem_wk[i].astype(BF16), mem_wv[i].astype(BF16))
        return _mem_attn(h, row(norm_mem_q[i]), mem_wq[i].astype(BF16), k, v, mem_wo[i].astype(BF16),
                         _row_tile(S, 512))

    h = mem_layer(h, 0)
    h = _ffn(h, row(norm_ffn[0]), ffn_w_gate[0].astype(BF16), ffn_w_up[0].astype(BF16),
             ffn_w_down[0].astype(BF16), _row_tile(S, 1024), 512)

    h = _pool_mixer(h, row(norm_mix[1]), pool_w[0].astype(BF16), row(pool_b[0]), row(pool_scale[0]),
                    _row_tile(S, 512))
    h = mem_layer(h, 1)

    r = jnp.pad(moe_router[0], ((0, 0), (0, LANES - N_EXPERTS)))
    r_hi = r.astype(BF16)
    r_lo = (r - r_hi.astype(F32)).astype(BF16)
    u, idx, gate_w = _router(h, row(norm_ffn[1]), r_hi, r_lo, _row_tile(S, 512))
    tok_buf, blk_exp, n_used, dest = _moe_dispatch_indices(idx[:, :TOP_K])
    y = _moe_experts(tok_buf, blk_exp, n_used, u, moe_w_gate[0].astype(BF16), moe_w_up[0].astype(BF16),
                     moe_w_down[0].astype(BF16), 512)
    out = _moe_combine(dest, h, gate_w, row(norm_final), y, _row_tile(S, 256))
    return out.reshape(B, S, D)
```

```python
import functools

import numpy as np
import jax
import jax.numpy as jnp
from jax import lax
from jax.experimental import pallas as pl
from jax.experimental.pallas import tpu as pltpu

F32 = jnp.float32
BF16 = jnp.bfloat16

N_HEADS = 16
HEAD_DIM = 128
N_KV_GROUPS = 4
HEADS_PER_GROUP = 4
KV_WIDTH = N_KV_GROUPS * HEAD_DIM
CMP_BLOCK = 32
CMP_STRIDE = 16
SLC_BLOCK = 64
RATIO = SLC_BLOCK // CMP_STRIDE
N_SLC = 16
WINDOW = 512
Q_BLOCK = 128
ROPE_THETA = 10000.0
POOL_WINDOWS = (2, 4, 8, 16)
POOL_GROUP = 512
MEM_HEADS = 4
MEM_HEAD_DIM = 128
N_EXPERTS = 8
TOP_K = 2
MOE_BLOCK = 512
EPS = 1e-6
NEG_INF = -1e30
FORCE = 1e9
REMOVED = -3e38
LOG2E = float(np.log2(np.e))

LANES = 128
VMEM_LIMIT = 56 * 1024 * 1024

SEL_KEY_TILE = 512


def _cparams(sem, flags=None):
    return pltpu.CompilerParams(dimension_semantics=sem, vmem_limit_bytes=VMEM_LIMIT, flags=flags)


def _rms_rows(xf, g):
    r = lax.rsqrt(jnp.mean(xf * xf, axis=-1, keepdims=True) + EPS)
    return (xf * r) * g


def _dot(a, b):
    return jnp.dot(a, b, preferred_element_type=F32)


def _dot_t(a, b):
    return lax.dot_general(a, b, (((1,), (1,)), ((), ())), preferred_element_type=F32)


def _proj_kernel(x_ref, g_ref, w_ref, wgate_ref, bgate_ref, cos_ref, sin_ref, o_ref, gates_ref, u_ref):
    j = pl.program_id(1)

    @pl.when(j == 0)
    def _():
        ub = _rms_rows(x_ref[...], g_ref[...]).astype(BF16)
        u_ref[...] = ub
        gates_ref[...] = jax.nn.sigmoid(_dot(ub, wgate_ref[...]) + bgate_ref[...])

    acc = _dot(u_ref[...], w_ref[...])
    n_q_tiles = N_HEADS * HEAD_DIM // KV_WIDTH
    is_q = j < n_q_tiles
    is_rope = is_q | (j == n_q_tiles + 2) | (j == n_q_tiles + 4)

    @pl.when(is_rope)
    def _():
        scale = jnp.where(is_q, HEAD_DIM ** -0.5 * LOG2E, 1.0).astype(F32)
        cos = cos_ref[...]
        sin = sin_ref[...]
        for h in range(KV_WIDTH // HEAD_DIM):
            xh = acc[:, h * HEAD_DIM:(h + 1) * HEAD_DIM]
            rot = pltpu.roll(xh, HEAD_DIM // 2, axis=1)
            o_ref[:, h * HEAD_DIM:(h + 1) * HEAD_DIM] = ((xh * cos + rot * sin) * scale).astype(BF16)

    @pl.when(jnp.logical_not(is_rope))
    def _():
        o_ref[...] = acc.astype(BF16)


def _nsa_proj(x, g, w_main, w_gate, b_gate, cos, sin, tm):
    S, D = x.shape
    n_main = w_main.shape[1]
    tn = KV_WIDTH
    n_gate = w_gate.shape[1]
    return pl.pallas_call(
        _proj_kernel,
        grid=(S // tm, n_main // tn),
        in_specs=[
            pl.BlockSpec((tm, D), lambda i, j: (i, 0)),
            pl.BlockSpec((1, D), lambda i, j: (0, 0)),
            pl.BlockSpec((D, tn), lambda i, j: (0, j)),
            pl.BlockSpec((D, n_gate), lambda i, j: (0, 0)),
            pl.BlockSpec((1, n_gate), lambda i, j: (0, 0)),
            pl.BlockSpec((tm, HEAD_DIM), lambda i, j: (i, 0)),
            pl.BlockSpec((tm, HEAD_DIM), lambda i, j: (i, 0)),
        ],
        out_specs=[
            pl.BlockSpec((tm, tn), lambda i, j: (i, j)),
            pl.BlockSpec((tm, n_gate), lambda i, j: (i, 0)),
        ],
        out_shape=[
            jax.ShapeDtypeStruct((S, n_main), BF16),
            jax.ShapeDtypeStruct((S, n_gate), F32),
        ],
        scratch_shapes=[pltpu.VMEM((tm, D), BF16)],
        compiler_params=_cparams(("arbitrary", "arbitrary")),
        name="nsa_proj",
    )(x, g, w_main, w_gate, b_gate, cos, sin)


def _gelu_tanh(x):
    return 0.5 * x * (1.0 + jnp.tanh(np.sqrt(2.0 / np.pi).astype(np.float32) * (x + 0.044715 * (x * x * x))))


def _compress_kernel(t_ref, pe_ref, w1_ref, w2_ref, cos_ref, sin_ref, o_ref, *, apply_rope):
    half = CMP_STRIDE * HEAD_DIM
    t = t_ref[0].astype(F32)
    n = t.shape[0]
    a = _dot((t + pe_ref[:, :half]).astype(BF16), w1_ref[:half, :])
    b = _dot((t + pe_ref[:, half:]).astype(BF16), w1_ref[half:, :])
    hid = a + pltpu.roll(b, n - 1, axis=0)
    out = _dot(_gelu_tanh(hid).astype(BF16), w2_ref[...])
    if apply_rope:
        rot = pltpu.roll(out, HEAD_DIM // 2, axis=1)
        out = out * cos_ref[...] + rot * sin_ref[...]
    o_ref[0] = out.astype(BF16)


def _compress(t, pe_flat, w1, w2, cos, sin, apply_rope):
    G, n, width = t.shape
    hidden = w1.shape[1]
    out_block = (1, n, HEAD_DIM)
    return pl.pallas_call(
        functools.partial(_compress_kernel, apply_rope=apply_rope),
        grid=(G,),
        in_specs=[
            pl.BlockSpec((1, n, width), lambda g: (g, 0, 0)),
            pl.BlockSpec((1, 2 * width), lambda g: (0, 0)),
            pl.BlockSpec((2 * width, hidden), lambda g: (0, 0)),
            pl.BlockSpec((hidden, HEAD_DIM), lambda g: (0, 0)),
            pl.BlockSpec((n, HEAD_DIM), lambda g: (0, 0)),
            pl.BlockSpec((n, HEAD_DIM), lambda g: (0, 0)),
        ],
        out_specs=pl.BlockSpec(out_block, lambda g: (g, 0, 0)),
        out_shape=jax.ShapeDtypeStruct((G,) + out_block[1:], BF16),
        compiler_params=_cparams(("arbitrary",)),
        name="nsa_compress_k" if apply_rope else "nsa_compress_v",
    )(t, pe_flat, w1, w2, cos, sin)


def _softmax_cols(s):
    e = jnp.exp2(s - jnp.max(s, axis=0, keepdims=True))
    return e, jnp.sum(e, axis=0, keepdims=True)


def _tile_heads(x):
    return jnp.concatenate([x] * HEADS_PER_GROUP, axis=1)


SEL_PAD = 16


def _run_on_shortest_prefix(fn, total, needed, parts):
    piece = total // parts
    sizes = [piece * k for k in range(1, parts + 1)] if piece % LANES == 0 else [total]
    for idx, n in enumerate(sizes):
        cond = needed > (sizes[idx - 1] if idx else 0)
        if idx < len(sizes) - 1:
            cond = cond & (needed <= n)
        pl.when(cond)(functools.partial(fn, n))


def _nsa_attn_kernel(q_ref, gates_ref, kc_ref, vct_ref, ks_ref, vst_ref, wselt_ref, onehot_ref, *rest, n_sel):
    n_win = WINDOW // Q_BLOCK + 1
    kw_refs = rest[:n_win]
    vwt_refs = rest[n_win:2 * n_win]
    o_ref = rest[2 * n_win]
    gall_ref, s0_ref, s1_ref, p0_ref, p1_ref, acc_ref, oc_ref, sblk_ref = rest[2 * n_win + 1:]

    b = pl.program_id(1)
    t0 = b * Q_BLOCK
    Q = Q_BLOCK
    H = HEADS_PER_GROUP
    C = H * Q
    q = q_ref[...]
    qt = jnp.concatenate(
        [q[:, h * HEAD_DIM:(h + 1) * HEAD_DIM].astype(F32).T.astype(BF16) for h in range(H)], axis=1)

    n_cmp = kc_ref.shape[1]
    nsp = wselt_ref.shape[0]
    qi_row = lax.broadcasted_iota(jnp.int32, (1, C), 1) & (Q - 1)

    def compressed(n):
        qi_c = lax.broadcasted_iota(jnp.int32, (n, Q), 1)
        cpos = lax.broadcasted_iota(jnp.int32, (n, Q), 0) * CMP_STRIDE + (CMP_BLOCK - 1)
        bias_c = jnp.where(cpos <= t0 + qi_c, 0.0, NEG_INF)
        e_c, l_c = _softmax_cols(_dot(kc_ref[0, 0:n, :], qt) + _tile_heads(bias_c))
        p_c = e_c * jnp.where(t0 + qi_row >= CMP_BLOCK - 1, 1.0 / l_c, 0.0)
        oc_ref[...] = _dot(vct_ref[0, :, 0:n], p_c.astype(BF16))
        imp = (p_c[:, 0:Q] + p_c[:, Q:2 * Q]) + p_c[:, 2 * Q:3 * Q] + p_c[:, 3 * Q:4 * Q]
        imp_hi = imp.astype(BF16)
        imp_lo = (imp - imp_hi.astype(F32)).astype(BF16)
        wselt = wselt_ref[:, 0:n]
        sblk_ref[...] = _dot(wselt, imp_hi) + _dot(wselt, imp_lo)

    _run_on_shortest_prefix(compressed, n_cmp, (t0 + Q - CMP_BLOCK) // CMP_STRIDE + 1, 4)
    o_c = oc_ref[...]

    def select(rows):
        blk = lax.broadcasted_iota(jnp.int32, (rows, Q), 0)
        qp = t0 + lax.broadcasted_iota(jnp.int32, (rows, Q), 1)
        cur = qp >> 6
        valid = blk * SLC_BLOCK <= qp
        forced = (blk == 0) | (blk == cur) | (blk == cur - 1)
        s_blk = jnp.where(forced, FORCE, jnp.where(valid, sblk_ref[0:rows, :], NEG_INF))
        blk_f = blk.astype(F32)

        def pick(_, carry):
            s, sel = carry
            m = jnp.max(s, axis=0, keepdims=True)
            first = jnp.min(jnp.where(s == m, blk_f, float(rows)), axis=0, keepdims=True)
            hit = blk_f == first
            return jnp.where(hit, REMOVED, s), jnp.where(hit, 1.0, sel)

        _, sel = lax.fori_loop(0, n_sel, pick, (s_blk, jnp.zeros((rows, Q), F32)))
        gall_ref[SEL_PAD:SEL_PAD + rows, :] = _tile_heads(jnp.where((sel > 0.5) & valid, 0.0, NEG_INF))
        if rows < nsp:
            gall_ref[SEL_PAD + rows:SEL_PAD + nsp, :] = jnp.full((nsp - rows, C), NEG_INF, F32)

    _run_on_shortest_prefix(select, nsp, (t0 + Q - 1) // SLC_BLOCK + 1, 2)

    T = SEL_KEY_TILE
    bpt = T // SLC_BLOCK
    n_tiles = (t0 + Q + T - 1) // T
    gall_ref[0:SEL_PAD, :] = jnp.full((SEL_PAD, C), NEG_INF, F32)
    gall_ref[SEL_PAD + nsp:, :] = jnp.full((SEL_PAD, C), NEG_INF, F32)
    onehot = onehot_ref[(b + 1) & 3]
    zero_rows = jnp.zeros((HEAD_DIM - 2 * bpt, C), BF16)
    k_rows = ks_ref.shape[1]
    v_blocks = vst_ref.shape[1]

    def scores(kt):
        start = pl.multiple_of(jnp.clip(t0 - kt * T, 0, k_rows - T), Q)
        first_blk = 2 * b + 2 - bpt * (kt + 1) + SEL_PAD
        w0 = pl.multiple_of(jnp.maximum((first_blk >> 3) << 3, 0), 8)
        table = gall_ref[pl.ds(w0, 2 * bpt), :].astype(BF16)
        lhs = jnp.concatenate([ks_ref[0, pl.ds(start, T), :], onehot], axis=1)
        rhs = jnp.concatenate([qt, table, zero_rows], axis=0)
        return _dot(lhs, rhs)

    def values(kt):
        vb = jnp.clip(b - (T // Q) * kt, 0, v_blocks - T // Q)
        return jnp.concatenate([vst_ref[0, vb + c] for c in range(T // Q)], axis=1)

    def tile_step(kt, s_cur, s_nxt, p_cur, p_prv, carry):
        alpha_prev, m, l = carry
        acc_ref[...] = alpha_prev * acc_ref[...] + _dot(values(kt - 1), p_prv[...])
        s_nxt[...] = scores(kt + 1)
        s = s_cur[...]
        m_new = jnp.maximum(m, jnp.max(s, axis=0, keepdims=True))
        alpha = jnp.exp2(m - m_new)
        p = jnp.exp2(s - m_new)
        l = alpha * l + jnp.sum(p, axis=0, keepdims=True)
        p_cur[...] = p.astype(BF16)
        return alpha, m_new, l

    def pair_step(j, carry):
        carry = tile_step(2 * j, s0_ref, s1_ref, p0_ref, p1_ref, carry)
        return tile_step(2 * j + 1, s1_ref, s0_ref, p1_ref, p0_ref, carry)

    s_first = scores(0)
    key_j = lax.broadcasted_iota(jnp.int32, (Q, Q), 0)
    qry_i = lax.broadcasted_iota(jnp.int32, (Q, Q), 1)
    s0_ref[0:T - Q, :] = s_first[0:T - Q]
    s0_ref[T - Q:, :] = s_first[T - Q:] + _tile_heads(jnp.where(key_j <= qry_i, 0.0, NEG_INF))
    p1_ref[...] = jnp.zeros_like(p1_ref)
    acc_ref[...] = jnp.zeros_like(acc_ref)
    n_pairs = (n_tiles + 1) // 2
    alpha_last, _, l_s = lax.fori_loop(
        0, n_pairs, pair_step,
        (jnp.ones((1, C), F32), jnp.full((1, C), NEG_INF, F32), jnp.zeros((1, C), F32)))
    acc_s = alpha_last * acc_ref[...] + _dot(values(2 * n_pairs - 1), p1_ref[...])
    o_s = acc_s * (1.0 / l_s)

    kw = jnp.concatenate([r[...] for r in kw_refs], axis=0)
    vwt = jnp.concatenate([r[0] for r in vwt_refs], axis=1)
    nw = WINDOW + Q
    row_w = lax.broadcasted_iota(jnp.int32, (nw, Q), 0)
    diff = lax.broadcasted_iota(jnp.int32, (nw, Q), 1) - row_w + WINDOW
    ok_w = (diff >= 0) & (diff < WINDOW) & (row_w >= WINDOW - t0)
    e_w, l_w = _softmax_cols(_dot(kw, qt) + _tile_heads(jnp.where(ok_w, 0.0, NEG_INF)))
    o_w = _dot(vwt, e_w.astype(BF16)) * (1.0 / l_w)

    gt = gates_ref[...].T
    for h in range(H):
        cols = slice(h * Q, (h + 1) * Q)
        o = (gt[h:h + 1] * o_c[:, cols] + gt[H + h:H + h + 1] * o_s[:, cols]
             + gt[2 * H + h:2 * H + h + 1] * o_w[:, cols])
        o_ref[:, h * HEAD_DIM:(h + 1) * HEAD_DIM] = o.T.astype(BF16)


def _selection_matrix_t(n_cmp, nsp):
    j = np.arange(nsp)[:, None]
    c = np.arange(n_cmp)[None, :]
    d = c - RATIO * j
    w = np.where((d == -1) | (d == RATIO - 1), 1.0, np.where((d >= 0) & (d < RATIO - 1), 2.0, 0.0))
    return jnp.asarray(w, BF16)


def _nsa_attention(proj, gates, kc, vct):
    S = proj.shape[0]
    G = N_KV_GROUPS
    T = SEL_KEY_TILE
    nb = S // Q_BLOCK
    n_cmp = kc.shape[1]
    ns = S // SLC_BLOCK
    nsp = -(-ns // LANES) * LANES
    n_sel = min(N_SLC, ns)
    wselt = _selection_matrix_t(n_cmp, nsp)
    n_win = WINDOW // Q_BLOCK + 1
    cpb = KV_WIDTH // HEAD_DIM
    q_w = N_HEADS * HEAD_DIM
    q_cols = q_w // HEAD_DIM
    kw_col = q_cols + 4 * cpb
    pad = T - Q_BLOCK
    ksp = jnp.pad(proj[:, q_w + 2 * KV_WIDTH:q_w + 3 * KV_WIDTH], ((pad, 0), (0, 0)))
    ksp = ksp.reshape(S + pad, G, HEAD_DIM).transpose(1, 0, 2)
    vsp = jnp.pad(proj[:, q_w + 3 * KV_WIDTH:q_w + 4 * KV_WIDTH], ((pad, 0), (0, 0)))
    vst = vsp.reshape((S + pad) // Q_BLOCK, Q_BLOCK, G, HEAD_DIM).transpose(2, 0, 3, 1)
    vwt = proj[:, q_w + 5 * KV_WIDTH:q_w + 6 * KV_WIDTH].reshape(S, G, HEAD_DIM).transpose(1, 2, 0)
    r = np.arange(T)[None, :, None] // SLC_BLOCK
    onehot = jnp.asarray(np.arange(LANES)[None, None, :] == 2 * np.arange(4)[:, None, None] + r, BF16)

    def win_block(b, i):
        return jnp.maximum(b - (n_win - 1) + i, 0)

    in_specs = [
        pl.BlockSpec((Q_BLOCK, KV_WIDTH), lambda g, b: (b, g)),
        pl.BlockSpec((Q_BLOCK, LANES), lambda g, b: (b, g)),
        pl.BlockSpec((1, n_cmp, HEAD_DIM), lambda g, b: (g, 0, 0)),
        pl.BlockSpec((1, HEAD_DIM, n_cmp), lambda g, b: (g, 0, 0)),
        pl.BlockSpec((1,) + ksp.shape[1:], lambda g, b: (g, 0, 0)),
        pl.BlockSpec((1,) + vst.shape[1:], lambda g, b: (g, 0, 0, 0)),
        pl.BlockSpec(wselt.shape, lambda g, b: (0, 0)),
        pl.BlockSpec(onehot.shape, lambda g, b: (0, 0, 0)),
    ]
    in_specs += [pl.BlockSpec((Q_BLOCK, HEAD_DIM), functools.partial(lambda g, b, i: (win_block(b, i), kw_col + g), i=i))
                 for i in range(n_win)]
    in_specs += [pl.BlockSpec((1, HEAD_DIM, Q_BLOCK), functools.partial(lambda g, b, i: (g, 0, win_block(b, i)), i=i))
                 for i in range(n_win)]
    args = [proj, gates, kc, vct, ksp, vst, wselt, onehot] + [proj] * n_win + [vwt] * n_win
    return pl.pallas_call(
        functools.partial(_nsa_attn_kernel, n_sel=n_sel),
        grid=(G, nb),
        in_specs=in_specs,
        out_specs=pl.BlockSpec((Q_BLOCK, KV_WIDTH), lambda g, b: (b, g)),
        out_shape=jax.ShapeDtypeStruct((S, N_HEADS * HEAD_DIM), BF16),
        scratch_shapes=[pltpu.VMEM((nsp + 2 * SEL_PAD, KV_WIDTH), F32),
                        pltpu.VMEM((T, KV_WIDTH), F32), pltpu.VMEM((T, KV_WIDTH), F32),
                        pltpu.VMEM((T, KV_WIDTH), BF16), pltpu.VMEM((T, KV_WIDTH), BF16),
                        pltpu.VMEM((HEAD_DIM, KV_WIDTH), F32),
                        pltpu.VMEM((HEAD_DIM, KV_WIDTH), F32), pltpu.VMEM((nsp, Q_BLOCK), F32)],
        compiler_params=_cparams(("arbitrary", "arbitrary")),
        name="nsa_attention",
    )(*args)


NSA_GROUPS_PER_STEP = 2


def _nsa_attn_multi_kernel(q_ref, gates_ref, kc_ref, vct_ref, ks_ref, vst_ref, wselt_ref, onehot_ref, *rest, n_sel):
    n_win = WINDOW // Q_BLOCK + 1
    GG = NSA_GROUPS_PER_STEP
    kw_refs = rest[:n_win]
    vwt_refs = rest[n_win:2 * n_win]
    o_ref = rest[2 * n_win]
    scratch = rest[2 * n_win + 1:]
    gall_refs, s0_refs, s1_refs, p0_refs, p1_refs, acc_refs = (scratch[i * GG:(i + 1) * GG] for i in range(6))
    groups = range(GG)

    b = pl.program_id(1)
    t0 = b * Q_BLOCK
    Q = Q_BLOCK
    H = HEADS_PER_GROUP
    C = H * Q
    q = q_ref[...]
    qts = [jnp.concatenate(
        [q[:, (g * H + h) * HEAD_DIM:(g * H + h + 1) * HEAD_DIM].astype(F32).T.astype(BF16) for h in range(H)],
        axis=1) for g in groups]

    n_cmp = kc_ref.shape[1]
    qi_c = lax.broadcasted_iota(jnp.int32, (n_cmp, Q), 1)
    cpos = lax.broadcasted_iota(jnp.int32, (n_cmp, Q), 0) * CMP_STRIDE + (CMP_BLOCK - 1)
    bias_c = _tile_heads(jnp.where(cpos <= t0 + qi_c, 0.0, NEG_INF))
    qi_row = lax.broadcasted_iota(jnp.int32, (1, C), 1) & (Q - 1)
    sees_cmp = t0 + qi_row >= CMP_BLOCK - 1
    s_cs = [_dot(kc_ref[g], qts[g]) + bias_c for g in groups]
    el_cs = [_softmax_cols(s) for s in s_cs]
    p_cs = [e * jnp.where(sees_cmp, 1.0 / l, 0.0) for e, l in el_cs]
    o_cs = [_dot(vct_ref[g], p_cs[g].astype(BF16)) for g in groups]

    wselt = wselt_ref[...]
    nsp = wselt.shape[0]
    blk = lax.broadcasted_iota(jnp.int32, (nsp, Q), 0)
    qp = t0 + lax.broadcasted_iota(jnp.int32, (nsp, Q), 1)
    cur = qp >> 6
    valid = blk * SLC_BLOCK <= qp
    forced = (blk == 0) | (blk == cur) | (blk == cur - 1)
    blk_f = blk.astype(F32)
    s_blks = []
    for p_c in p_cs:
        imp = (p_c[:, 0:Q] + p_c[:, Q:2 * Q]) + p_c[:, 2 * Q:3 * Q] + p_c[:, 3 * Q:4 * Q]
        imp_hi = imp.astype(BF16)
        imp_lo = (imp - imp_hi.astype(F32)).astype(BF16)
        s_blk = _dot(wselt, imp_hi) + _dot(wselt, imp_lo)
        s_blks.append(jnp.where(forced, FORCE, jnp.where(valid, s_blk, NEG_INF)))

    def pick(_, carry):
        out = []
        for s, sel in carry:
            m = jnp.max(s, axis=0, keepdims=True)
            first = jnp.min(jnp.where(s == m, blk_f, float(nsp)), axis=0, keepdims=True)
            hit = blk_f == first
            out.append((jnp.where(hit, REMOVED, s), jnp.where(hit, 1.0, sel)))
        return tuple(out)

    picked = lax.fori_loop(0, n_sel, pick, tuple((s, jnp.zeros((nsp, Q), F32)) for s in s_blks))

    T = SEL_KEY_TILE
    bpt = T // SLC_BLOCK
    n_tiles = (t0 + Q + T - 1) // T
    for g in groups:
        gall_refs[g][0:SEL_PAD, :] = jnp.full((SEL_PAD, C), NEG_INF, F32)
        gall_refs[g][SEL_PAD:SEL_PAD + nsp, :] = _tile_heads(jnp.where((picked[g][1] > 0.5) & valid, 0.0, NEG_INF))
        gall_refs[g][SEL_PAD + nsp:, :] = jnp.full((SEL_PAD, C), NEG_INF, F32)
    onehot = onehot_ref[(b + 1) & 3]
    zero_rows = jnp.zeros((HEAD_DIM - 2 * bpt, C), BF16)
    k_rows = ks_ref.shape[1]
    v_blocks = vst_ref.shape[1]

    def scores(g, kt):
        start = pl.multiple_of(jnp.clip(t0 - kt * T, 0, k_rows - T), Q)
        first_blk = 2 * b + 2 - bpt * (kt + 1) + SEL_PAD
        w0 = pl.multiple_of(jnp.maximum((first_blk >> 3) << 3, 0), 8)
        table = gall_refs[g][pl.ds(w0, 2 * bpt), :].astype(BF16)
        lhs = jnp.concatenate([ks_ref[g, pl.ds(start, T), :], onehot], axis=1)
        rhs = jnp.concatenate([qts[g], table, zero_rows], axis=0)
        return _dot(lhs, rhs)

    def values(g, kt):
        vb = jnp.clip(b - (T // Q) * kt, 0, v_blocks - T // Q)
        return jnp.concatenate([vst_ref[g, vb + c] for c in range(T // Q)], axis=1)

    def tile_step(g, kt, s_cur, s_nxt, p_cur, p_prv, carry):
        alpha_prev, m, l = carry
        acc_refs[g][...] = alpha_prev * acc_refs[g][...] + _dot(values(g, kt - 1), p_prv[...])
        s_nxt[...] = scores(g, kt + 1)
        s = s_cur[...]
        m_new = jnp.maximum(m, jnp.max(s, axis=0, keepdims=True))
        alpha = jnp.exp2(m - m_new)
        p = jnp.exp2(s - m_new)
        l = alpha * l + jnp.sum(p, axis=0, keepdims=True)
        p_cur[...] = p.astype(BF16)
        return alpha, m_new, l

    def pair_step(j, carry):
        carry = tuple(tile_step(g, 2 * j, s0_refs[g], s1_refs[g], p0_refs[g], p1_refs[g], carry[g]) for g in groups)
        return tuple(tile_step(g, 2 * j + 1, s1_refs[g], s0_refs[g], p1_refs[g], p0_refs[g], carry[g]) for g in groups)

    key_j = lax.broadcasted_iota(jnp.int32, (Q, Q), 0)
    qry_i = lax.broadcasted_iota(jnp.int32, (Q, Q), 1)
    tri = _tile_heads(jnp.where(key_j <= qry_i, 0.0, NEG_INF))
    for g in groups:
        s_first = scores(g, 0)
        s0_refs[g][0:T - Q, :] = s_first[0:T - Q]
        s0_refs[g][T - Q:, :] = s_first[T - Q:] + tri
        p1_refs[g][...] = jnp.zeros_like(p1_refs[g])
        acc_refs[g][...] = jnp.zeros_like(acc_refs[g])
    n_pairs = (n_tiles + 1) // 2
    init = (jnp.ones((1, C), F32), jnp.full((1, C), NEG_INF, F32), jnp.zeros((1, C), F32))
    final = lax.fori_loop(0, n_pairs, pair_step, tuple(init for _ in groups))
    o_ss = []
    for g in groups:
        alpha_last, _, l_s = final[g]
        acc_s = alpha_last * acc_refs[g][...] + _dot(values(g, 2 * n_pairs - 1), p1_refs[g][...])
        o_ss.append(acc_s * (1.0 / l_s))

    nw = WINDOW + Q
    row_w = lax.broadcasted_iota(jnp.int32, (nw, Q), 0)
    diff = lax.broadcasted_iota(jnp.int32, (nw, Q), 1) - row_w + WINDOW
    ok_w = (diff >= 0) & (diff < WINDOW) & (row_w >= WINDOW - t0)
    bias_w = _tile_heads(jnp.where(ok_w, 0.0, NEG_INF))
    kw = jnp.concatenate([r[...] for r in kw_refs], axis=0)
    o_ws = []
    for g in groups:
        vwt = jnp.concatenate([r[g] for r in vwt_refs], axis=1)
        e_w, l_w = _softmax_cols(_dot(kw[:, g * HEAD_DIM:(g + 1) * HEAD_DIM], qts[g]) + bias_w)
        o_ws.append(_dot(vwt, e_w.astype(BF16)) * (1.0 / l_w))

    for g in groups:
        gt = gates_ref[:, g * LANES:(g + 1) * LANES].T
        for h in range(H):
            cols = slice(h * Q, (h + 1) * Q)
            o = (gt[h:h + 1] * o_cs[g][:, cols] + gt[H + h:H + h + 1] * o_ss[g][:, cols]
                 + gt[2 * H + h:2 * H + h + 1] * o_ws[g][:, cols])
            o_ref[:, (g * H + h) * HEAD_DIM:(g * H + h + 1) * HEAD_DIM] = o.T.astype(BF16)


def _nsa_attention_multi(proj, gates, kc, vct):
    S = proj.shape[0]
    G = N_KV_GROUPS
    GG = NSA_GROUPS_PER_STEP
    T = SEL_KEY_TILE
    nb = S // Q_BLOCK
    n_cmp = kc.shape[1]
    ns = S // SLC_BLOCK
    nsp = -(-ns // LANES) * LANES
    n_sel = min(N_SLC, ns)
    wselt = _selection_matrix_t(n_cmp, nsp)
    n_win = WINDOW // Q_BLOCK + 1
    q_w = N_HEADS * HEAD_DIM
    kw_col = (q_w + 4 * KV_WIDTH) // (GG * HEAD_DIM)
    pad = T - Q_BLOCK
    ksp = jnp.pad(proj[:, q_w + 2 * KV_WIDTH:q_w + 3 * KV_WIDTH], ((pad, 0), (0, 0)))
    ksp = ksp.reshape(S + pad, G, HEAD_DIM).transpose(1, 0, 2)
    vsp = jnp.pad(proj[:, q_w + 3 * KV_WIDTH:q_w + 4 * KV_WIDTH], ((pad, 0), (0, 0)))
    vst = vsp.reshape((S + pad) // Q_BLOCK, Q_BLOCK, G, HEAD_DIM).transpose(2, 0, 3, 1)
    vwt = proj[:, q_w + 5 * KV_WIDTH:q_w + 6 * KV_WIDTH].reshape(S, G, HEAD_DIM).transpose(1, 2, 0)
    r = np.arange(T)[None, :, None] // SLC_BLOCK
    onehot = jnp.asarray(np.arange(LANES)[None, None, :] == 2 * np.arange(4)[:, None, None] + r, BF16)

    def win_block(b, i):
        return jnp.maximum(b - (n_win - 1) + i, 0)

    in_specs = [
        pl.BlockSpec((Q_BLOCK, GG * KV_WIDTH), lambda g, b: (b, g)),
        pl.BlockSpec((Q_BLOCK, GG * LANES), lambda g, b: (b, g)),
        pl.BlockSpec((GG, n_cmp, HEAD_DIM), lambda g, b: (g, 0, 0)),
        pl.BlockSpec((GG, HEAD_DIM, n_cmp), lambda g, b: (g, 0, 0)),
        pl.BlockSpec((GG,) + ksp.shape[1:], lambda g, b: (g, 0, 0)),
        pl.BlockSpec((GG,) + vst.shape[1:], lambda g, b: (g, 0, 0, 0)),
        pl.BlockSpec(wselt.shape, lambda g, b: (0, 0)),
        pl.BlockSpec(onehot.shape, lambda g, b: (0, 0, 0)),
    ]
    in_specs += [pl.BlockSpec((Q_BLOCK, GG * HEAD_DIM), functools.partial(lambda g, b, i: (win_block(b, i), kw_col + g), i=i))
                 for i in range(n_win)]
    in_specs += [pl.BlockSpec((GG, HEAD_DIM, Q_BLOCK), functools.partial(lambda g, b, i: (g, 0, win_block(b, i)), i=i))
                 for i in range(n_win)]
    args = [proj, gates, kc, vct, ksp, vst, wselt, onehot] + [proj] * n_win + [vwt] * n_win
    per_group = lambda shape, dtype: [pltpu.VMEM(shape, dtype) for _ in range(GG)]
    scratch = (per_group((nsp + 2 * SEL_PAD, KV_WIDTH), F32)
               + per_group((T, KV_WIDTH), F32) + per_group((T, KV_WIDTH), F32)
               + per_group((T, KV_WIDTH), BF16) + per_group((T, KV_WIDTH), BF16)
               + per_group((HEAD_DIM, KV_WIDTH), F32))
    return pl.pallas_call(
        functools.partial(_nsa_attn_multi_kernel, n_sel=n_sel),
        grid=(G // GG, nb),
        in_specs=in_specs,
        out_specs=pl.BlockSpec((Q_BLOCK, GG * KV_WIDTH), lambda g, b: (b, g)),
        out_shape=jax.ShapeDtypeStruct((S, N_HEADS * HEAD_DIM), BF16),
        scratch_shapes=scratch,
        compiler_params=_cparams(("arbitrary", "arbitrary")),
        name="nsa_attention",
    )(*args)


def _matmul_residual_kernel(a_ref, w_ref, res_ref, o_ref):
    o_ref[...] = res_ref[...] + _dot(a_ref[...], w_ref[...])


def _matmul_residual(a, w, res, tm):
    S, K = a.shape
    N = w.shape[1]
    return pl.pallas_call(
        _matmul_residual_kernel,
        grid=(S // tm,),
        in_specs=[
            pl.BlockSpec((tm, K), lambda i: (i, 0)),
            pl.BlockSpec((K, N), lambda i: (0, 0)),
            pl.BlockSpec((tm, N), lambda i: (i, 0)),
        ],
        out_specs=pl.BlockSpec((tm, N), lambda i: (i, 0)),
        out_shape=jax.ShapeDtypeStruct((S, N), F32),
        compiler_params=_cparams(("arbitrary",)),
        name="out_proj_residual",
    )(a, w, res)


def _mem_kv_kernel(mem_ref, g_ref, wk_ref, wv_ref, k_ref, v_ref):
    mb = _rms_rows(mem_ref[...], g_ref[...]).astype(BF16)
    k_ref[...] = _dot(mb, wk_ref[...]).astype(BF16)
    v_ref[...] = _dot(mb, wv_ref[...]).astype(BF16)


def _mem_kv(mem, g, wk, wv):
    M, D = mem.shape
    W = wk.shape[1]
    full = lambda shape: pl.BlockSpec(shape, lambda i: (0, 0))
    return pl.pallas_call(
        _mem_kv_kernel,
        grid=(1,),
        in_specs=[full((M, D)), full((1, D)), full((D, W)), full((D, W))],
        out_specs=[full((M, W)), full((M, W))],
        out_shape=[jax.ShapeDtypeStruct((M, W), BF16)] * 2,
        compiler_params=_cparams(("arbitrary",)),
        name="mem_kv",
    )(mem, g, wk, wv)


def _mem_attn_kernel(h_ref, g_ref, wq_ref, k_ref, v_ref, wo_ref, o_ref):
    h = h_ref[...]
    ub = _rms_rows(h, g_ref[...]).astype(BF16)
    q = (_dot(ub, wq_ref[...]) * (MEM_HEAD_DIM ** -0.5)).astype(BF16)
    outs = []
    for hd in range(MEM_HEADS):
        cols = slice(hd * MEM_HEAD_DIM, (hd + 1) * MEM_HEAD_DIM)
        s = _dot_t(q[:, cols], k_ref[:, cols])
        e = jnp.exp(s - jnp.max(s, axis=1, keepdims=True))
        p = e / jnp.sum(e, axis=1, keepdims=True)
        outs.append(_dot(p.astype(BF16), v_ref[:, cols]).astype(BF16))
    o = jnp.concatenate(outs, axis=1)
    o_ref[...] = h + _dot(o, wo_ref[...])


def _mem_attn(h, g, wq, k, v, wo, tm):
    S, D = h.shape
    W = wq.shape[1]
    M = k.shape[0]
    full = lambda shape: pl.BlockSpec(shape, lambda i: (0, 0))
    return pl.pallas_call(
        _mem_attn_kernel,
        grid=(S // tm,),
        in_specs=[pl.BlockSpec((tm, D), lambda i: (i, 0)), full((1, D)), full((D, W)),
                  full((M, W)), full((M, W)), full((W, D))],
        out_specs=pl.BlockSpec((tm, D), lambda i: (i, 0)),
        out_shape=jax.ShapeDtypeStruct((S, D), F32),
        compiler_params=_cparams(("arbitrary",)),
        name="mem_attention",
    )(h, g, wq, k, v, wo)


def _swiglu_tile(ub, wg, wu, wd):
    gate = _dot(ub, wg)
    up = _dot(ub, wu)
    act = (gate * jax.nn.sigmoid(gate)) * up
    return _dot(act.astype(BF16), wd)


def _ffn_kernel(h_ref, g_ref, wg_ref, wu_ref, wd_ref, o_ref, u_ref):
    f = pl.program_id(1)

    @pl.when(f == 0)
    def _():
        h = h_ref[...]
        u_ref[...] = _rms_rows(h, g_ref[...]).astype(BF16)
        o_ref[...] = h

    o_ref[...] += _swiglu_tile(u_ref[...], wg_ref[...], wu_ref[...], wd_ref[...])


def _ffn(h, g, wg, wu, wd, tm, tf):
    S, D = h.shape
    F = wg.shape[1]
    return pl.pallas_call(
        _ffn_kernel,
        grid=(S // tm, F // tf),
        in_specs=[
            pl.BlockSpec((tm, D), lambda i, f: (i, 0)),
            pl.BlockSpec((1, D), lambda i, f: (0, 0)),
            pl.BlockSpec((D, tf), lambda i, f: (0, f)),
            pl.BlockSpec((D, tf), lambda i, f: (0, f)),
            pl.BlockSpec((tf, D), lambda i, f: (f, 0)),
        ],
        out_specs=pl.BlockSpec((tm, D), lambda i, f: (i, 0)),
        out_shape=jax.ShapeDtypeStruct((S, D), F32),
        scratch_shapes=[pltpu.VMEM((tm, D), BF16)],
        compiler_params=_cparams(("arbitrary", "arbitrary")),
        name="dense_swiglu",
    )(h, g, wg, wu, wd)


POOL_HALO = 16


def _pool_kernel(h_ref, halo_ref, g_ref, w_ref, b_ref, scale_ref, o_ref, ext_ref):
    i = pl.program_id(0)
    tm = h_ref.shape[0]
    h = h_ref[...]
    g = g_ref[...]
    u = _rms_rows(h, g)
    halo = _rms_rows(halo_ref[...], g)
    ext_ref[0:POOL_HALO, :] = jnp.where(i > 0, halo, 0.0)
    ext_ref[POOL_HALO:, :] = u
    pos = i * tm + lax.broadcasted_iota(jnp.int32, (tm, 1), 0)
    for gi, w in enumerate(POOL_WINDOWS):
        cols = slice(gi * POOL_GROUP, (gi + 1) * POOL_GROUP)
        tot = u[:, cols]
        for k in range(1, w):
            tot = tot + ext_ref[POOL_HALO - k:POOL_HALO - k + tm, cols]
        cnt = jnp.minimum(pos + 1, w).astype(F32)
        d = tot / cnt - u[:, cols]
        z = _dot(d.astype(BF16), w_ref[gi]) + b_ref[:, cols]
        o_ref[:, cols] = h[:, cols] + z * scale_ref[:, cols]


def _pool_mixer(h, g, w, b, scale, tm):
    S, D = h.shape
    ratio = tm // POOL_HALO
    return pl.pallas_call(
        _pool_kernel,
        grid=(S // tm,),
        in_specs=[
            pl.BlockSpec((tm, D), lambda i: (i, 0)),
            pl.BlockSpec((POOL_HALO, D), lambda i: (jnp.maximum(i * ratio - 1, 0), 0)),
            pl.BlockSpec((1, D), lambda i: (0, 0)),
            pl.BlockSpec(w.shape, lambda i: (0, 0, 0)),
            pl.BlockSpec((1, D), lambda i: (0, 0)),
            pl.BlockSpec((1, D), lambda i: (0, 0)),
        ],
        out_specs=pl.BlockSpec((tm, D), lambda i: (i, 0)),
        out_shape=jax.ShapeDtypeStruct((S, D), F32),
        scratch_shapes=[pltpu.VMEM((tm + POOL_HALO, D), F32)],
        compiler_params=_cparams(("arbitrary",)),
        name="pool_mixer",
    )(h, h, g, w, b, scale)


def _router_kernel(h_ref, g_ref, rhi_ref, rlo_ref, u_ref, idx_ref, w_ref):
    u = _rms_rows(h_ref[...], g_ref[...])
    u_ref[...] = u
    u_hi = u.astype(BF16)
    u_lo = (u - u_hi.astype(F32)).astype(BF16)
    rhi = rhi_ref[...]
    logits = (_dot(u_hi, rhi) + _dot(u_lo, rhi)) + _dot(u_hi, rlo_ref[...])
    lane = lax.broadcasted_iota(jnp.int32, logits.shape, 1)
    lane_f = lane.astype(F32)
    s = jnp.where(lane < N_EXPERTS, logits, REMOVED)
    m1 = jnp.max(s, axis=1, keepdims=True)
    i1 = jnp.min(jnp.where(s == m1, lane_f, float(LANES)), axis=1, keepdims=True)
    s = jnp.where(lane_f == i1, REMOVED, s)
    m2 = jnp.max(s, axis=1, keepdims=True)
    i2 = jnp.min(jnp.where(s == m2, lane_f, float(LANES)), axis=1, keepdims=True)
    e2 = jnp.exp(m2 - m1)
    den = 1.0 + e2
    idx_ref[...] = jnp.where(lane == 0, i1, jnp.where(lane == 1, i2, 0.0)).astype(jnp.int32)
    w_ref[...] = jnp.where(lane == 0, 1.0 / den, jnp.where(lane == 1, e2 / den, 0.0))


def _router(h, g, r_hi, r_lo, tm):
    S, D = h.shape
    return pl.pallas_call(
        _router_kernel,
        grid=(S // tm,),
        in_specs=[
            pl.BlockSpec((tm, D), lambda i: (i, 0)),
            pl.BlockSpec((1, D), lambda i: (0, 0)),
            pl.BlockSpec((D, LANES), lambda i: (0, 0)),
            pl.BlockSpec((D, LANES), lambda i: (0, 0)),
        ],
        out_specs=[
            pl.BlockSpec((tm, D), lambda i: (i, 0)),
            pl.BlockSpec((tm, LANES), lambda i: (i, 0)),
            pl.BlockSpec((tm, LANES), lambda i: (i, 0)),
        ],
        out_shape=[
            jax.ShapeDtypeStruct((S, D), F32),
            jax.ShapeDtypeStruct((S, LANES), jnp.int32),
            jax.ShapeDtypeStruct((S, LANES), F32),
        ],
        compiler_params=_cparams(("arbitrary",)),
        name="moe_router",
    )(h, g, r_hi, r_lo)


GATHER_UNROLL = 8


def _start_row_gather(src_hbm, dst_vmem, sem, n_rows, src_row_of):
    def body(r, c):
        pltpu.make_async_copy(src_hbm.at[pl.ds(src_row_of(r), 1), :], dst_vmem.at[pl.ds(r, 1), :], sem).start()
        return c

    lax.fori_loop(0, n_rows, body, 0, unroll=GATHER_UNROLL)


def _wait_row_gather(src_hbm, dst_vmem, sem):
    pltpu.make_async_copy(src_hbm.at[pl.ds(0, dst_vmem.shape[0]), :], dst_vmem, sem).wait()


def _moe_expert_kernel(tok_ref, exp_ref, nused_ref, u_hbm, wg_ref, wu_ref, wd_ref, o_ref, x_ref, xb_ref, sem):
    i = pl.program_id(0)
    f = pl.program_id(1)
    rows = x_ref.shape[1]
    n_used = nused_ref[0]
    active = i < n_used
    slot = i & 1

    def gather(block, into):
        _start_row_gather(u_hbm, x_ref.at[into], sem.at[into], rows, lambda r: tok_ref[block * rows + r])

    @pl.when((i == 0) & (f == 0) & active)
    def _():
        gather(0, 0)

    @pl.when(active & (f == 0))
    def _():
        _wait_row_gather(u_hbm, x_ref.at[slot], sem.at[slot])
        xb_ref[...] = x_ref[slot].astype(BF16)
        o_ref[...] = jnp.zeros_like(o_ref)

        @pl.when(i + 1 < n_used)
        def _():
            gather(i + 1, 1 - slot)

    @pl.when(active)
    def _():
        o_ref[...] += _swiglu_tile(xb_ref[...], wg_ref[0], wu_ref[0], wd_ref[0])

    @pl.when(jnp.logical_not(active) & (f == 0))
    def _():
        o_ref[...] = jnp.zeros_like(o_ref)


def _moe_experts(tok, blk_exp, n_used, u, wg, wu, wd, tf):
    N, D = u.shape
    n_blk = blk_exp.shape[0]
    F = wg.shape[2]
    rows = MOE_BLOCK
    n_f = F // tf

    def ftile(i, f, nu):
        return jnp.where(i < nu[0], f, n_f - 1)

    grid_spec = pltpu.PrefetchScalarGridSpec(
        num_scalar_prefetch=3,
        grid=(n_blk, n_f),
        in_specs=[
            pl.BlockSpec(memory_space=pl.ANY),
            pl.BlockSpec((1, D, tf), lambda i, f, tok, ex, nu: (ex[i], 0, ftile(i, f, nu))),
            pl.BlockSpec((1, D, tf), lambda i, f, tok, ex, nu: (ex[i], 0, ftile(i, f, nu))),
            pl.BlockSpec((1, tf, D), lambda i, f, tok, ex, nu: (ex[i], ftile(i, f, nu), 0)),
        ],
        out_specs=pl.BlockSpec((rows, D), lambda i, f, tok, ex, nu: (i, 0)),
        scratch_shapes=[pltpu.VMEM((2, rows, D), F32), pltpu.VMEM((rows, D), BF16), pltpu.SemaphoreType.DMA((2,))],
    )
    return pl.pallas_call(
        _moe_expert_kernel,
        grid_spec=grid_spec,
        out_shape=jax.ShapeDtypeStruct((n_blk * rows, D), F32),
        compiler_params=_cparams(("arbitrary", "arbitrary")),
        name="moe_experts",
    )(tok, blk_exp, n_used, u, wg, wu, wd)


def _moe_combine_kernel(pos_ref, h_ref, w_ref, gfin_ref, y_hbm, o_ref, r_ref, sem):
    i = pl.program_id(0)
    n_steps = pl.num_programs(0)
    tm = h_ref.shape[0]
    slot = i & 1

    def gather(step, into):
        for k in range(TOP_K):
            _start_row_gather(y_hbm, r_ref.at[into, k], sem.at[into, k], tm,
                              lambda r: pos_ref[(step * tm + r) * TOP_K + k])

    @pl.when(i == 0)
    def _():
        gather(0, 0)

    for k in range(TOP_K):
        _wait_row_gather(y_hbm, r_ref.at[slot, k], sem.at[slot, k])

    @pl.when(i + 1 < n_steps)
    def _():
        gather(i + 1, 1 - slot)

    w = w_ref[...]
    y = h_ref[...] + (r_ref[slot, 0] * w[:, 0:1] + r_ref[slot, 1] * w[:, 1:2])
    o_ref[...] = _rms_rows(y, gfin_ref[...])


def _moe_combine(pos, h, w, g_final, y, tm):
    S, D = h.shape
    grid_spec = pltpu.PrefetchScalarGridSpec(
        num_scalar_prefetch=1,
        grid=(S // tm,),
        in_specs=[
            pl.BlockSpec((tm, D), lambda i, pos: (i, 0)),
            pl.BlockSpec((tm, LANES), lambda i, pos: (i, 0)),
            pl.BlockSpec((1, D), lambda i, pos: (0, 0)),
            pl.BlockSpec(memory_space=pl.ANY),
        ],
        out_specs=pl.BlockSpec((tm, D), lambda i, pos: (i, 0)),
        scratch_shapes=[pltpu.VMEM((2, TOP_K, tm, D), F32), pltpu.SemaphoreType.DMA((2, TOP_K))],
    )
    return pl.pallas_call(
        _moe_combine_kernel,
        grid_spec=grid_spec,
        out_shape=jax.ShapeDtypeStruct((S, D), F32),
        compiler_params=_cparams(("arbitrary",)),
        name="moe_combine_norm",
    )(pos, h, w, g_final, y)


def _moe_dispatch_indices(top_idx):
    N = top_idx.shape[0]
    e_flat = top_idx.reshape(-1)
    onehot = (e_flat[:, None] == jnp.arange(N_EXPERTS, dtype=jnp.int32)[None, :]).astype(jnp.int32)
    csum = jnp.cumsum(onehot, axis=0)
    counts = csum[-1]
    rank = jnp.take_along_axis(csum, e_flat[:, None], axis=1)[:, 0] - 1
    padded = (counts + MOE_BLOCK - 1) // MOE_BLOCK * MOE_BLOCK
    pend = jnp.cumsum(padded)
    poff = pend - padded
    dest = poff[e_flat] + rank
    n_blk = (N * TOP_K + MOE_BLOCK - 1) // MOE_BLOCK + N_EXPERTS
    cap = n_blk * MOE_BLOCK
    t_flat = jnp.repeat(jnp.arange(N, dtype=jnp.int32), TOP_K)
    tok_buf = jnp.zeros((cap,), jnp.int32).at[dest].set(t_flat)
    blk_start = jnp.arange(n_blk, dtype=jnp.int32) * MOE_BLOCK
    blk_exp = jnp.minimum(jnp.sum(pend[None, :] <= blk_start[:, None], axis=1), N_EXPERTS - 1).astype(jnp.int32)
    n_used = (pend[-1] // MOE_BLOCK).astype(jnp.int32).reshape(1)
    return tok_buf, blk_exp, n_used, dest.astype(jnp.int32)


def _rope_tables(pos):
    half = HEAD_DIM // 2
    inv = ROPE_THETA ** (-jnp.arange(half, dtype=F32) / half)
    ang = pos.astype(F32)[:, None] * inv[None, :]
    cos, sin = jnp.cos(ang), jnp.sin(ang)
    return jnp.concatenate([cos, cos], axis=1), jnp.concatenate([-sin, sin], axis=1)


def _gate_params(w_gates, gate_b):
    D = w_gates.shape[0]
    w = w_gates.reshape(D, 3, N_KV_GROUPS, HEADS_PER_GROUP).transpose(0, 2, 1, 3).reshape(D, N_KV_GROUPS, 3 * HEADS_PER_GROUP)
    w = jnp.pad(w, ((0, 0), (0, 0), (0, LANES - 3 * HEADS_PER_GROUP))).reshape(D, N_KV_GROUPS * LANES)
    b = gate_b.reshape(3, N_KV_GROUPS, HEADS_PER_GROUP).transpose(1, 0, 2).reshape(N_KV_GROUPS, 3 * HEADS_PER_GROUP)
    b = jnp.pad(b, ((0, 0), (0, LANES - 3 * HEADS_PER_GROUP))).reshape(1, N_KV_GROUPS * LANES)
    return w.astype(BF16), b


def _chunked_groups(t):
    S = t.shape[0]
    return (t.reshape(S // CMP_STRIDE, CMP_STRIDE, N_KV_GROUPS, HEAD_DIM)
            .transpose(2, 0, 1, 3).reshape(N_KV_GROUPS, S // CMP_STRIDE, CMP_STRIDE * HEAD_DIM))


def _row_tile(S, want):
    t = min(S, want)
    assert S % t == 0
    return t


def kernel(x, mem, norm_mix, norm_mem_q, norm_mem_kv, norm_ffn, norm_final, nsa_w_in, nsa_gate_b, nsa_pe_k, nsa_pe_v, nsa_cmp_k_w1, nsa_cmp_k_w2, nsa_cmp_v_w1, nsa_cmp_v_w2, nsa_w_out, pool_w, pool_b, pool_scale, mem_wq, mem_wk, mem_wv, mem_wo, ffn_w_gate, ffn_w_up, ffn_w_down, moe_router, moe_w_gate, moe_w_up, moe_w_down):
    B, S, D = x.shape
    assert B == 1 and S % WINDOW == 0 and S >= SLC_BLOCK * N_SLC
    h = x.reshape(S, D)
    memf = mem.reshape(mem.shape[1], D)
    row = lambda v: v.reshape(1, -1)

    n_main = N_HEADS * HEAD_DIM + 6 * KV_WIDTH
    w_in = nsa_w_in[0]
    w_gate, b_gate = _gate_params(w_in[:, n_main:], nsa_gate_b[0])
    cos, sin = _rope_tables(jnp.arange(S))
    proj, gates = _nsa_proj(h, row(norm_mix[0]), w_in[:, :n_main].astype(BF16), w_gate, b_gate, cos, sin,
                            _row_tile(S, 1024))
    q_w = N_HEADS * HEAD_DIM
    n_chunks = S // CMP_STRIDE
    cmp_pos = jnp.arange(n_chunks) * CMP_STRIDE + CMP_BLOCK - 1
    ccos, csin = _rope_tables(cmp_pos)
    kc = _compress(_chunked_groups(proj[:, q_w:q_w + KV_WIDTH]), nsa_pe_k[0].reshape(1, -1),
                   nsa_cmp_k_w1[0].astype(BF16), nsa_cmp_k_w2[0].astype(BF16), ccos, csin, True)
    vct = _compress(_chunked_groups(proj[:, q_w + KV_WIDTH:q_w + 2 * KV_WIDTH]), nsa_pe_v[0].reshape(1, -1),
                    nsa_cmp_v_w1[0].astype(BF16), nsa_cmp_v_w2[0].astype(BF16), ccos, csin, False).transpose(0, 2, 1)
    attn = _nsa_attention(proj, gates, kc, vct)
    h = _matmul_residual(attn, nsa_w_out[0].astype(BF16), h, _row_tile(S, 512))

    def mem_layer(h, i):
        k, v = _mem_kv(memf, row(norm_mem_kv[i]), mem_wk[i].astype(BF16), mem_wv[i].astype(BF16))
        return _mem_attn(h, row(norm_mem_q[i]), mem_wq[i].astype(BF16), k, v, mem_wo[i].astype(BF16),
                         _row_tile(S, 512))

    h = mem_layer(h, 0)
    h = _ffn(h, row(norm_ffn[0]), ffn_w_gate[0].astype(BF16), ffn_w_up[0].astype(BF16),
             ffn_w_down[0].astype(BF16), _row_tile(S, 1024), 512)

    h = _pool_mixer(h, row(norm_mix[1]), pool_w[0].astype(BF16), row(pool_b[0]), row(pool_scale[0]),
                    _row_tile(S, 512))
    h = mem_layer(h, 1)

    r = jnp.pad(moe_router[0], ((0, 0), (0, LANES - N_EXPERTS)))
    r_hi = r.astype(BF16)
    r_lo = (r - r_hi.astype(F32)).astype(BF16)
    u, idx, gate_w = _router(h, row(norm_ffn[1]), r_hi, r_lo, _row_tile(S, 512))
    tok_buf, blk_exp, n_used, dest = _moe_dispatch_indices(idx[:, :TOP_K])
    y = _moe_experts(tok_buf, blk_exp, n_used, u, moe_w_gate[0].astype(BF16), moe_w_up[0].astype(BF16),
                     moe_w_down[0].astype(BF16), 512)
    out = _moe_combine(dest, h, gate_w, row(norm_final), y, _row_tile(S, 256))
    return out.reshape(B, S, D)
```

```python
import functools

import numpy as np
import jax
import jax.numpy as jnp
from jax import lax
from jax.experimental import pallas as pl
from jax.experimental.pallas import tpu as pltpu

F32 = jnp.float32
BF16 = jnp.bfloat16

N_HEADS = 16
HEAD_DIM = 128
N_KV_GROUPS = 4
HEADS_PER_GROUP = 4
KV_WIDTH = N_KV_GROUPS * HEAD_DIM
CMP_BLOCK = 32
CMP_STRIDE = 16
SLC_BLOCK = 64
RATIO = SLC_BLOCK // CMP_STRIDE
N_SLC = 16
WINDOW = 512
Q_BLOCK = 128
ROPE_THETA = 10000.0
POOL_WINDOWS = (2, 4, 8, 16)
POOL_GROUP = 512
MEM_HEADS = 4
MEM_HEAD_DIM = 128
N_EXPERTS = 8
TOP_K = 2
MOE_BLOCK = 512
EPS = 1e-6
NEG_INF = -1e30
FORCE = 1e9
REMOVED = -3e38
LOG2E = float(np.log2(np.e))

LANES = 128
VMEM_LIMIT = 56 * 1024 * 1024

SEL_KEY_TILE = 512


def _cparams(sem, flags=None):
    return pltpu.CompilerParams(dimension_semantics=sem, vmem_limit_bytes=VMEM_LIMIT, flags=flags)


def _rms_rows(xf, g):
    r = lax.rsqrt(jnp.mean(xf * xf, axis=-1, keepdims=True) + EPS)
    return (xf * r) * g


def _dot(a, b):
    return jnp.dot(a, b, preferred_element_type=F32)


def _dot_t(a, b):
    return lax.dot_general(a, b, (((1,), (1,)), ((), ())), preferred_element_type=F32)


def _proj_kernel(x_ref, g_ref, w_ref, wgate_ref, bgate_ref, cos_ref, sin_ref, o_ref, gates_ref, u_ref):
    j = pl.program_id(1)

    @pl.when(j == 0)
    def _():
        ub = _rms_rows(x_ref[...], g_ref[...]).astype(BF16)
        u_ref[...] = ub
        gates_ref[...] = jax.nn.sigmoid(_dot(ub, wgate_ref[...]) + bgate_ref[...])

    acc = _dot(u_ref[...], w_ref[...])
    n_q_tiles = N_HEADS * HEAD_DIM // KV_WIDTH
    is_q = j < n_q_tiles
    is_rope = is_q | (j == n_q_tiles + 2) | (j == n_q_tiles + 4)

    @pl.when(is_rope)
    def _():
        scale = jnp.where(is_q, HEAD_DIM ** -0.5 * LOG2E, 1.0).astype(F32)
        cos = cos_ref[...]
        sin = sin_ref[...]
        for h in range(KV_WIDTH // HEAD_DIM):
            xh = acc[:, h * HEAD_DIM:(h + 1) * HEAD_DIM]
            rot = pltpu.roll(xh, HEAD_DIM // 2, axis=1)
            o_ref[:, h * HEAD_DIM:(h + 1) * HEAD_DIM] = ((xh * cos + rot * sin) * scale).astype(BF16)

    @pl.when(jnp.logical_not(is_rope))
    def _():
        o_ref[...] = acc.astype(BF16)


def _nsa_proj(x, g, w_main, w_gate, b_gate, cos, sin, tm):
    S, D = x.shape
    n_main = w_main.shape[1]
    tn = KV_WIDTH
    n_gate = w_gate.shape[1]
    return pl.pallas_call(
        _proj_kernel,
        grid=(S // tm, n_main // tn),
        in_specs=[
            pl.BlockSpec((tm, D), lambda i, j: (i, 0)),
            pl.BlockSpec((1, D), lambda i, j: (0, 0)),
            pl.BlockSpec((D, tn), lambda i, j: (0, j)),
            pl.BlockSpec((D, n_gate), lambda i, j: (0, 0)),
            pl.BlockSpec((1, n_gate), lambda i, j: (0, 0)),
            pl.BlockSpec((tm, HEAD_DIM), lambda i, j: (i, 0)),
            pl.BlockSpec((tm, HEAD_DIM), lambda i, j: (i, 0)),
        ],
        out_specs=[
            pl.BlockSpec((tm, tn), lambda i, j: (i, j)),
            pl.BlockSpec((tm, n_gate), lambda i, j: (i, 0)),
        ],
        out_shape=[
            jax.ShapeDtypeStruct((S, n_main), BF16),
            jax.ShapeDtypeStruct((S, n_gate), F32),
        ],
        scratch_shapes=[pltpu.VMEM((tm, D), BF16)],
        compiler_params=_cparams(("arbitrary", "arbitrary")),
        name="nsa_proj",
    )(x, g, w_main, w_gate, b_gate, cos, sin)


def _gelu_tanh(x):
    return 0.5 * x * (1.0 + jnp.tanh(np.sqrt(2.0 / np.pi).astype(np.float32) * (x + 0.044715 * (x * x * x))))


def _compress_kernel(t_ref, pe_ref, w1_ref, w2_ref, cos_ref, sin_ref, o_ref, *, apply_rope):
    half = CMP_STRIDE * HEAD_DIM
    t = t_ref[0].astype(F32)
    n = t.shape[0]
    a = _dot((t + pe_ref[:, :half]).astype(BF16), w1_ref[:half, :])
    b = _dot((t + pe_ref[:, half:]).astype(BF16), w1_ref[half:, :])
    hid = a + pltpu.roll(b, n - 1, axis=0)
    out = _dot(_gelu_tanh(hid).astype(BF16), w2_ref[...])
    if apply_rope:
        rot = pltpu.roll(out, HEAD_DIM // 2, axis=1)
        out = out * cos_ref[...] + rot * sin_ref[...]
    o_ref[0] = out.astype(BF16)


def _compress(t, pe_flat, w1, w2, cos, sin, apply_rope):
    G, n, width = t.shape
    hidden = w1.shape[1]
    out_block = (1, n, HEAD_DIM)
    return pl.pallas_call(
        functools.partial(_compress_kernel, apply_rope=apply_rope),
        grid=(G,),
        in_specs=[
            pl.BlockSpec((1, n, width), lambda g: (g, 0, 0)),
            pl.BlockSpec((1, 2 * width), lambda g: (0, 0)),
            pl.BlockSpec((2 * width, hidden), lambda g: (0, 0)),
            pl.BlockSpec((hidden, HEAD_DIM), lambda g: (0, 0)),
            pl.BlockSpec((n, HEAD_DIM), lambda g: (0, 0)),
            pl.BlockSpec((n, HEAD_DIM), lambda g: (0, 0)),
        ],
        out_specs=pl.BlockSpec(out_block, lambda g: (g, 0, 0)),
        out_shape=jax.ShapeDtypeStruct((G,) + out_block[1:], BF16),
        compiler_params=_cparams(("arbitrary",)),
        name="nsa_compress_k" if apply_rope else "nsa_compress_v",
    )(t, pe_flat, w1, w2, cos, sin)


def _softmax_cols(s):
    e = jnp.exp2(s - jnp.max(s, axis=0, keepdims=True))
    return e, jnp.sum(e, axis=0, keepdims=True)


def _tile_heads(x):
    return jnp.concatenate([x] * HEADS_PER_GROUP, axis=1)


SEL_PAD = 16
V_EXTRA_ROWS = 16


def _run_on_shortest_prefix(fn, total, needed, parts):
    piece = total // parts
    sizes = [piece * k for k in range(1, parts + 1)] if piece % LANES == 0 else [total]
    for idx, n in enumerate(sizes):
        cond = needed > (sizes[idx - 1] if idx else 0)
        if idx < len(sizes) - 1:
            cond = cond & (needed <= n)
        pl.when(cond)(functools.partial(fn, n))


def _nsa_attn_kernel(q_ref, gates_ref, kc_ref, vct_ref, ks_ref, vst_ref, wselt_ref, onehot_ref, *rest, n_sel):
    n_win = WINDOW // Q_BLOCK + 1
    kw_refs = rest[:n_win]
    vwt_refs = rest[n_win:2 * n_win]
    o_ref = rest[2 * n_win]
    gall_ref, s0_ref, s1_ref, p0_ref, p1_ref, acc_ref, oc_ref, sblk_ref = rest[2 * n_win + 1:]

    b = pl.program_id(1)
    t0 = b * Q_BLOCK
    Q = Q_BLOCK
    H = HEADS_PER_GROUP
    C = H * Q
    q = q_ref[...]
    qt = jnp.concatenate(
        [q[:, h * HEAD_DIM:(h + 1) * HEAD_DIM].astype(F32).T.astype(BF16) for h in range(H)], axis=1)

    n_cmp = kc_ref.shape[1]
    nsp = wselt_ref.shape[0]
    qi_row = lax.broadcasted_iota(jnp.int32, (1, C), 1) & (Q - 1)

    def compressed(n):
        qi_c = lax.broadcasted_iota(jnp.int32, (n, Q), 1)
        cpos = lax.broadcasted_iota(jnp.int32, (n, Q), 0) * CMP_STRIDE + (CMP_BLOCK - 1)
        bias_c = jnp.where(cpos <= t0 + qi_c, 0.0, NEG_INF)
        e_c, l_c = _softmax_cols(_dot(kc_ref[0, 0:n, :], qt) + _tile_heads(bias_c))
        p_c = e_c * jnp.where(t0 + qi_row >= CMP_BLOCK - 1, 1.0 / l_c, 0.0)
        oc_ref[...] = _dot(vct_ref[0, :, 0:n], p_c.astype(BF16))
        imp = (p_c[:, 0:Q] + p_c[:, Q:2 * Q]) + p_c[:, 2 * Q:3 * Q] + p_c[:, 3 * Q:4 * Q]
        imp_hi = imp.astype(BF16)
        imp_lo = (imp - imp_hi.astype(F32)).astype(BF16)
        wselt = wselt_ref[:, 0:n]
        sblk_ref[...] = _dot(wselt, imp_hi) + _dot(wselt, imp_lo)

    _run_on_shortest_prefix(compressed, n_cmp, (t0 + Q - CMP_BLOCK) // CMP_STRIDE + 1, 4)
    o_c = oc_ref[...]

    def select(rows):
        blk = lax.broadcasted_iota(jnp.int32, (rows, Q), 0)
        qp = t0 + lax.broadcasted_iota(jnp.int32, (rows, Q), 1)
        cur = qp >> 6
        valid = blk * SLC_BLOCK <= qp
        forced = (blk == 0) | (blk == cur) | (blk == cur - 1)
        s_blk = jnp.where(forced, FORCE, jnp.where(valid, sblk_ref[0:rows, :], NEG_INF))
        blk_f = blk.astype(F32)

        def pick(_, carry):
            s, sel = carry
            m = jnp.max(s, axis=0, keepdims=True)
            first = jnp.min(jnp.where(s == m, blk_f, float(rows)), axis=0, keepdims=True)
            hit = blk_f == first
            return jnp.where(hit, REMOVED, s), jnp.where(hit, 1.0, sel)

        _, sel = lax.fori_loop(0, n_sel, pick, (s_blk, jnp.zeros((rows, Q), F32)))
        gall_ref[SEL_PAD:SEL_PAD + rows, :] = _tile_heads(jnp.where((sel > 0.5) & valid, 0.0, NEG_INF))
        if rows < nsp:
            gall_ref[SEL_PAD + rows:SEL_PAD + nsp, :] = jnp.full((nsp - rows, C), NEG_INF, F32)

    _run_on_shortest_prefix(select, nsp, (t0 + Q - 1) // SLC_BLOCK + 1, 2)

    T = SEL_KEY_TILE
    bpt = T // SLC_BLOCK
    n_tiles = (t0 + Q + T - 1) // T
    gall_ref[0:SEL_PAD, :] = jnp.full((SEL_PAD, C), NEG_INF, F32)
    gall_ref[SEL_PAD + nsp:, :] = jnp.full((SEL_PAD, C), NEG_INF, F32)
    onehot = onehot_ref[(b + 1) & 3]
    zero_rows = jnp.zeros((HEAD_DIM - 2 * bpt, C), BF16)
    k_rows = ks_ref.shape[1]
    v_blocks = vst_ref.shape[1]

    def scores(kt):
        start = pl.multiple_of(jnp.clip(t0 - kt * T, 0, k_rows - T), Q)
        first_blk = 2 * b + 2 - bpt * (kt + 1) + SEL_PAD
        w0 = pl.multiple_of(jnp.maximum((first_blk >> 3) << 3, 0), 8)
        table = gall_ref[pl.ds(w0, 2 * bpt), :].astype(BF16)
        lhs = jnp.concatenate([ks_ref[0, pl.ds(start, T), :], onehot], axis=1)
        rhs = jnp.concatenate([qt, table, zero_rows], axis=0)
        return _dot(lhs, rhs)

    def values(kt):
        vb = jnp.clip(b - (T // Q) * kt, 0, v_blocks - T // Q)
        return jnp.concatenate([vst_ref[0, vb + c] for c in range(T // Q)], axis=1)

    def tile_step(kt, s_cur, s_nxt, p_cur, p_prv, carry):
        alpha_prev, m = carry
        acc_ref[...] = alpha_prev * acc_ref[...] + _dot(values(kt - 1), p_prv[...])
        s_nxt[...] = scores(kt + 1)
        s = s_cur[...]
        m_new = jnp.maximum(m, jnp.max(s, axis=0, keepdims=True))
        p_cur[...] = jnp.exp2((s - m_new).astype(BF16))
        return jnp.exp2(m - m_new), m_new

    def pair_step(j, carry):
        carry = tile_step(2 * j, s0_ref, s1_ref, p0_ref, p1_ref, carry)
        return tile_step(2 * j + 1, s1_ref, s0_ref, p1_ref, p0_ref, carry)

    s_first = scores(0)
    key_j = lax.broadcasted_iota(jnp.int32, (Q, Q), 0)
    qry_i = lax.broadcasted_iota(jnp.int32, (Q, Q), 1)
    s0_ref[0:T - Q, :] = s_first[0:T - Q]
    s0_ref[T - Q:, :] = s_first[T - Q:] + _tile_heads(jnp.where(key_j <= qry_i, 0.0, NEG_INF))
    p1_ref[...] = jnp.zeros_like(p1_ref)
    acc_ref[...] = jnp.zeros_like(acc_ref)
    n_pairs = (n_tiles + 1) // 2
    alpha_last, _ = lax.fori_loop(
        0, n_pairs, pair_step, (jnp.ones((1, C), F32), jnp.full((1, C), NEG_INF, F32)))
    acc_s = alpha_last * acc_ref[...] + _dot(values(2 * n_pairs - 1), p1_ref[...])
    o_s = acc_s[0:HEAD_DIM] * (1.0 / acc_s[HEAD_DIM:HEAD_DIM + 1])

    kw = jnp.concatenate([r[...] for r in kw_refs], axis=0)
    vwt = jnp.concatenate([r[0] for r in vwt_refs], axis=1)
    nw = WINDOW + Q
    row_w = lax.broadcasted_iota(jnp.int32, (nw, Q), 0)
    diff = lax.broadcasted_iota(jnp.int32, (nw, Q), 1) - row_w + WINDOW
    ok_w = (diff >= 0) & (diff < WINDOW) & (row_w >= WINDOW - t0)
    s_w = _dot(kw, qt) + _tile_heads(jnp.where(ok_w, 0.0, NEG_INF))
    e_w = jnp.exp2((s_w - jnp.max(s_w, axis=0, keepdims=True)).astype(BF16))
    acc_w = _dot(vwt, e_w)
    o_w = acc_w[0:HEAD_DIM] * (1.0 / acc_w[HEAD_DIM:HEAD_DIM + 1])

    gt = gates_ref[...].T
    for h in range(H):
        cols = slice(h * Q, (h + 1) * Q)
        o = (gt[h:h + 1] * o_c[:, cols] + gt[H + h:H + h + 1] * o_s[:, cols]
             + gt[2 * H + h:2 * H + h + 1] * o_w[:, cols])
        o_ref[:, h * HEAD_DIM:(h + 1) * HEAD_DIM] = o.T.astype(BF16)


def _selection_matrix_t(n_cmp, nsp):
    j = np.arange(nsp)[:, None]
    c = np.arange(n_cmp)[None, :]
    d = c - RATIO * j
    w = np.where((d == -1) | (d == RATIO - 1), 1.0, np.where((d >= 0) & (d < RATIO - 1), 2.0, 0.0))
    return jnp.asarray(w, BF16)


def _nsa_attention(proj, gates, kc, vct):
    S = proj.shape[0]
    G = N_KV_GROUPS
    T = SEL_KEY_TILE
    nb = S // Q_BLOCK
    n_cmp = kc.shape[1]
    ns = S // SLC_BLOCK
    nsp = -(-ns // LANES) * LANES
    n_sel = min(N_SLC, ns)
    wselt = _selection_matrix_t(n_cmp, nsp)
    n_win = WINDOW // Q_BLOCK + 1
    cpb = KV_WIDTH // HEAD_DIM
    q_w = N_HEADS * HEAD_DIM
    q_cols = q_w // HEAD_DIM
    kw_col = q_cols + 4 * cpb
    pad = T - Q_BLOCK
    ksp = jnp.pad(proj[:, q_w + 2 * KV_WIDTH:q_w + 3 * KV_WIDTH], ((pad, 0), (0, 0)))
    ksp = ksp.reshape(S + pad, G, HEAD_DIM).transpose(1, 0, 2)
    vsp = jnp.pad(proj[:, q_w + 3 * KV_WIDTH:q_w + 4 * KV_WIDTH], ((pad, 0), (0, 0)))
    vst = vsp.reshape((S + pad) // Q_BLOCK, Q_BLOCK, G, HEAD_DIM).transpose(2, 0, 3, 1)
    extra = jnp.zeros(vst.shape[:2] + (V_EXTRA_ROWS, Q_BLOCK), BF16).at[:, :, 0, :].set(1.0)
    vst = jnp.concatenate([vst, extra], axis=2)
    vwt = proj[:, q_w + 5 * KV_WIDTH:q_w + 6 * KV_WIDTH].reshape(S, G, HEAD_DIM).transpose(1, 2, 0)
    vwt = jnp.concatenate([vwt, jnp.zeros((G, V_EXTRA_ROWS, S), BF16).at[:, 0, :].set(1.0)], axis=1)
    r = np.arange(T)[None, :, None] // SLC_BLOCK
    onehot = jnp.asarray(np.arange(LANES)[None, None, :] == 2 * np.arange(4)[:, None, None] + r, BF16)

    def win_block(b, i):
        return jnp.maximum(b - (n_win - 1) + i, 0)

    in_specs = [
        pl.BlockSpec((Q_BLOCK, KV_WIDTH), lambda g, b: (b, g)),
        pl.BlockSpec((Q_BLOCK, LANES), lambda g, b: (b, g)),
        pl.BlockSpec((1, n_cmp, HEAD_DIM), lambda g, b: (g, 0, 0)),
        pl.BlockSpec((1, HEAD_DIM, n_cmp), lambda g, b: (g, 0, 0)),
        pl.BlockSpec((1,) + ksp.shape[1:], lambda g, b: (g, 0, 0)),
        pl.BlockSpec((1,) + vst.shape[1:], lambda g, b: (g, 0, 0, 0)),
        pl.BlockSpec(wselt.shape, lambda g, b: (0, 0)),
        pl.BlockSpec(onehot.shape, lambda g, b: (0, 0, 0)),
    ]
    in_specs += [pl.BlockSpec((Q_BLOCK, HEAD_DIM), functools.partial(lambda g, b, i: (win_block(b, i), kw_col + g), i=i))
                 for i in range(n_win)]
    in_specs += [pl.BlockSpec((1, vwt.shape[1], Q_BLOCK), functools.partial(lambda g, b, i: (g, 0, win_block(b, i)), i=i))
                 for i in range(n_win)]
    args = [proj, gates, kc, vct, ksp, vst, wselt, onehot] + [proj] * n_win + [vwt] * n_win
    return pl.pallas_call(
        functools.partial(_nsa_attn_kernel, n_sel=n_sel),
        grid=(G, nb),
        in_specs=in_specs,
        out_specs=pl.BlockSpec((Q_BLOCK, KV_WIDTH), lambda g, b: (b, g)),
        out_shape=jax.ShapeDtypeStruct((S, N_HEADS * HEAD_DIM), BF16),
        scratch_shapes=[pltpu.VMEM((nsp + 2 * SEL_PAD, KV_WIDTH), F32),
                        pltpu.VMEM((T, KV_WIDTH), F32), pltpu.VMEM((T, KV_WIDTH), F32),
                        pltpu.VMEM((T, KV_WIDTH), BF16), pltpu.VMEM((T, KV_WIDTH), BF16),
                        pltpu.VMEM((HEAD_DIM + V_EXTRA_ROWS, KV_WIDTH), F32),
                        pltpu.VMEM((HEAD_DIM, KV_WIDTH), F32), pltpu.VMEM((nsp, Q_BLOCK), F32)],
        compiler_params=_cparams(("arbitrary", "arbitrary")),
        name="nsa_attention",
    )(*args)


NSA_GROUPS_PER_STEP = 2


def _nsa_attn_multi_kernel(q_ref, gates_ref, kc_ref, vct_ref, ks_ref, vst_ref, wselt_ref, onehot_ref, *rest, n_sel):
    n_win = WINDOW // Q_BLOCK + 1
    GG = NSA_GROUPS_PER_STEP
    kw_refs = rest[:n_win]
    vwt_refs = rest[n_win:2 * n_win]
    o_ref = rest[2 * n_win]
    scratch = rest[2 * n_win + 1:]
    gall_refs, s0_refs, s1_refs, p0_refs, p1_refs, acc_refs = (scratch[i * GG:(i + 1) * GG] for i in range(6))
    groups = range(GG)

    b = pl.program_id(1)
    t0 = b * Q_BLOCK
    Q = Q_BLOCK
    H = HEADS_PER_GROUP
    C = H * Q
    q = q_ref[...]
    qts = [jnp.concatenate(
        [q[:, (g * H + h) * HEAD_DIM:(g * H + h + 1) * HEAD_DIM].astype(F32).T.astype(BF16) for h in range(H)],
        axis=1) for g in groups]

    n_cmp = kc_ref.shape[1]
    qi_c = lax.broadcasted_iota(jnp.int32, (n_cmp, Q), 1)
    cpos = lax.broadcasted_iota(jnp.int32, (n_cmp, Q), 0) * CMP_STRIDE + (CMP_BLOCK - 1)
    bias_c = _tile_heads(jnp.where(cpos <= t0 + qi_c, 0.0, NEG_INF))
    qi_row = lax.broadcasted_iota(jnp.int32, (1, C), 1) & (Q - 1)
    sees_cmp = t0 + qi_row >= CMP_BLOCK - 1
    s_cs = [_dot(kc_ref[g], qts[g]) + bias_c for g in groups]
    el_cs = [_softmax_cols(s) for s in s_cs]
    p_cs = [e * jnp.where(sees_cmp, 1.0 / l, 0.0) for e, l in el_cs]
    o_cs = [_dot(vct_ref[g], p_cs[g].astype(BF16)) for g in groups]

    wselt = wselt_ref[...]
    nsp = wselt.shape[0]
    blk = lax.broadcasted_iota(jnp.int32, (nsp, Q), 0)
    qp = t0 + lax.broadcasted_iota(jnp.int32, (nsp, Q), 1)
    cur = qp >> 6
    valid = blk * SLC_BLOCK <= qp
    forced = (blk == 0) | (blk == cur) | (blk == cur - 1)
    blk_f = blk.astype(F32)
    s_blks = []
    for p_c in p_cs:
        imp = (p_c[:, 0:Q] + p_c[:, Q:2 * Q]) + p_c[:, 2 * Q:3 * Q] + p_c[:, 3 * Q:4 * Q]
        imp_hi = imp.astype(BF16)
        imp_lo = (imp - imp_hi.astype(F32)).astype(BF16)
        s_blk = _dot(wselt, imp_hi) + _dot(wselt, imp_lo)
        s_blks.append(jnp.where(forced, FORCE, jnp.where(valid, s_blk, NEG_INF)))

    def pick(_, carry):
        out = []
        for s, sel in carry:
            m = jnp.max(s, axis=0, keepdims=True)
            first = jnp.min(jnp.where(s == m, blk_f, float(nsp)), axis=0, keepdims=True)
            hit = blk_f == first
            out.append((jnp.where(hit, REMOVED, s), jnp.where(hit, 1.0, sel)))
        return tuple(out)

    picked = lax.fori_loop(0, n_sel, pick, tuple((s, jnp.zeros((nsp, Q), F32)) for s in s_blks))

    T = SEL_KEY_TILE
    bpt = T // SLC_BLOCK
    n_tiles = (t0 + Q + T - 1) // T
    for g in groups:
        gall_refs[g][0:SEL_PAD, :] = jnp.full((SEL_PAD, C), NEG_INF, F32)
        gall_refs[g][SEL_PAD:SEL_PAD + nsp, :] = _tile_heads(jnp.where((picked[g][1] > 0.5) & valid, 0.0, NEG_INF))
        gall_refs[g][SEL_PAD + nsp:, :] = jnp.full((SEL_PAD, C), NEG_INF, F32)
    onehot = onehot_ref[(b + 1) & 3]
    zero_rows = jnp.zeros((HEAD_DIM - 2 * bpt, C), BF16)
    k_rows = ks_ref.shape[1]
    v_blocks = vst_ref.shape[1]

    def scores(g, kt):
        start = pl.multiple_of(jnp.clip(t0 - kt * T, 0, k_rows - T), Q)
        first_blk = 2 * b + 2 - bpt * (kt + 1) + SEL_PAD
        w0 = pl.multiple_of(jnp.maximum((first_blk >> 3) << 3, 0), 8)
        table = gall_refs[g][pl.ds(w0, 2 * bpt), :].astype(BF16)
        lhs = jnp.concatenate([ks_ref[g, pl.ds(start, T), :], onehot], axis=1)
        rhs = jnp.concatenate([qts[g], table, zero_rows], axis=0)
        return _dot(lhs, rhs)

    def values(g, kt):
        vb = jnp.clip(b - (T // Q) * kt, 0, v_blocks - T // Q)
        return jnp.concatenate([vst_ref[g, vb + c] for c in range(T // Q)], axis=1)

    def tile_step(g, kt, s_cur, s_nxt, p_cur, p_prv, carry):
        alpha_prev, m, l = carry
        acc_refs[g][...] = alpha_prev * acc_refs[g][...] + _dot(values(g, kt - 1), p_prv[...])
        s_nxt[...] = scores(g, kt + 1)
        s = s_cur[...]
        m_new = jnp.maximum(m, jnp.max(s, axis=0, keepdims=True))
        alpha = jnp.exp2(m - m_new)
        p = jnp.exp2(s - m_new)
        l = alpha * l + jnp.sum(p, axis=0, keepdims=True)
        p_cur[...] = p.astype(BF16)
        return alpha, m_new, l

    def pair_step(j, carry):
        carry = tuple(tile_step(g, 2 * j, s0_refs[g], s1_refs[g], p0_refs[g], p1_refs[g], carry[g]) for g in groups)
        return tuple(tile_step(g, 2 * j + 1, s1_refs[g], s0_refs[g], p1_refs[g], p0_refs[g], carry[g]) for g in groups)

    key_j = lax.broadcasted_iota(jnp.int32, (Q, Q), 0)
    qry_i = lax.broadcasted_iota(jnp.int32, (Q, Q), 1)
    tri = _tile_heads(jnp.where(key_j <= qry_i, 0.0, NEG_INF))
    for g in groups:
        s_first = scores(g, 0)
        s0_refs[g][0:T - Q, :] = s_first[0:T - Q]
        s0_refs[g][T - Q:, :] = s_first[T - Q:] + tri
        p1_refs[g][...] = jnp.zeros_like(p1_refs[g])
        acc_refs[g][...] = jnp.zeros_like(acc_refs[g])
    n_pairs = (n_tiles + 1) // 2
    init = (jnp.ones((1, C), F32), jnp.full((1, C), NEG_INF, F32), jnp.zeros((1, C), F32))
    final = lax.fori_loop(0, n_pairs, pair_step, tuple(init for _ in groups))
    o_ss = []
    for g in groups:
        alpha_last, _, l_s = final[g]
        acc_s = alpha_last * acc_refs[g][...] + _dot(values(g, 2 * n_pairs - 1), p1_refs[g][...])
        o_ss.append(acc_s * (1.0 / l_s))

    nw = WINDOW + Q
    row_w = lax.broadcasted_iota(jnp.int32, (nw, Q), 0)
    diff = lax.broadcasted_iota(jnp.int32, (nw, Q), 1) - row_w + WINDOW
    ok_w = (diff >= 0) & (diff < WINDOW) & (row_w >= WINDOW - t0)
    bias_w = _tile_heads(jnp.where(ok_w, 0.0, NEG_INF))
    kw = jnp.concatenate([r[...] for r in kw_refs], axis=0)
    o_ws = []
    for g in groups:
        vwt = jnp.concatenate([r[g] for r in vwt_refs], axis=1)
        e_w, l_w = _softmax_cols(_dot(kw[:, g * HEAD_DIM:(g + 1) * HEAD_DIM], qts[g]) + bias_w)
        o_ws.append(_dot(vwt, e_w.astype(BF16)) * (1.0 / l_w))

    for g in groups:
        gt = gates_ref[:, g * LANES:(g + 1) * LANES].T
        for h in range(H):
            cols = slice(h * Q, (h + 1) * Q)
            o = (gt[h:h + 1] * o_cs[g][:, cols] + gt[H + h:H + h + 1] * o_ss[g][:, cols]
                 + gt[2 * H + h:2 * H + h + 1] * o_ws[g][:, cols])
            o_ref[:, (g * H + h) * HEAD_DIM:(g * H + h + 1) * HEAD_DIM] = o.T.astype(BF16)


def _nsa_attention_multi(proj, gates, kc, vct):
    S = proj.shape[0]
    G = N_KV_GROUPS
    GG = NSA_GROUPS_PER_STEP
    T = SEL_KEY_TILE
    nb = S // Q_BLOCK
    n_cmp = kc.shape[1]
    ns = S // SLC_BLOCK
    nsp = -(-ns // LANES) * LANES
    n_sel = min(N_SLC, ns)
    wselt = _selection_matrix_t(n_cmp, nsp)
    n_win = WINDOW // Q_BLOCK + 1
    q_w = N_HEADS * HEAD_DIM
    kw_col = (q_w + 4 * KV_WIDTH) // (GG * HEAD_DIM)
    pad = T - Q_BLOCK
    ksp = jnp.pad(proj[:, q_w + 2 * KV_WIDTH:q_w + 3 * KV_WIDTH], ((pad, 0), (0, 0)))
    ksp = ksp.reshape(S + pad, G, HEAD_DIM).transpose(1, 0, 2)
    vsp = jnp.pad(proj[:, q_w + 3 * KV_WIDTH:q_w + 4 * KV_WIDTH], ((pad, 0), (0, 0)))
    vst = vsp.reshape((S + pad) // Q_BLOCK, Q_BLOCK, G, HEAD_DIM).transpose(2, 0, 3, 1)
    vwt = proj[:, q_w + 5 * KV_WIDTH:q_w + 6 * KV_WIDTH].reshape(S, G, HEAD_DIM).transpose(1, 2, 0)
    r = np.arange(T)[None, :, None] // SLC_BLOCK
    onehot = jnp.asarray(np.arange(LANES)[None, None, :] == 2 * np.arange(4)[:, None, None] + r, BF16)

    def win_block(b, i):
        return jnp.maximum(b - (n_win - 1) + i, 0)

    in_specs = [
        pl.BlockSpec((Q_BLOCK, GG * KV_WIDTH), lambda g, b: (b, g)),
        pl.BlockSpec((Q_BLOCK, GG * LANES), lambda g, b: (b, g)),
        pl.BlockSpec((GG, n_cmp, HEAD_DIM), lambda g, b: (g, 0, 0)),
        pl.BlockSpec((GG, HEAD_DIM, n_cmp), lambda g, b: (g, 0, 0)),
        pl.BlockSpec((GG,) + ksp.shape[1:], lambda g, b: (g, 0, 0)),
        pl.BlockSpec((GG,) + vst.shape[1:], lambda g, b: (g, 0, 0, 0)),
        pl.BlockSpec(wselt.shape, lambda g, b: (0, 0)),
        pl.BlockSpec(onehot.shape, lambda g, b: (0, 0, 0)),
    ]
    in_specs += [pl.BlockSpec((Q_BLOCK, GG * HEAD_DIM), functools.partial(lambda g, b, i: (win_block(b, i), kw_col + g), i=i))
                 for i in range(n_win)]
    in_specs += [pl.BlockSpec((GG, HEAD_DIM, Q_BLOCK), functools.partial(lambda g, b, i: (g, 0, win_block(b, i)), i=i))
                 for i in range(n_win)]
    args = [proj, gates, kc, vct, ksp, vst, wselt, onehot] + [proj] * n_win + [vwt] * n_win
    per_group = lambda shape, dtype: [pltpu.VMEM(shape, dtype) for _ in range(GG)]
    scratch = (per_group((nsp + 2 * SEL_PAD, KV_WIDTH), F32)
               + per_group((T, KV_WIDTH), F32) + per_group((T, KV_WIDTH), F32)
               + per_group((T, KV_WIDTH), BF16) + per_group((T, KV_WIDTH), BF16)
               + per_group((HEAD_DIM, KV_WIDTH), F32))
    return pl.pallas_call(
        functools.partial(_nsa_attn_multi_kernel, n_sel=n_sel),
        grid=(G // GG, nb),
        in_specs=in_specs,
        out_specs=pl.BlockSpec((Q_BLOCK, GG * KV_WIDTH), lambda g, b: (b, g)),
        out_shape=jax.ShapeDtypeStruct((S, N_HEADS * HEAD_DIM), BF16),
        scratch_shapes=scratch,
        compiler_params=_cparams(("arbitrary", "arbitrary")),
        name="nsa_attention",
    )(*args)


def _matmul_residual_kernel(a_ref, w_ref, res_ref, o_ref):
    o_ref[...] = res_ref[...] + _dot(a_ref[...], w_ref[...])


def _matmul_residual(a, w, res, tm):
    S, K = a.shape
    N = w.shape[1]
    return pl.pallas_call(
        _matmul_residual_kernel,
        grid=(S // tm,),
        in_specs=[
            pl.BlockSpec((tm, K), lambda i: (i, 0)),
            pl.BlockSpec((K, N), lambda i: (0, 0)),
            pl.BlockSpec((tm, N), lambda i: (i, 0)),
        ],
        out_specs=pl.BlockSpec((tm, N), lambda i: (i, 0)),
        out_shape=jax.ShapeDtypeStruct((S, N), F32),
        compiler_params=_cparams(("arbitrary",)),
        name="out_proj_residual",
    )(a, w, res)


def _mem_kv_kernel(mem_ref, g_ref, wk_ref, wv_ref, k_ref, v_ref):
    mb = _rms_rows(mem_ref[...], g_ref[...]).astype(BF16)
    k_ref[...] = _dot(mb, wk_ref[...]).astype(BF16)
    v_ref[...] = _dot(mb, wv_ref[...]).astype(BF16)


def _mem_kv(mem, g, wk, wv):
    M, D = mem.shape
    W = wk.shape[1]
    full = lambda shape: pl.BlockSpec(shape, lambda i: (0, 0))
    return pl.pallas_call(
        _mem_kv_kernel,
        grid=(1,),
        in_specs=[full((M, D)), full((1, D)), full((D, W)), full((D, W))],
        out_specs=[full((M, W)), full((M, W))],
        out_shape=[jax.ShapeDtypeStruct((M, W), BF16)] * 2,
        compiler_params=_cparams(("arbitrary",)),
        name="mem_kv",
    )(mem, g, wk, wv)


def _mem_attn_kernel(h_ref, g_ref, wq_ref, k_ref, v_ref, wo_ref, o_ref):
    h = h_ref[...]
    ub = _rms_rows(h, g_ref[...]).astype(BF16)
    q = (_dot(ub, wq_ref[...]) * (MEM_HEAD_DIM ** -0.5)).astype(BF16)
    outs = []
    for hd in range(MEM_HEADS):
        cols = slice(hd * MEM_HEAD_DIM, (hd + 1) * MEM_HEAD_DIM)
        s = _dot_t(q[:, cols], k_ref[:, cols])
        e = jnp.exp(s - jnp.max(s, axis=1, keepdims=True))
        p = e / jnp.sum(e, axis=1, keepdims=True)
        outs.append(_dot(p.astype(BF16), v_ref[:, cols]).astype(BF16))
    o = jnp.concatenate(outs, axis=1)
    o_ref[...] = h + _dot(o, wo_ref[...])


def _mem_attn(h, g, wq, k, v, wo, tm):
    S, D = h.shape
    W = wq.shape[1]
    M = k.shape[0]
    full = lambda shape: pl.BlockSpec(shape, lambda i: (0, 0))
    return pl.pallas_call(
        _mem_attn_kernel,
        grid=(S // tm,),
        in_specs=[pl.BlockSpec((tm, D), lambda i: (i, 0)), full((1, D)), full((D, W)),
                  full((M, W)), full((M, W)), full((W, D))],
        out_specs=pl.BlockSpec((tm, D), lambda i: (i, 0)),
        out_shape=jax.ShapeDtypeStruct((S, D), F32),
        compiler_params=_cparams(("arbitrary",)),
        name="mem_attention",
    )(h, g, wq, k, v, wo)


def _swiglu_tile(ub, wg, wu, wd):
    gate = _dot(ub, wg)
    up = _dot(ub, wu)
    act = (gate * jax.nn.sigmoid(gate)) * up
    return _dot(act.astype(BF16), wd)


def _ffn_kernel(h_ref, g_ref, wg_ref, wu_ref, wd_ref, o_ref, u_ref):
    f = pl.program_id(1)

    @pl.when(f == 0)
    def _():
        h = h_ref[...]
        u_ref[...] = _rms_rows(h, g_ref[...]).astype(BF16)
        o_ref[...] = h

    o_ref[...] += _swiglu_tile(u_ref[...], wg_ref[...], wu_ref[...], wd_ref[...])


def _ffn(h, g, wg, wu, wd, tm, tf):
    S, D = h.shape
    F = wg.shape[1]
    return pl.pallas_call(
        _ffn_kernel,
        grid=(S // tm, F // tf),
        in_specs=[
            pl.BlockSpec((tm, D), lambda i, f: (i, 0)),
            pl.BlockSpec((1, D), lambda i, f: (0, 0)),
            pl.BlockSpec((D, tf), lambda i, f: (0, f)),
            pl.BlockSpec((D, tf), lambda i, f: (0, f)),
            pl.BlockSpec((tf, D), lambda i, f: (f, 0)),
        ],
        out_specs=pl.BlockSpec((tm, D), lambda i, f: (i, 0)),
        out_shape=jax.ShapeDtypeStruct((S, D), F32),
        scratch_shapes=[pltpu.VMEM((tm, D), BF16)],
        compiler_params=_cparams(("arbitrary", "arbitrary")),
        name="dense_swiglu",
    )(h, g, wg, wu, wd)


POOL_HALO = 16


def _pool_kernel(h_ref, halo_ref, g_ref, w_ref, b_ref, scale_ref, o_ref, ext_ref):
    i = pl.program_id(0)
    tm = h_ref.shape[0]
    h = h_ref[...]
    g = g_ref[...]
    u = _rms_rows(h, g)
    halo = _rms_rows(halo_ref[...], g)
    ext_ref[0:POOL_HALO, :] = jnp.where(i > 0, halo, 0.0)
    ext_ref[POOL_HALO:, :] = u
    pos = i * tm + lax.broadcasted_iota(jnp.int32, (tm, 1), 0)
    for gi, w in enumerate(POOL_WINDOWS):
        cols = slice(gi * POOL_GROUP, (gi + 1) * POOL_GROUP)
        tot = u[:, cols]
        for k in range(1, w):
            tot = tot + ext_ref[POOL_HALO - k:POOL_HALO - k + tm, cols]
        cnt = jnp.minimum(pos + 1, w).astype(F32)
        d = tot / cnt - u[:, cols]
        z = _dot(d.astype(BF16), w_ref[gi]) + b_ref[:, cols]
        o_ref[:, cols] = h[:, cols] + z * scale_ref[:, cols]


def _pool_mixer(h, g, w, b, scale, tm):
    S, D = h.shape
    ratio = tm // POOL_HALO
    return pl.pallas_call(
        _pool_kernel,
        grid=(S // tm,),
        in_specs=[
            pl.BlockSpec((tm, D), lambda i: (i, 0)),
            pl.BlockSpec((POOL_HALO, D), lambda i: (jnp.maximum(i * ratio - 1, 0), 0)),
            pl.BlockSpec((1, D), lambda i: (0, 0)),
            pl.BlockSpec(w.shape, lambda i: (0, 0, 0)),
            pl.BlockSpec((1, D), lambda i: (0, 0)),
            pl.BlockSpec((1, D), lambda i: (0, 0)),
        ],
        out_specs=pl.BlockSpec((tm, D), lambda i: (i, 0)),
        out_shape=jax.ShapeDtypeStruct((S, D), F32),
        scratch_shapes=[pltpu.VMEM((tm + POOL_HALO, D), F32)],
        compiler_params=_cparams(("arbitrary",)),
        name="pool_mixer",
    )(h, h, g, w, b, scale)


def _router_kernel(h_ref, g_ref, rhi_ref, rlo_ref, u_ref, idx_ref, w_ref):
    u = _rms_rows(h_ref[...], g_ref[...])
    u_ref[...] = u
    u_hi = u.astype(BF16)
    u_lo = (u - u_hi.astype(F32)).astype(BF16)
    rhi = rhi_ref[...]
    logits = (_dot(u_hi, rhi) + _dot(u_lo, rhi)) + _dot(u_hi, rlo_ref[...])
    lane = lax.broadcasted_iota(jnp.int32, logits.shape, 1)
    lane_f = lane.astype(F32)
    s = jnp.where(lane < N_EXPERTS, logits, REMOVED)
    m1 = jnp.max(s, axis=1, keepdims=True)
    i1 = jnp.min(jnp.where(s == m1, lane_f, float(LANES)), axis=1, keepdims=True)
    s = jnp.where(lane_f == i1, REMOVED, s)
    m2 = jnp.max(s, axis=1, keepdims=True)
    i2 = jnp.min(jnp.where(s == m2, lane_f, float(LANES)), axis=1, keepdims=True)
    e2 = jnp.exp(m2 - m1)
    den = 1.0 + e2
    idx_ref[...] = jnp.where(lane == 0, i1, jnp.where(lane == 1, i2, 0.0)).astype(jnp.int32)
    w_ref[...] = jnp.where(lane == 0, 1.0 / den, jnp.where(lane == 1, e2 / den, 0.0))


def _router(h, g, r_hi, r_lo, tm):
    S, D = h.shape
    return pl.pallas_call(
        _router_kernel,
        grid=(S // tm,),
        in_specs=[
            pl.BlockSpec((tm, D), lambda i: (i, 0)),
            pl.BlockSpec((1, D), lambda i: (0, 0)),
            pl.BlockSpec((D, LANES), lambda i: (0, 0)),
            pl.BlockSpec((D, LANES), lambda i: (0, 0)),
        ],
        out_specs=[
            pl.BlockSpec((tm, D), lambda i: (i, 0)),
            pl.BlockSpec((tm, LANES), lambda i: (i, 0)),
            pl.BlockSpec((tm, LANES), lambda i: (i, 0)),
        ],
        out_shape=[
            jax.ShapeDtypeStruct((S, D), F32),
            jax.ShapeDtypeStruct((S, LANES), jnp.int32),
            jax.ShapeDtypeStruct((S, LANES), F32),
        ],
        compiler_params=_cparams(("arbitrary",)),
        name="moe_router",
    )(h, g, r_hi, r_lo)


GATHER_UNROLL = 8


def _start_row_gather(src_hbm, dst_vmem, sem, n_rows, src_row_of):
    def body(r, c):
        pltpu.make_async_copy(src_hbm.at[pl.ds(src_row_of(r), 1), :], dst_vmem.at[pl.ds(r, 1), :], sem).start()
        return c

    lax.fori_loop(0, n_rows, body, 0, unroll=GATHER_UNROLL)


def _wait_row_gather(src_hbm, dst_vmem, sem):
    pltpu.make_async_copy(src_hbm.at[pl.ds(0, dst_vmem.shape[0]), :], dst_vmem, sem).wait()


def _moe_expert_kernel(tok_ref, exp_ref, nused_ref, u_hbm, wg_ref, wu_ref, wd_ref, o_ref, x_ref, xb_ref, sem):
    i = pl.program_id(0)
    f = pl.program_id(1)
    rows = x_ref.shape[1]
    n_used = nused_ref[0]
    active = i < n_used
    slot = i & 1

    def gather(block, into):
        _start_row_gather(u_hbm, x_ref.at[into], sem.at[into], rows, lambda r: tok_ref[block * rows + r])

    @pl.when((i == 0) & (f == 0) & active)
    def _():
        gather(0, 0)

    @pl.when(active & (f == 0))
    def _():
        _wait_row_gather(u_hbm, x_ref.at[slot], sem.at[slot])
        xb_ref[...] = x_ref[slot].astype(BF16)
        o_ref[...] = jnp.zeros_like(o_ref)

        @pl.when(i + 1 < n_used)
        def _():
            gather(i + 1, 1 - slot)

    @pl.when(active)
    def _():
        o_ref[...] += _swiglu_tile(xb_ref[...], wg_ref[0], wu_ref[0], wd_ref[0])

    @pl.when(jnp.logical_not(active) & (f == 0))
    def _():
        o_ref[...] = jnp.zeros_like(o_ref)


def _moe_experts(tok, blk_exp, n_used, u, wg, wu, wd, tf):
    N, D = u.shape
    n_blk = blk_exp.shape[0]
    F = wg.shape[2]
    rows = MOE_BLOCK
    n_f = F // tf

    def ftile(i, f, nu):
        return jnp.where(i < nu[0], f, n_f - 1)

    grid_spec = pltpu.PrefetchScalarGridSpec(
        num_scalar_prefetch=3,
        grid=(n_blk, n_f),
        in_specs=[
            pl.BlockSpec(memory_space=pl.ANY),
            pl.BlockSpec((1, D, tf), lambda i, f, tok, ex, nu: (ex[i], 0, ftile(i, f, nu))),
            pl.BlockSpec((1, D, tf), lambda i, f, tok, ex, nu: (ex[i], 0, ftile(i, f, nu))),
            pl.BlockSpec((1, tf, D), lambda i, f, tok, ex, nu: (ex[i], ftile(i, f, nu), 0)),
        ],
        out_specs=pl.BlockSpec((rows, D), lambda i, f, tok, ex, nu: (i, 0)),
        scratch_shapes=[pltpu.VMEM((2, rows, D), F32), pltpu.VMEM((rows, D), BF16), pltpu.SemaphoreType.DMA((2,))],
    )
    return pl.pallas_call(
        _moe_expert_kernel,
        grid_spec=grid_spec,
        out_shape=jax.ShapeDtypeStruct((n_blk * rows, D), F32),
        compiler_params=_cparams(("arbitrary", "arbitrary")),
        name="moe_experts",
    )(tok, blk_exp, n_used, u, wg, wu, wd)


def _moe_combine_kernel(pos_ref, h_ref, w_ref, gfin_ref, y_hbm, o_ref, r_ref, sem):
    i = pl.program_id(0)
    n_steps = pl.num_programs(0)
    tm = h_ref.shape[0]
    slot = i & 1

    def gather(step, into):
        for k in range(TOP_K):
            _start_row_gather(y_hbm, r_ref.at[into, k], sem.at[into, k], tm,
                              lambda r: pos_ref[(step * tm + r) * TOP_K + k])

    @pl.when(i == 0)
    def _():
        gather(0, 0)

    for k in range(TOP_K):
        _wait_row_gather(y_hbm, r_ref.at[slot, k], sem.at[slot, k])

    @pl.when(i + 1 < n_steps)
    def _():
        gather(i + 1, 1 - slot)

    w = w_ref[...]
    y = h_ref[...] + (r_ref[slot, 0] * w[:, 0:1] + r_ref[slot, 1] * w[:, 1:2])
    o_ref[...] = _rms_rows(y, gfin_ref[...])


def _moe_combine(pos, h, w, g_final, y, tm):
    S, D = h.shape
    grid_spec = pltpu.PrefetchScalarGridSpec(
        num_scalar_prefetch=1,
        grid=(S // tm,),
        in_specs=[
            pl.BlockSpec((tm, D), lambda i, pos: (i, 0)),
            pl.BlockSpec((tm, LANES), lambda i, pos: (i, 0)),
            pl.BlockSpec((1, D), lambda i, pos: (0, 0)),
            pl.BlockSpec(memory_space=pl.ANY),
        ],
        out_specs=pl.BlockSpec((tm, D), lambda i, pos: (i, 0)),
        scratch_shapes=[pltpu.VMEM((2, TOP_K, tm, D), F32), pltpu.SemaphoreType.DMA((2, TOP_K))],
    )
    return pl.pallas_call(
        _moe_combine_kernel,
        grid_spec=grid_spec,
        out_shape=jax.ShapeDtypeStruct((S, D), F32),
        compiler_params=_cparams(("arbitrary",)),
        name="moe_combine_norm",
    )(pos, h, w, g_final, y)


def _slot_scatter_kernel(dest_ref, tok_ref):
    def zero(i, c):
        tok_ref[i] = 0
        return c

    lax.fori_loop(0, tok_ref.shape[0], zero, 0, unroll=GATHER_UNROLL)

    shift = TOP_K.bit_length() - 1
    assert 1 << shift == TOP_K

    def put(s, c):
        tok_ref[dest_ref[s]] = lax.shift_right_logical(s, shift)
        return c

    lax.fori_loop(0, dest_ref.shape[0], put, 0, unroll=GATHER_UNROLL)


def _slot_scatter(dest, cap):
    return pl.pallas_call(
        _slot_scatter_kernel,
        in_specs=[pl.BlockSpec(memory_space=pltpu.SMEM)],
        out_specs=pl.BlockSpec(memory_space=pltpu.SMEM),
        out_shape=jax.ShapeDtypeStruct((cap,), jnp.int32),
        name="moe_slot_scatter",
    )(dest)


def _moe_dispatch_indices(top_idx):
    N = top_idx.shape[0]
    e_flat = top_idx.reshape(-1)
    onehot = (e_flat[:, None] == jnp.arange(N_EXPERTS, dtype=jnp.int32)[None, :]).astype(jnp.int32)
    csum = jnp.cumsum(onehot, axis=0)
    counts = csum[-1]
    rank = jnp.take_along_axis(csum, e_flat[:, None], axis=1)[:, 0] - 1
    padded = (counts + MOE_BLOCK - 1) // MOE_BLOCK * MOE_BLOCK
    pend = jnp.cumsum(padded)
    poff = pend - padded
    dest = poff[e_flat] + rank
    n_blk = (N * TOP_K + MOE_BLOCK - 1) // MOE_BLOCK + N_EXPERTS
    cap = n_blk * MOE_BLOCK
    tok_buf = _slot_scatter(dest.astype(jnp.int32), cap)
    blk_start = jnp.arange(n_blk, dtype=jnp.int32) * MOE_BLOCK
    blk_exp = jnp.minimum(jnp.sum(pend[None, :] <= blk_start[:, None], axis=1), N_EXPERTS - 1).astype(jnp.int32)
    n_used = (pend[-1] // MOE_BLOCK).astype(jnp.int32).reshape(1)
    return tok_buf, blk_exp, n_used, dest.astype(jnp.int32)


def _rope_tables(pos):
    half = HEAD_DIM // 2
    inv = ROPE_THETA ** (-np.arange(half, dtype=np.float64) / half)
    ang = np.asarray(pos, np.float64)[:, None] * inv[None, :]
    cos, sin = np.cos(ang), np.sin(ang)
    return (jnp.asarray(np.concatenate([cos, cos], axis=1), F32),
            jnp.asarray(np.concatenate([-sin, sin], axis=1), F32))


def _gate_params(w_gates, gate_b):
    D = w_gates.shape[0]
    w = w_gates.reshape(D, 3, N_KV_GROUPS, HEADS_PER_GROUP).transpose(0, 2, 1, 3).reshape(D, N_KV_GROUPS, 3 * HEADS_PER_GROUP)
    w = jnp.pad(w, ((0, 0), (0, 0), (0, LANES - 3 * HEADS_PER_GROUP))).reshape(D, N_KV_GROUPS * LANES)
    b = gate_b.reshape(3, N_KV_GROUPS, HEADS_PER_GROUP).transpose(1, 0, 2).reshape(N_KV_GROUPS, 3 * HEADS_PER_GROUP)
    b = jnp.pad(b, ((0, 0), (0, LANES - 3 * HEADS_PER_GROUP))).reshape(1, N_KV_GROUPS * LANES)
    return w.astype(BF16), b


def _chunked_groups(t):
    S = t.shape[0]
    return (t.reshape(S // CMP_STRIDE, CMP_STRIDE, N_KV_GROUPS, HEAD_DIM)
            .transpose(2, 0, 1, 3).reshape(N_KV_GROUPS, S // CMP_STRIDE, CMP_STRIDE * HEAD_DIM))


def _row_tile(S, want):
    t = min(S, want)
    assert S % t == 0
    return t


def kernel(x, mem, norm_mix, norm_mem_q, norm_mem_kv, norm_ffn, norm_final, nsa_w_in, nsa_gate_b, nsa_pe_k, nsa_pe_v, nsa_cmp_k_w1, nsa_cmp_k_w2, nsa_cmp_v_w1, nsa_cmp_v_w2, nsa_w_out, pool_w, pool_b, pool_scale, mem_wq, mem_wk, mem_wv, mem_wo, ffn_w_gate, ffn_w_up, ffn_w_down, moe_router, moe_w_gate, moe_w_up, moe_w_down):
    B, S, D = x.shape
    assert B == 1 and S % WINDOW == 0 and S >= SLC_BLOCK * N_SLC
    h = x.reshape(S, D)
    memf = mem.reshape(mem.shape[1], D)
    row = lambda v: v.reshape(1, -1)

    n_main = N_HEADS * HEAD_DIM + 6 * KV_WIDTH
    w_in = nsa_w_in[0]
    w_gate, b_gate = _gate_params(w_in[:, n_main:], nsa_gate_b[0])
    cos, sin = _rope_tables(np.arange(S))
    proj, gates = _nsa_proj(h, row(norm_mix[0]), w_in[:, :n_main].astype(BF16), w_gate, b_gate, cos, sin,
                            _row_tile(S, 1024))
    q_w = N_HEADS * HEAD_DIM
    n_chunks = S // CMP_STRIDE
    cmp_pos = np.arange(n_chunks) * CMP_STRIDE + CMP_BLOCK - 1
    ccos, csin = _rope_tables(cmp_pos)
    kc = _compress(_chunked_groups(proj[:, q_w:q_w + KV_WIDTH]), nsa_pe_k[0].reshape(1, -1),
                   nsa_cmp_k_w1[0].astype(BF16), nsa_cmp_k_w2[0].astype(BF16), ccos, csin, True)
    vct = _compress(_chunked_groups(proj[:, q_w + KV_WIDTH:q_w + 2 * KV_WIDTH]), nsa_pe_v[0].reshape(1, -1),
                    nsa_cmp_v_w1[0].astype(BF16), nsa_cmp_v_w2[0].astype(BF16), ccos, csin, False).transpose(0, 2, 1)
    attn = _nsa_attention(proj, gates, kc, vct)
    h = _matmul_residual(attn, nsa_w_out[0].astype(BF16), h, _row_tile(S, 512))

    def mem_layer(h, i):
        k, v = _mem_kv(memf, row(norm_mem_kv[i]), mem_wk[i].astype(BF16), mem_wv[i].astype(BF16))
        return _mem_attn(h, row(norm_mem_q[i]), mem_wq[i].astype(BF16), k, v, mem_wo[i].astype(BF16),
                         _row_tile(S, 512))

    h = mem_layer(h, 0)
    h = _ffn(h, row(norm_ffn[0]), ffn_w_gate[0].astype(BF16), ffn_w_up[0].astype(BF16),
             ffn_w_down[0].astype(BF16), _row_tile(S, 1024), 512)

    h = _pool_mixer(h, row(norm_mix[1]), pool_w[0].astype(BF16), row(pool_b[0]), row(pool_scale[0]),
                    _row_tile(S, 512))
    h = mem_layer(h, 1)

    r = jnp.pad(moe_router[0], ((0, 0), (0, LANES - N_EXPERTS)))
    r_hi = r.astype(BF16)
    r_lo = (r - r_hi.astype(F32)).astype(BF16)
    u, idx, gate_w = _router(h, row(norm_ffn[1]), r_hi, r_lo, _row_tile(S, 512))
    tok_buf, blk_exp, n_used, dest = _moe_dispatch_indices(idx[:, :TOP_K])
    y = _moe_experts(tok_buf, blk_exp, n_used, u, moe_w_gate[0].astype(BF16), moe_w_up[0].astype(BF16),
                     moe_w_down[0].astype(BF16), 512)
    out = _moe_combine(dest, h, gate_w, row(norm_final), y, _row_tile(S, 256))
    return out.reshape(B, S, D)
```

```python
import functools

import numpy as np
import jax
import jax.numpy as jnp
from jax import lax
from jax.experimental import pallas as pl
from jax.experimental.pallas import tpu as pltpu

F32 = jnp.float32
BF16 = jnp.bfloat16

N_HEADS = 16
HEAD_DIM = 128
N_KV_GROUPS = 4
HEADS_PER_GROUP = 4
KV_WIDTH = N_KV_GROUPS * HEAD_DIM
CMP_BLOCK = 32
CMP_STRIDE = 16
SLC_BLOCK = 64
RATIO = SLC_BLOCK // CMP_STRIDE
N_SLC = 16
WINDOW = 512
Q_BLOCK = 128
ROPE_THETA = 10000.0
POOL_WINDOWS = (2, 4, 8, 16)
POOL_GROUP = 512
MEM_HEADS = 4
MEM_HEAD_DIM = 128
N_EXPERTS = 8
TOP_K = 2
MOE_BLOCK = 512
EPS = 1e-6
NEG_INF = -1e30
FORCE = 1e9
REMOVED = -3e38
LOG2E = float(np.log2(np.e))

LANES = 128
VMEM_LIMIT = 56 * 1024 * 1024

SEL_KEY_TILE = 512


def _cparams(sem):
    return pltpu.CompilerParams(dimension_semantics=sem, vmem_limit_bytes=VMEM_LIMIT)


def _rms_rows(xf, g):
    r = lax.rsqrt(jnp.mean(xf * xf, axis=-1, keepdims=True) + EPS)
    return (xf * r) * g


def _dot(a, b):
    return jnp.dot(a, b, preferred_element_type=F32)


def _dot_t(a, b):
    return lax.dot_general(a, b, (((1,), (1,)), ((), ())), preferred_element_type=F32)


def _proj_kernel(x_ref, g_ref, w_ref, wgate_ref, bgate_ref, cos_ref, sin_ref, o_ref, gates_ref, u_ref):
    j = pl.program_id(1)

    @pl.when(j == 0)
    def _():
        ub = _rms_rows(x_ref[...], g_ref[...]).astype(BF16)
        u_ref[...] = ub
        gates_ref[...] = jax.nn.sigmoid(_dot(ub, wgate_ref[...]) + bgate_ref[...])

    acc = _dot(u_ref[...], w_ref[...])
    n_q_tiles = N_HEADS * HEAD_DIM // KV_WIDTH
    is_q = j < n_q_tiles
    is_rope = is_q | (j == n_q_tiles + 2) | (j == n_q_tiles + 4)

    @pl.when(is_rope)
    def _():
        scale = jnp.where(is_q, HEAD_DIM ** -0.5 * LOG2E, 1.0).astype(F32)
        cos = cos_ref[...]
        sin = sin_ref[...]
        for h in range(KV_WIDTH // HEAD_DIM):
            xh = acc[:, h * HEAD_DIM:(h + 1) * HEAD_DIM]
            rot = pltpu.roll(xh, HEAD_DIM // 2, axis=1)
            o_ref[:, h * HEAD_DIM:(h + 1) * HEAD_DIM] = ((xh * cos + rot * sin) * scale).astype(BF16)

    @pl.when(jnp.logical_not(is_rope))
    def _():
        o_ref[...] = acc.astype(BF16)


def _nsa_proj(x, g, w_main, w_gate, b_gate, cos, sin, tm):
    S, D = x.shape
    n_main = w_main.shape[1]
    tn = KV_WIDTH
    n_gate = w_gate.shape[1]
    return pl.pallas_call(
        _proj_kernel,
        grid=(S // tm, n_main // tn),
        in_specs=[
            pl.BlockSpec((tm, D), lambda i, j: (i, 0)),
            pl.BlockSpec((1, D), lambda i, j: (0, 0)),
            pl.BlockSpec((D, tn), lambda i, j: (0, j)),
            pl.BlockSpec((D, n_gate), lambda i, j: (0, 0)),
            pl.BlockSpec((1, n_gate), lambda i, j: (0, 0)),
            pl.BlockSpec((tm, HEAD_DIM), lambda i, j: (i, 0)),
            pl.BlockSpec((tm, HEAD_DIM), lambda i, j: (i, 0)),
        ],
        out_specs=[
            pl.BlockSpec((tm, tn), lambda i, j: (i, j)),
            pl.BlockSpec((tm, n_gate), lambda i, j: (i, 0)),
        ],
        out_shape=[
            jax.ShapeDtypeStruct((S, n_main), BF16),
            jax.ShapeDtypeStruct((S, n_gate), F32),
        ],
        scratch_shapes=[pltpu.VMEM((tm, D), BF16)],
        compiler_params=_cparams(("arbitrary", "arbitrary")),
        name="nsa_proj",
    )(x, g, w_main, w_gate, b_gate, cos, sin)


def _gelu_tanh(x):
    return 0.5 * x * (1.0 + jnp.tanh(np.sqrt(2.0 / np.pi).astype(np.float32) * (x + 0.044715 * (x * x * x))))


def _compress_kernel(t_ref, pe_ref, w1_ref, w2_ref, cos_ref, sin_ref, o_ref, *, apply_rope):
    half = CMP_STRIDE * HEAD_DIM
    t = t_ref[0].astype(F32)
    n = t.shape[0]
    a = _dot((t + pe_ref[:, :half]).astype(BF16), w1_ref[:half, :])
    b = _dot((t + pe_ref[:, half:]).astype(BF16), w1_ref[half:, :])
    hid = a + pltpu.roll(b, n - 1, axis=0)
    out = _dot(_gelu_tanh(hid).astype(BF16), w2_ref[...])
    if apply_rope:
        rot = pltpu.roll(out, HEAD_DIM // 2, axis=1)
        out = out * cos_ref[...] + rot * sin_ref[...]
    o_ref[0] = out.astype(BF16)


def _compress(t, pe_flat, w1, w2, cos, sin, apply_rope):
    G, n, width = t.shape
    hidden = w1.shape[1]
    out_block = (1, n, HEAD_DIM)
    return pl.pallas_call(
        functools.partial(_compress_kernel, apply_rope=apply_rope),
        grid=(G,),
        in_specs=[
            pl.BlockSpec((1, n, width), lambda g: (g, 0, 0)),
            pl.BlockSpec((1, 2 * width), lambda g: (0, 0)),
            pl.BlockSpec((2 * width, hidden), lambda g: (0, 0)),
            pl.BlockSpec((hidden, HEAD_DIM), lambda g: (0, 0)),
            pl.BlockSpec((n, HEAD_DIM), lambda g: (0, 0)),
            pl.BlockSpec((n, HEAD_DIM), lambda g: (0, 0)),
        ],
        out_specs=pl.BlockSpec(out_block, lambda g: (g, 0, 0)),
        out_shape=jax.ShapeDtypeStruct((G,) + out_block[1:], BF16),
        compiler_params=_cparams(("arbitrary",)),
        name="nsa_compress_k" if apply_rope else "nsa_compress_v",
    )(t, pe_flat, w1, w2, cos, sin)


def _softmax_cols(s):
    e = jnp.exp2(s - jnp.max(s, axis=0, keepdims=True))
    return e, jnp.sum(e, axis=0, keepdims=True)


def _tile_heads(x):
    return jnp.concatenate([x] * HEADS_PER_GROUP, axis=1)


SEL_PAD = 16
V_EXTRA_ROWS = 16


def _run_on_shortest_prefix(fn, total, needed, parts):
    piece = total // parts
    sizes = [piece * k for k in range(1, parts + 1)] if piece % LANES == 0 else [total]
    for idx, n in enumerate(sizes):
        cond = needed > (sizes[idx - 1] if idx else 0)
        if idx < len(sizes) - 1:
            cond = cond & (needed <= n)
        pl.when(cond)(functools.partial(fn, n))


def _nsa_attn_kernel(q_ref, gates_ref, kc_ref, vct_ref, ks_ref, vst_ref, wselt_ref, onehot_ref, *rest, n_sel):
    n_win = WINDOW // Q_BLOCK + 1
    kw_refs = rest[:n_win]
    vwt_refs = rest[n_win:2 * n_win]
    o_ref = rest[2 * n_win]
    gall_ref, s0_ref, s1_ref, p0_ref, p1_ref, acc_ref, oc_ref, sblk_ref = rest[2 * n_win + 1:]

    b = pl.program_id(1)
    t0 = b * Q_BLOCK
    Q = Q_BLOCK
    H = HEADS_PER_GROUP
    C = H * Q
    q = q_ref[...]
    qt = jnp.concatenate(
        [q[:, h * HEAD_DIM:(h + 1) * HEAD_DIM].astype(F32).T.astype(BF16) for h in range(H)], axis=1)

    n_cmp = kc_ref.shape[1]
    nsp = wselt_ref.shape[0]
    qi_row = lax.broadcasted_iota(jnp.int32, (1, C), 1) & (Q - 1)

    def compressed(n):
        qi_c = lax.broadcasted_iota(jnp.int32, (n, Q), 1)
        cpos = lax.broadcasted_iota(jnp.int32, (n, Q), 0) * CMP_STRIDE + (CMP_BLOCK - 1)
        bias_c = jnp.where(cpos <= t0 + qi_c, 0.0, NEG_INF)
        e_c, l_c = _softmax_cols(_dot(kc_ref[0, 0:n, :], qt) + _tile_heads(bias_c))
        p_c = e_c * jnp.where(t0 + qi_row >= CMP_BLOCK - 1, 1.0 / l_c, 0.0)
        oc_ref[...] = _dot(vct_ref[0, :, 0:n], p_c.astype(BF16))
        imp = (p_c[:, 0:Q] + p_c[:, Q:2 * Q]) + p_c[:, 2 * Q:3 * Q] + p_c[:, 3 * Q:4 * Q]
        imp_hi = imp.astype(BF16)
        imp_lo = (imp - imp_hi.astype(F32)).astype(BF16)
        wselt = wselt_ref[:, 0:n]
        sblk_ref[...] = _dot(wselt, imp_hi) + _dot(wselt, imp_lo)

    _run_on_shortest_prefix(compressed, n_cmp, (t0 + Q - CMP_BLOCK) // CMP_STRIDE + 1, 4)
    o_c = oc_ref[...]

    def select(rows):
        blk = lax.broadcasted_iota(jnp.int32, (rows, Q), 0)
        qp = t0 + lax.broadcasted_iota(jnp.int32, (rows, Q), 1)
        cur = qp >> 6
        valid = blk * SLC_BLOCK <= qp
        forced = (blk == 0) | (blk == cur) | (blk == cur - 1)
        s_blk = jnp.where(forced, FORCE, jnp.where(valid, sblk_ref[0:rows, :], NEG_INF))
        blk_f = blk.astype(F32)

        def pick(_, carry):
            s, sel = carry
            m = jnp.max(s, axis=0, keepdims=True)
            first = jnp.min(jnp.where(s == m, blk_f, float(rows)), axis=0, keepdims=True)
            hit = blk_f == first
            return jnp.where(hit, REMOVED, s), jnp.where(hit, 1.0, sel)

        _, sel = lax.fori_loop(0, n_sel, pick, (s_blk, jnp.zeros((rows, Q), F32)))
        gall_ref[SEL_PAD:SEL_PAD + rows, :] = _tile_heads(jnp.where((sel > 0.5) & valid, 0.0, NEG_INF))
        if rows < nsp:
            gall_ref[SEL_PAD + rows:SEL_PAD + nsp, :] = jnp.full((nsp - rows, C), NEG_INF, F32)

    _run_on_shortest_prefix(select, nsp, (t0 + Q - 1) // SLC_BLOCK + 1, 2)

    T = SEL_KEY_TILE
    bpt = T // SLC_BLOCK
    n_tiles = (t0 + Q + T - 1) // T
    gall_ref[0:SEL_PAD, :] = jnp.full((SEL_PAD, C), NEG_INF, F32)
    gall_ref[SEL_PAD + nsp:, :] = jnp.full((SEL_PAD, C), NEG_INF, F32)
    onehot = onehot_ref[(b + 1) & 3]
    zero_rows = jnp.zeros((HEAD_DIM - 2 * bpt, C), BF16)
    k_rows = ks_ref.shape[1]
    v_blocks = vst_ref.shape[1]

    def scores(kt):
        start = pl.multiple_of(jnp.clip(t0 - kt * T, 0, k_rows - T), Q)
        first_blk = 2 * b + 2 - bpt * (kt + 1) + SEL_PAD
        w0 = pl.multiple_of(jnp.maximum((first_blk >> 3) << 3, 0), 8)
        table = gall_ref[pl.ds(w0, 2 * bpt), :].astype(BF16)
        lhs = jnp.concatenate([ks_ref[0, pl.ds(start, T), :], onehot], axis=1)
        rhs = jnp.concatenate([qt, table, zero_rows], axis=0)
        return _dot(lhs, rhs)

    def values(kt):
        vb = jnp.clip(b - (T // Q) * kt, 0, v_blocks - T // Q)
        return jnp.concatenate([vst_ref[0, vb + c] for c in range(T // Q)], axis=1)

    def tile_step(kt, s_cur, s_nxt, p_cur, p_prv, carry):
        alpha_prev, m = carry
        acc_ref[...] = alpha_prev * acc_ref[...] + _dot(values(kt - 1), p_prv[...])
        s_nxt[...] = scores(kt + 1)
        s = s_cur[...]
        m_new = jnp.maximum(m, jnp.max(s, axis=0, keepdims=True))
        p_cur[...] = jnp.exp2((s - m_new).astype(BF16))
        return jnp.exp2(m - m_new), m_new

    def pair_step(j, carry):
        carry = tile_step(2 * j, s0_ref, s1_ref, p0_ref, p1_ref, carry)
        return tile_step(2 * j + 1, s1_ref, s0_ref, p1_ref, p0_ref, carry)

    s_first = scores(0)
    key_j = lax.broadcasted_iota(jnp.int32, (Q, Q), 0)
    qry_i = lax.broadcasted_iota(jnp.int32, (Q, Q), 1)
    s0_ref[0:T - Q, :] = s_first[0:T - Q]
    s0_ref[T - Q:, :] = s_first[T - Q:] + _tile_heads(jnp.where(key_j <= qry_i, 0.0, NEG_INF))
    p1_ref[...] = jnp.zeros_like(p1_ref)
    acc_ref[...] = jnp.zeros_like(acc_ref)
    n_pairs = (n_tiles + 1) // 2
    alpha_last, _ = lax.fori_loop(
        0, n_pairs, pair_step, (jnp.ones((1, C), F32), jnp.full((1, C), NEG_INF, F32)))
    acc_s = alpha_last * acc_ref[...] + _dot(values(2 * n_pairs - 1), p1_ref[...])
    o_s = acc_s[0:HEAD_DIM] * (1.0 / acc_s[HEAD_DIM:HEAD_DIM + 1])

    kw = jnp.concatenate([r[...] for r in kw_refs], axis=0)
    vwt = jnp.concatenate([r[0] for r in vwt_refs], axis=1)
    nw = WINDOW + Q
    row_w = lax.broadcasted_iota(jnp.int32, (nw, Q), 0)
    diff = lax.broadcasted_iota(jnp.int32, (nw, Q), 1) - row_w + WINDOW
    ok_w = (diff >= 0) & (diff < WINDOW) & (row_w >= WINDOW - t0)
    s_w = _dot(kw, qt) + _tile_heads(jnp.where(ok_w, 0.0, NEG_INF))
    e_w = jnp.exp2((s_w - jnp.max(s_w, axis=0, keepdims=True)).astype(BF16))
    acc_w = _dot(vwt, e_w)
    o_w = acc_w[0:HEAD_DIM] * (1.0 / acc_w[HEAD_DIM:HEAD_DIM + 1])

    gt = gates_ref[...].T
    for h in range(H):
        cols = slice(h * Q, (h + 1) * Q)
        o = (gt[h:h + 1] * o_c[:, cols] + gt[H + h:H + h + 1] * o_s[:, cols]
             + gt[2 * H + h:2 * H + h + 1] * o_w[:, cols])
        o_ref[:, h * HEAD_DIM:(h + 1) * HEAD_DIM] = o.T.astype(BF16)


def _selection_matrix_t(n_cmp, nsp):
    j = np.arange(nsp)[:, None]
    c = np.arange(n_cmp)[None, :]
    d = c - RATIO * j
    w = np.where((d == -1) | (d == RATIO - 1), 1.0, np.where((d >= 0) & (d < RATIO - 1), 2.0, 0.0))
    return jnp.asarray(w, BF16)


def _nsa_attention(proj, gates, kc, vct):
    S = proj.shape[0]
    G = N_KV_GROUPS
    T = SEL_KEY_TILE
    nb = S // Q_BLOCK
    n_cmp = kc.shape[1]
    ns = S // SLC_BLOCK
    nsp = -(-ns // LANES) * LANES
    n_sel = min(N_SLC, ns)
    wselt = _selection_matrix_t(n_cmp, nsp)
    n_win = WINDOW // Q_BLOCK + 1
    cpb = KV_WIDTH // HEAD_DIM
    q_w = N_HEADS * HEAD_DIM
    q_cols = q_w // HEAD_DIM
    kw_col = q_cols + 4 * cpb
    pad = T - Q_BLOCK
    ksp = jnp.pad(proj[:, q_w + 2 * KV_WIDTH:q_w + 3 * KV_WIDTH], ((pad, 0), (0, 0)))
    ksp = ksp.reshape(S + pad, G, HEAD_DIM).transpose(1, 0, 2)
    vsp = jnp.pad(proj[:, q_w + 3 * KV_WIDTH:q_w + 4 * KV_WIDTH], ((pad, 0), (0, 0)))
    vst = vsp.reshape((S + pad) // Q_BLOCK, Q_BLOCK, G, HEAD_DIM).transpose(2, 0, 3, 1)
    extra = jnp.zeros(vst.shape[:2] + (V_EXTRA_ROWS, Q_BLOCK), BF16).at[:, :, 0, :].set(1.0)
    vst = jnp.concatenate([vst, extra], axis=2)
    vwt = proj[:, q_w + 5 * KV_WIDTH:q_w + 6 * KV_WIDTH].reshape(S, G, HEAD_DIM).transpose(1, 2, 0)
    vwt = jnp.concatenate([vwt, jnp.zeros((G, V_EXTRA_ROWS, S), BF16).at[:, 0, :].set(1.0)], axis=1)
    r = np.arange(T)[None, :, None] // SLC_BLOCK
    onehot = jnp.asarray(np.arange(LANES)[None, None, :] == 2 * np.arange(4)[:, None, None] + r, BF16)

    def win_block(b, i):
        return jnp.maximum(b - (n_win - 1) + i, 0)

    in_specs = [
        pl.BlockSpec((Q_BLOCK, KV_WIDTH), lambda g, b: (b, g)),
        pl.BlockSpec((Q_BLOCK, LANES), lambda g, b: (b, g)),
        pl.BlockSpec((1, n_cmp, HEAD_DIM), lambda g, b: (g, 0, 0)),
        pl.BlockSpec((1, HEAD_DIM, n_cmp), lambda g, b: (g, 0, 0)),
        pl.BlockSpec((1,) + ksp.shape[1:], lambda g, b: (g, 0, 0)),
        pl.BlockSpec((1,) + vst.shape[1:], lambda g, b: (g, 0, 0, 0)),
        pl.BlockSpec(wselt.shape, lambda g, b: (0, 0)),
        pl.BlockSpec(onehot.shape, lambda g, b: (0, 0, 0)),
    ]
    in_specs += [pl.BlockSpec((Q_BLOCK, HEAD_DIM), functools.partial(lambda g, b, i: (win_block(b, i), kw_col + g), i=i))
                 for i in range(n_win)]
    in_specs += [pl.BlockSpec((1, vwt.shape[1], Q_BLOCK), functools.partial(lambda g, b, i: (g, 0, win_block(b, i)), i=i))
                 for i in range(n_win)]
    args = [proj, gates, kc, vct, ksp, vst, wselt, onehot] + [proj] * n_win + [vwt] * n_win
    return pl.pallas_call(
        functools.partial(_nsa_attn_kernel, n_sel=n_sel),
        grid=(G, nb),
        in_specs=in_specs,
        out_specs=pl.BlockSpec((Q_BLOCK, KV_WIDTH), lambda g, b: (b, g)),
        out_shape=jax.ShapeDtypeStruct((S, N_HEADS * HEAD_DIM), BF16),
        scratch_shapes=[pltpu.VMEM((nsp + 2 * SEL_PAD, KV_WIDTH), F32),
                        pltpu.VMEM((T, KV_WIDTH), F32), pltpu.VMEM((T, KV_WIDTH), F32),
                        pltpu.VMEM((T, KV_WIDTH), BF16), pltpu.VMEM((T, KV_WIDTH), BF16),
                        pltpu.VMEM((HEAD_DIM + V_EXTRA_ROWS, KV_WIDTH), F32),
                        pltpu.VMEM((HEAD_DIM, KV_WIDTH), F32), pltpu.VMEM((nsp, Q_BLOCK), F32)],
        compiler_params=_cparams(("arbitrary", "arbitrary")),
        name="nsa_attention",
    )(*args)


def _const_spec(shape):
    return pl.BlockSpec(shape, lambda i: (0,) * len(shape), pipeline_mode=pl.Buffered(1))


def _mem_kv_kernel(mem_ref, g_ref, wk_ref, wv_ref, k_ref, v_ref):
    mb = _rms_rows(mem_ref[...], g_ref[...]).astype(BF16)
    k_ref[...] = _dot(mb, wk_ref[...]).astype(BF16)
    v_ref[...] = _dot(mb, wv_ref[...]).astype(BF16)


def _mem_kv(mem, g, wk, wv):
    M, D = mem.shape
    W = wk.shape[1]
    full = lambda shape: pl.BlockSpec(shape, lambda i: (0, 0))
    return pl.pallas_call(
        _mem_kv_kernel,
        grid=(1,),
        in_specs=[full((M, D)), full((1, D)), full((D, W)), full((D, W))],
        out_specs=[full((M, W)), full((M, W))],
        out_shape=[jax.ShapeDtypeStruct((M, W), BF16)] * 2,
        compiler_params=_cparams(("arbitrary",)),
        name="mem_kv",
    )(mem, g, wk, wv)


def _route(h, g_ref, rhi_ref, rlo_ref, u_ref, idx_ref, w_ref):
    u = _rms_rows(h, g_ref[...])
    u_ref[...] = u
    u_hi = u.astype(BF16)
    u_lo = (u - u_hi.astype(F32)).astype(BF16)
    rhi = rhi_ref[...]
    logits = (_dot(u_hi, rhi) + _dot(u_lo, rhi)) + _dot(u_hi, rlo_ref[...])
    lane = lax.broadcasted_iota(jnp.int32, logits.shape, 1)
    lane_f = lane.astype(F32)
    s = jnp.where(lane < N_EXPERTS, logits, REMOVED)
    m1 = jnp.max(s, axis=1, keepdims=True)
    i1 = jnp.min(jnp.where(s == m1, lane_f, float(LANES)), axis=1, keepdims=True)
    s = jnp.where(lane_f == i1, REMOVED, s)
    m2 = jnp.max(s, axis=1, keepdims=True)
    i2 = jnp.min(jnp.where(s == m2, lane_f, float(LANES)), axis=1, keepdims=True)
    e2 = jnp.exp(m2 - m1)
    den = 1.0 + e2
    idx_ref[...] = jnp.where(lane == 0, i1, jnp.where(lane == 1, i2, 0.0)).astype(jnp.int32)
    w_ref[...] = jnp.where(lane == 0, 1.0 / den, jnp.where(lane == 1, e2 / den, 0.0))


def _router_kernel(h_ref, g_ref, rhi_ref, rlo_ref, u_ref, idx_ref, w_ref):
    _route(h_ref[...], g_ref, rhi_ref, rlo_ref, u_ref, idx_ref, w_ref)


def _router(h, g, r_hi, r_lo, tm):
    S, D = h.shape
    rows = lambda width: pl.BlockSpec((tm, width), lambda i: (i, 0))
    return pl.pallas_call(
        _router_kernel,
        grid=(S // tm,),
        in_specs=[rows(D), _const_spec(g.shape), _const_spec(r_hi.shape), _const_spec(r_lo.shape)],
        out_specs=[rows(D), rows(LANES), rows(LANES)],
        out_shape=[jax.ShapeDtypeStruct((S, D), F32), jax.ShapeDtypeStruct((S, LANES), jnp.int32),
                   jax.ShapeDtypeStruct((S, LANES), F32)],
        compiler_params=_cparams(("arbitrary",)),
        name="moe_router",
    )(h, g, r_hi, r_lo)


def _mem_attn_kernel(*refs, pre_proj):
    it = iter(refs)
    if pre_proj:
        a_ref, wpre_ref = next(it), next(it)
    res_ref, g_ref, wq_ref, k_ref, v_ref, wo_ref, o_ref = it

    h = res_ref[...]
    if pre_proj:
        h = h + _dot(a_ref[...], wpre_ref[...])
    ub = _rms_rows(h, g_ref[...]).astype(BF16)
    q = (_dot(ub, wq_ref[...]) * (MEM_HEAD_DIM ** -0.5)).astype(BF16)
    outs = []
    for hd in range(MEM_HEADS):
        cols = slice(hd * MEM_HEAD_DIM, (hd + 1) * MEM_HEAD_DIM)
        s = _dot_t(q[:, cols], k_ref[:, cols])
        e = jnp.exp(s - jnp.max(s, axis=1, keepdims=True))
        p = e / jnp.sum(e, axis=1, keepdims=True)
        outs.append(_dot(p.astype(BF16), v_ref[:, cols]).astype(BF16))
    o_ref[...] = h + _dot(jnp.concatenate(outs, axis=1), wo_ref[...])


def _mem_attn(res, g, wq, k, v, wo, tm, pre_proj=None):
    S, D = res.shape
    rows = lambda width: pl.BlockSpec((tm, width), lambda i: (i, 0))
    args, in_specs = [], []
    if pre_proj is not None:
        a, w_pre = pre_proj
        args += [a, w_pre]
        in_specs += [rows(a.shape[1]), _const_spec(w_pre.shape)]
    args += [res, g, wq, k, v, wo]
    in_specs += [rows(D)] + [_const_spec(x.shape) for x in (g, wq, k, v, wo)]
    return pl.pallas_call(
        functools.partial(_mem_attn_kernel, pre_proj=pre_proj is not None),
        grid=(S // tm,),
        in_specs=in_specs,
        out_specs=rows(D),
        out_shape=jax.ShapeDtypeStruct((S, D), F32),
        compiler_params=_cparams(("arbitrary",)),
        name="mem_attention",
    )(*args)


def _swiglu_tile(ub, wg, wu, wd):
    gate = _dot(ub, wg)
    up = _dot(ub, wu)
    act = (gate * jax.nn.sigmoid(gate)) * up
    return _dot(act.astype(BF16), wd)


def _ffn_kernel(h_ref, g_ref, wg_ref, wu_ref, wd_ref, o_ref, u_ref):
    f = pl.program_id(1)

    @pl.when(f == 0)
    def _():
        h = h_ref[...]
        u_ref[...] = _rms_rows(h, g_ref[...]).astype(BF16)
        o_ref[...] = h

    o_ref[...] += _swiglu_tile(u_ref[...], wg_ref[...], wu_ref[...], wd_ref[...])


def _ffn(h, g, wg, wu, wd, tm, tf):
    S, D = h.shape
    F = wg.shape[1]
    return pl.pallas_call(
        _ffn_kernel,
        grid=(S // tm, F // tf),
        in_specs=[
            pl.BlockSpec((tm, D), lambda i, f: (i, 0)),
            pl.BlockSpec((1, D), lambda i, f: (0, 0)),
            pl.BlockSpec((D, tf), lambda i, f: (0, f)),
            pl.BlockSpec((D, tf), lambda i, f: (0, f)),
            pl.BlockSpec((tf, D), lambda i, f: (f, 0)),
        ],
        out_specs=pl.BlockSpec((tm, D), lambda i, f: (i, 0)),
        out_shape=jax.ShapeDtypeStruct((S, D), F32),
        scratch_shapes=[pltpu.VMEM((tm, D), BF16)],
        compiler_params=_cparams(("arbitrary", "arbitrary")),
        name="dense_swiglu",
    )(h, g, wg, wu, wd)


POOL_HALO = 16


def _pool_kernel(h_ref, halo_ref, g_ref, w_ref, b_ref, scale_ref, o_ref, ext_ref):
    i = pl.program_id(0)
    tm = h_ref.shape[0]
    h = h_ref[...]
    g = g_ref[...]
    u = _rms_rows(h, g)
    halo = _rms_rows(halo_ref[...], g)
    ext_ref[0:POOL_HALO, :] = jnp.where(i > 0, halo, 0.0)
    ext_ref[POOL_HALO:, :] = u
    pos = i * tm + lax.broadcasted_iota(jnp.int32, (tm, 1), 0)
    for gi, w in enumerate(POOL_WINDOWS):
        cols = slice(gi * POOL_GROUP, (gi + 1) * POOL_GROUP)
        tot = u[:, cols]
        for k in range(1, w):
            tot = tot + ext_ref[POOL_HALO - k:POOL_HALO - k + tm, cols]
        cnt = jnp.minimum(pos + 1, w).astype(F32)
        d = tot / cnt - u[:, cols]
        z = _dot(d.astype(BF16), w_ref[gi]) + b_ref[:, cols]
        o_ref[:, cols] = h[:, cols] + z * scale_ref[:, cols]


def _pool_mixer(h, g, w, b, scale, tm):
    S, D = h.shape
    ratio = tm // POOL_HALO
    return pl.pallas_call(
        _pool_kernel,
        grid=(S // tm,),
        in_specs=[
            pl.BlockSpec((tm, D), lambda i: (i, 0)),
            pl.BlockSpec((POOL_HALO, D), lambda i: (jnp.maximum(i * ratio - 1, 0), 0)),
            pl.BlockSpec((1, D), lambda i: (0, 0)),
            pl.BlockSpec(w.shape, lambda i: (0, 0, 0)),
            pl.BlockSpec((1, D), lambda i: (0, 0)),
            pl.BlockSpec((1, D), lambda i: (0, 0)),
        ],
        out_specs=pl.BlockSpec((tm, D), lambda i: (i, 0)),
        out_shape=jax.ShapeDtypeStruct((S, D), F32),
        scratch_shapes=[pltpu.VMEM((tm + POOL_HALO, D), F32)],
        compiler_params=_cparams(("arbitrary",)),
        name="pool_mixer",
    )(h, h, g, w, b, scale)


GATHER_UNROLL = 8


def _start_row_gather(src_hbm, dst_vmem, sem, n_rows, src_row_of):
    def body(r, c):
        pltpu.make_async_copy(src_hbm.at[pl.ds(src_row_of(r), 1), :], dst_vmem.at[pl.ds(r, 1), :], sem).start()
        return c

    lax.fori_loop(0, n_rows, body, 0, unroll=GATHER_UNROLL)


def _wait_row_gather(src_hbm, dst_vmem, sem):
    pltpu.make_async_copy(src_hbm.at[pl.ds(0, dst_vmem.shape[0]), :], dst_vmem, sem).wait()


def _moe_expert_kernel(tok_ref, exp_ref, nused_ref, u_hbm, wg_ref, wu_ref, wd_ref, o_ref, x_ref, xb_ref, sem):
    i = pl.program_id(0)
    f = pl.program_id(1)
    rows = x_ref.shape[1]
    n_used = nused_ref[0]
    active = i < n_used
    slot = i & 1

    def gather(block, into):
        _start_row_gather(u_hbm, x_ref.at[into], sem.at[into], rows, lambda r: tok_ref[block * rows + r])

    @pl.when((i == 0) & (f == 0) & active)
    def _():
        gather(0, 0)

    @pl.when(active & (f == 0))
    def _():
        _wait_row_gather(u_hbm, x_ref.at[slot], sem.at[slot])
        xb_ref[...] = x_ref[slot].astype(BF16)
        o_ref[...] = jnp.zeros_like(o_ref)

        @pl.when(i + 1 < n_used)
        def _():
            gather(i + 1, 1 - slot)

    @pl.when(active)
    def _():
        o_ref[...] += _swiglu_tile(xb_ref[...], wg_ref[0], wu_ref[0], wd_ref[0])

    @pl.when(jnp.logical_not(active) & (f == 0))
    def _():
        o_ref[...] = jnp.zeros_like(o_ref)


def _moe_experts(tok, blk_exp, n_used, u, wg, wu, wd, tf):
    N, D = u.shape
    n_blk = blk_exp.shape[0]
    F = wg.shape[2]
    rows = MOE_BLOCK
    n_f = F // tf

    def ftile(i, f, nu):
        return jnp.where(i < nu[0], f, n_f - 1)

    grid_spec = pltpu.PrefetchScalarGridSpec(
        num_scalar_prefetch=3,
        grid=(n_blk, n_f),
        in_specs=[
            pl.BlockSpec(memory_space=pl.ANY),
            pl.BlockSpec((1, D, tf), lambda i, f, tok, ex, nu: (ex[i], 0, ftile(i, f, nu))),
            pl.BlockSpec((1, D, tf), lambda i, f, tok, ex, nu: (ex[i], 0, ftile(i, f, nu))),
            pl.BlockSpec((1, tf, D), lambda i, f, tok, ex, nu: (ex[i], ftile(i, f, nu), 0)),
        ],
        out_specs=pl.BlockSpec((rows, D), lambda i, f, tok, ex, nu: (i, 0)),
        scratch_shapes=[pltpu.VMEM((2, rows, D), F32), pltpu.VMEM((rows, D), BF16), pltpu.SemaphoreType.DMA((2,))],
    )
    return pl.pallas_call(
        _moe_expert_kernel,
        grid_spec=grid_spec,
        out_shape=jax.ShapeDtypeStruct((n_blk * rows, D), F32),
        compiler_params=_cparams(("arbitrary", "arbitrary")),
        name="moe_experts",
    )(tok, blk_exp, n_used, u, wg, wu, wd)


def _moe_combine_kernel(pos_ref, h_ref, w_ref, gfin_ref, y_hbm, o_ref, r_ref, sem):
    i = pl.program_id(0)
    n_steps = pl.num_programs(0)
    tm = h_ref.shape[0]
    slot = i & 1

    def gather(step, into):
        for k in range(TOP_K):
            _start_row_gather(y_hbm, r_ref.at[into, k], sem.at[into, k], tm,
                              lambda r: pos_ref[(step * tm + r) * TOP_K + k])

    @pl.when(i == 0)
    def _():
        gather(0, 0)

    for k in range(TOP_K):
        _wait_row_gather(y_hbm, r_ref.at[slot, k], sem.at[slot, k])

    @pl.when(i + 1 < n_steps)
    def _():
        gather(i + 1, 1 - slot)

    w = w_ref[...]
    y = h_ref[...] + (r_ref[slot, 0] * w[:, 0:1] + r_ref[slot, 1] * w[:, 1:2])
    o_ref[...] = _rms_rows(y, gfin_ref[...])


def _moe_combine(pos, h, w, g_final, y, tm):
    S, D = h.shape
    grid_spec = pltpu.PrefetchScalarGridSpec(
        num_scalar_prefetch=1,
        grid=(S // tm,),
        in_specs=[
            pl.BlockSpec((tm, D), lambda i, pos: (i, 0)),
            pl.BlockSpec((tm, LANES), lambda i, pos: (i, 0)),
            pl.BlockSpec((1, D), lambda i, pos: (0, 0)),
            pl.BlockSpec(memory_space=pl.ANY),
        ],
        out_specs=pl.BlockSpec((tm, D), lambda i, pos: (i, 0)),
        scratch_shapes=[pltpu.VMEM((2, TOP_K, tm, D), F32), pltpu.SemaphoreType.DMA((2, TOP_K))],
    )
    return pl.pallas_call(
        _moe_combine_kernel,
        grid_spec=grid_spec,
        out_shape=jax.ShapeDtypeStruct((S, D), F32),
        compiler_params=_cparams(("arbitrary",)),
        name="moe_combine_norm",
    )(pos, h, w, g_final, y)


def _slot_scatter_kernel(dest_ref, tok_ref):
    def zero(i, c):
        tok_ref[i] = 0
        return c

    lax.fori_loop(0, tok_ref.shape[0], zero, 0, unroll=GATHER_UNROLL)

    shift = TOP_K.bit_length() - 1
    assert 1 << shift == TOP_K

    def put(s, c):
        tok_ref[dest_ref[s]] = lax.shift_right_logical(s, shift)
        return c

    lax.fori_loop(0, dest_ref.shape[0], put, 0, unroll=GATHER_UNROLL)


def _slot_scatter(dest, cap):
    return pl.pallas_call(
        _slot_scatter_kernel,
        in_specs=[pl.BlockSpec(memory_space=pltpu.SMEM)],
        out_specs=pl.BlockSpec(memory_space=pltpu.SMEM),
        out_shape=jax.ShapeDtypeStruct((cap,), jnp.int32),
        name="moe_slot_scatter",
    )(dest)


def _moe_dispatch_indices(top_idx):
    N = top_idx.shape[0]
    e_flat = top_idx.reshape(-1)
    onehot = (e_flat[:, None] == jnp.arange(N_EXPERTS, dtype=jnp.int32)[None, :]).astype(jnp.int32)
    csum = jnp.cumsum(onehot, axis=0)
    counts = csum[-1]
    rank = jnp.take_along_axis(csum, e_flat[:, None], axis=1)[:, 0] - 1
    padded = (counts + MOE_BLOCK - 1) // MOE_BLOCK * MOE_BLOCK
    pend = jnp.cumsum(padded)
    poff = pend - padded
    dest = poff[e_flat] + rank
    n_blk = (N * TOP_K + MOE_BLOCK - 1) // MOE_BLOCK + N_EXPERTS
    cap = n_blk * MOE_BLOCK
    tok_buf = _slot_scatter(dest.astype(jnp.int32), cap)
    blk_start = jnp.arange(n_blk, dtype=jnp.int32) * MOE_BLOCK
    blk_exp = jnp.minimum(jnp.sum(pend[None, :] <= blk_start[:, None], axis=1), N_EXPERTS - 1).astype(jnp.int32)
    n_used = (pend[-1] // MOE_BLOCK).astype(jnp.int32).reshape(1)
    return tok_buf, blk_exp, n_used, dest.astype(jnp.int32)


def _rope_tables(pos):
    half = HEAD_DIM // 2
    inv = ROPE_THETA ** (-np.arange(half, dtype=np.float64) / half)
    ang = np.asarray(pos, np.float64)[:, None] * inv[None, :]
    cos, sin = np.cos(ang), np.sin(ang)
    return (jnp.asarray(np.concatenate([cos, cos], axis=1), F32),
            jnp.asarray(np.concatenate([-sin, sin], axis=1), F32))


def _gate_params(w_gates, gate_b):
    D = w_gates.shape[0]
    w = w_gates.reshape(D, 3, N_KV_GROUPS, HEADS_PER_GROUP).transpose(0, 2, 1, 3).reshape(D, N_KV_GROUPS, 3 * HEADS_PER_GROUP)
    w = jnp.pad(w, ((0, 0), (0, 0), (0, LANES - 3 * HEADS_PER_GROUP))).reshape(D, N_KV_GROUPS * LANES)
    b = gate_b.reshape(3, N_KV_GROUPS, HEADS_PER_GROUP).transpose(1, 0, 2).reshape(N_KV_GROUPS, 3 * HEADS_PER_GROUP)
    b = jnp.pad(b, ((0, 0), (0, LANES - 3 * HEADS_PER_GROUP))).reshape(1, N_KV_GROUPS * LANES)
    return w.astype(BF16), b


def _chunked_groups(t):
    S = t.shape[0]
    return (t.reshape(S // CMP_STRIDE, CMP_STRIDE, N_KV_GROUPS, HEAD_DIM)
            .transpose(2, 0, 1, 3).reshape(N_KV_GROUPS, S // CMP_STRIDE, CMP_STRIDE * HEAD_DIM))


def _row_tile(S, want):
    t = min(S, want)
    assert S % t == 0
    return t


def kernel(x, mem, norm_mix, norm_mem_q, norm_mem_kv, norm_ffn, norm_final, nsa_w_in, nsa_gate_b, nsa_pe_k, nsa_pe_v, nsa_cmp_k_w1, nsa_cmp_k_w2, nsa_cmp_v_w1, nsa_cmp_v_w2, nsa_w_out, pool_w, pool_b, pool_scale, mem_wq, mem_wk, mem_wv, mem_wo, ffn_w_gate, ffn_w_up, ffn_w_down, moe_router, moe_w_gate, moe_w_up, moe_w_down):
    B, S, D = x.shape
    assert B == 1 and S % WINDOW == 0 and S >= SLC_BLOCK * N_SLC
    h = x.reshape(S, D)
    memf = mem.reshape(mem.shape[1], D)
    row = lambda v: v.reshape(1, -1)

    n_main = N_HEADS * HEAD_DIM + 6 * KV_WIDTH
    w_in = nsa_w_in[0]
    w_gate, b_gate = _gate_params(w_in[:, n_main:], nsa_gate_b[0])
    cos, sin = _rope_tables(np.arange(S))
    proj, gates = _nsa_proj(h, row(norm_mix[0]), w_in[:, :n_main].astype(BF16), w_gate, b_gate, cos, sin,
                            _row_tile(S, 1024))
    q_w = N_HEADS * HEAD_DIM
    n_chunks = S // CMP_STRIDE
    cmp_pos = np.arange(n_chunks) * CMP_STRIDE + CMP_BLOCK - 1
    ccos, csin = _rope_tables(cmp_pos)
    kc = _compress(_chunked_groups(proj[:, q_w:q_w + KV_WIDTH]), nsa_pe_k[0].reshape(1, -1),
                   nsa_cmp_k_w1[0].astype(BF16), nsa_cmp_k_w2[0].astype(BF16), ccos, csin, True)
    vct = _compress(_chunked_groups(proj[:, q_w + KV_WIDTH:q_w + 2 * KV_WIDTH]), nsa_pe_v[0].reshape(1, -1),
                    nsa_cmp_v_w1[0].astype(BF16), nsa_cmp_v_w2[0].astype(BF16), ccos, csin, False).transpose(0, 2, 1)
    attn = _nsa_attention(proj, gates, kc, vct)

    def mem_layer(h, i, **fused):
        k, v = _mem_kv(memf, row(norm_mem_kv[i]), mem_wk[i].astype(BF16), mem_wv[i].astype(BF16))
        return _mem_attn(h, row(norm_mem_q[i]), mem_wq[i].astype(BF16), k, v, mem_wo[i].astype(BF16),
                         _row_tile(S, 512), **fused)

    h = mem_layer(h, 0, pre_proj=(attn, nsa_w_out[0].astype(BF16)))
    h = _ffn(h, row(norm_ffn[0]), ffn_w_gate[0].astype(BF16), ffn_w_up[0].astype(BF16),
             ffn_w_down[0].astype(BF16), _row_tile(S, 1024), 512)

    h = _pool_mixer(h, row(norm_mix[1]), pool_w[0].astype(BF16), row(pool_b[0]), row(pool_scale[0]),
                    _row_tile(S, 512))
    h = mem_layer(h, 1)
    r = jnp.pad(moe_router[0], ((0, 0), (0, LANES - N_EXPERTS)))
    r_hi = r.astype(BF16)
    r_lo = (r - r_hi.astype(F32)).astype(BF16)
    u, idx, gate_w = _router(h, row(norm_ffn[1]), r_hi, r_lo, _row_tile(S, 512))
    tok_buf, blk_exp, n_used, dest = _moe_dispatch_indices(idx[:, :TOP_K])
    y = _moe_experts(tok_buf, blk_exp, n_used, u, moe_w_gate[0].astype(BF16), moe_w_up[0].astype(BF16),
                     moe_w_down[0].astype(BF16), 512)
    out = _moe_combine(dest, h, gate_w, row(norm_final), y, _row_tile(S, 256))
    return out.reshape(B, S, D)
```

```python
import functools

import numpy as np
import jax
import jax.numpy as jnp
from jax import lax
from jax.experimental import pallas as pl
from jax.experimental.pallas import tpu as pltpu

F32 = jnp.float32
BF16 = jnp.bfloat16

N_HEADS = 16
HEAD_DIM = 128
N_KV_GROUPS = 4
HEADS_PER_GROUP = 4
KV_WIDTH = N_KV_GROUPS * HEAD_DIM
CMP_BLOCK = 32
CMP_STRIDE = 16
SLC_BLOCK = 64
RATIO = SLC_BLOCK // CMP_STRIDE
N_SLC = 16
WINDOW = 512
Q_BLOCK = 128
ROPE_THETA = 10000.0
POOL_WINDOWS = (2, 4, 8, 16)
POOL_GROUP = 512
MEM_HEADS = 4
MEM_HEAD_DIM = 128
N_EXPERTS = 8
TOP_K = 2
MOE_BLOCK = 512
EPS = 1e-6
NEG_INF = -1e30
FORCE = 1e9
REMOVED = -3e38
LOG2E = float(np.log2(np.e))

LANES = 128
VMEM_LIMIT = 56 * 1024 * 1024

SEL_KEY_TILE = 512


def _cparams(sem):
    return pltpu.CompilerParams(dimension_semantics=sem, vmem_limit_bytes=VMEM_LIMIT)


def _rms_rows(xf, g):
    r = lax.rsqrt(jnp.mean(xf * xf, axis=-1, keepdims=True) + EPS)
    return (xf * r) * g


def _dot(a, b):
    return jnp.dot(a, b, preferred_element_type=F32)


def _dot_t(a, b):
    return lax.dot_general(a, b, (((1,), (1,)), ((), ())), preferred_element_type=F32)


def _proj_kernel(x_ref, g_ref, w_ref, wgate_ref, bgate_ref, cos_ref, sin_ref, o_ref, gates_ref, u_ref):
    j = pl.program_id(1)

    @pl.when(j == 0)
    def _():
        ub = _rms_rows(x_ref[...], g_ref[...]).astype(BF16)
        u_ref[...] = ub
        gates_ref[...] = jax.nn.sigmoid(_dot(ub, wgate_ref[...]) + bgate_ref[...])

    acc = _dot(u_ref[...], w_ref[...])
    n_q_tiles = N_HEADS * HEAD_DIM // KV_WIDTH
    is_q = j < n_q_tiles
    is_rope = is_q | (j == n_q_tiles + 2) | (j == n_q_tiles + 4)

    @pl.when(is_rope)
    def _():
        scale = jnp.where(is_q, HEAD_DIM ** -0.5 * LOG2E, 1.0).astype(F32)
        cos = cos_ref[...]
        sin = sin_ref[...]
        for h in range(KV_WIDTH // HEAD_DIM):
            xh = acc[:, h * HEAD_DIM:(h + 1) * HEAD_DIM]
            rot = pltpu.roll(xh, HEAD_DIM // 2, axis=1)
            o_ref[:, h * HEAD_DIM:(h + 1) * HEAD_DIM] = ((xh * cos + rot * sin) * scale).astype(BF16)

    @pl.when(jnp.logical_not(is_rope))
    def _():
        o_ref[...] = acc.astype(BF16)


def _nsa_proj(x, g, w_main, w_gate, b_gate, cos, sin, tm):
    S, D = x.shape
    n_main = w_main.shape[1]
    tn = KV_WIDTH
    n_gate = w_gate.shape[1]
    return pl.pallas_call(
        _proj_kernel,
        grid=(S // tm, n_main // tn),
        in_specs=[
            pl.BlockSpec((tm, D), lambda i, j: (i, 0)),
            pl.BlockSpec((1, D), lambda i, j: (0, 0)),
            pl.BlockSpec((D, tn), lambda i, j: (0, j)),
            pl.BlockSpec((D, n_gate), lambda i, j: (0, 0)),
            pl.BlockSpec((1, n_gate), lambda i, j: (0, 0)),
            pl.BlockSpec((tm, HEAD_DIM), lambda i, j: (i, 0)),
            pl.BlockSpec((tm, HEAD_DIM), lambda i, j: (i, 0)),
        ],
        out_specs=[
            pl.BlockSpec((tm, tn), lambda i, j: (i, j)),
            pl.BlockSpec((tm, n_gate), lambda i, j: (i, 0)),
        ],
        out_shape=[
            jax.ShapeDtypeStruct((S, n_main), BF16),
            jax.ShapeDtypeStruct((S, n_gate), F32),
        ],
        scratch_shapes=[pltpu.VMEM((tm, D), BF16)],
        compiler_params=_cparams(("arbitrary", "arbitrary")),
        name="nsa_proj",
    )(x, g, w_main, w_gate, b_gate, cos, sin)


def _gelu_tanh(x):
    return 0.5 * x * (1.0 + jnp.tanh(np.sqrt(2.0 / np.pi).astype(np.float32) * (x + 0.044715 * (x * x * x))))


def _compress_kernel(t_ref, pe_ref, w1_ref, w2_ref, cos_ref, sin_ref, o_ref, *, apply_rope):
    half = CMP_STRIDE * HEAD_DIM
    t = t_ref[0].astype(F32)
    n = t.shape[0]
    a = _dot((t + pe_ref[:, :half]).astype(BF16), w1_ref[:half, :])
    b = _dot((t + pe_ref[:, half:]).astype(BF16), w1_ref[half:, :])
    hid = a + pltpu.roll(b, n - 1, axis=0)
    out = _dot(_gelu_tanh(hid).astype(BF16), w2_ref[...])
    if apply_rope:
        rot = pltpu.roll(out, HEAD_DIM // 2, axis=1)
        out = out * cos_ref[...] + rot * sin_ref[...]
    o_ref[0] = out.astype(BF16)


def _compress(t, pe_flat, w1, w2, cos, sin, apply_rope):
    G, n, width = t.shape
    hidden = w1.shape[1]
    out_block = (1, n, HEAD_DIM)
    return pl.pallas_call(
        functools.partial(_compress_kernel, apply_rope=apply_rope),
        grid=(G,),
        in_specs=[
            pl.BlockSpec((1, n, width), lambda g: (g, 0, 0)),
            pl.BlockSpec((1, 2 * width), lambda g: (0, 0)),
            pl.BlockSpec((2 * width, hidden), lambda g: (0, 0)),
            pl.BlockSpec((hidden, HEAD_DIM), lambda g: (0, 0)),
            pl.BlockSpec((n, HEAD_DIM), lambda g: (0, 0)),
            pl.BlockSpec((n, HEAD_DIM), lambda g: (0, 0)),
        ],
        out_specs=pl.BlockSpec(out_block, lambda g: (g, 0, 0)),
        out_shape=jax.ShapeDtypeStruct((G,) + out_block[1:], BF16),
        compiler_params=_cparams(("arbitrary",)),
        name="nsa_compress_k" if apply_rope else "nsa_compress_v",
    )(t, pe_flat, w1, w2, cos, sin)


def _softmax_cols(s):
    e = jnp.exp2(s - jnp.max(s, axis=0, keepdims=True))
    return e, jnp.sum(e, axis=0, keepdims=True)


def _tile_heads(x):
    return jnp.concatenate([x] * HEADS_PER_GROUP, axis=1)


SEL_PAD = 16
V_EXTRA_ROWS = 16


def _run_on_shortest_prefix(fn, total, needed, parts):
    piece = total // parts
    sizes = [piece * k for k in range(1, parts + 1)] if piece % LANES == 0 else [total]
    for idx, n in enumerate(sizes):
        cond = needed > (sizes[idx - 1] if idx else 0)
        if idx < len(sizes) - 1:
            cond = cond & (needed <= n)
        pl.when(cond)(functools.partial(fn, n))


def _nsa_attn_kernel(q_ref, gates_ref, kc_ref, vct_ref, ks_ref, vst_ref, wselt_ref, onehot_ref, *rest, n_sel):
    n_win = WINDOW // Q_BLOCK + 1
    kw_refs = rest[:n_win]
    vwt_refs = rest[n_win:2 * n_win]
    o_ref = rest[2 * n_win]
    gall_ref, s0_ref, s1_ref, p0_ref, p1_ref, acc_ref, oc_ref, sblk_ref = rest[2 * n_win + 1:]

    b = pl.program_id(1)
    t0 = b * Q_BLOCK
    Q = Q_BLOCK
    H = HEADS_PER_GROUP
    C = H * Q
    q = q_ref[...]
    qt = jnp.concatenate(
        [q[:, h * HEAD_DIM:(h + 1) * HEAD_DIM].astype(F32).T.astype(BF16) for h in range(H)], axis=1)

    n_cmp = kc_ref.shape[1]
    nsp = wselt_ref.shape[0]
    qi_row = lax.broadcasted_iota(jnp.int32, (1, C), 1) & (Q - 1)

    def compressed(n):
        qi_c = lax.broadcasted_iota(jnp.int32, (n, Q), 1)
        cpos = lax.broadcasted_iota(jnp.int32, (n, Q), 0) * CMP_STRIDE + (CMP_BLOCK - 1)
        bias_c = jnp.where(cpos <= t0 + qi_c, 0.0, NEG_INF)
        e_c, l_c = _softmax_cols(_dot(kc_ref[0, 0:n, :], qt) + _tile_heads(bias_c))
        p_c = e_c * jnp.where(t0 + qi_row >= CMP_BLOCK - 1, 1.0 / l_c, 0.0)
        oc_ref[...] = _dot(vct_ref[0, :, 0:n], p_c.astype(BF16))
        imp = (p_c[:, 0:Q] + p_c[:, Q:2 * Q]) + p_c[:, 2 * Q:3 * Q] + p_c[:, 3 * Q:4 * Q]
        imp_hi = imp.astype(BF16)
        imp_lo = (imp - imp_hi.astype(F32)).astype(BF16)
        wselt = wselt_ref[:, 0:n]
        sblk_ref[...] = _dot(wselt, imp_hi) + _dot(wselt, imp_lo)

    _run_on_shortest_prefix(compressed, n_cmp, (t0 + Q - CMP_BLOCK) // CMP_STRIDE + 1, 4)
    o_c = oc_ref[...]

    def select(rows):
        blk = lax.broadcasted_iota(jnp.int32, (rows, Q), 0)
        qp = t0 + lax.broadcasted_iota(jnp.int32, (rows, Q), 1)
        cur = qp >> 6
        valid = blk * SLC_BLOCK <= qp
        forced = (blk == 0) | (blk == cur) | (blk == cur - 1)
        s_blk = jnp.where(forced, FORCE, jnp.where(valid, sblk_ref[0:rows, :], NEG_INF))
        blk_f = blk.astype(F32)

        def pick(_, carry):
            s, sel = carry
            m = jnp.max(s, axis=0, keepdims=True)
            first = jnp.min(jnp.where(s == m, blk_f, float(rows)), axis=0, keepdims=True)
            hit = blk_f == first
            return jnp.where(hit, REMOVED, s), jnp.where(hit, 1.0, sel)

        _, sel = lax.fori_loop(0, n_sel, pick, (s_blk, jnp.zeros((rows, Q), F32)))
        gall_ref[SEL_PAD:SEL_PAD + rows, :] = _tile_heads(jnp.where((sel > 0.5) & valid, 0.0, NEG_INF))
        if rows < nsp:
            gall_ref[SEL_PAD + rows:SEL_PAD + nsp, :] = jnp.full((nsp - rows, C), NEG_INF, F32)

    _run_on_shortest_prefix(select, nsp, (t0 + Q - 1) // SLC_BLOCK + 1, 2)

    T = SEL_KEY_TILE
    bpt = T // SLC_BLOCK
    n_tiles = (t0 + Q + T - 1) // T
    gall_ref[0:SEL_PAD, :] = jnp.full((SEL_PAD, C), NEG_INF, F32)
    gall_ref[SEL_PAD + nsp:, :] = jnp.full((SEL_PAD, C), NEG_INF, F32)
    onehot = onehot_ref[(b + 1) & 3]
    zero_rows = jnp.zeros((HEAD_DIM - 2 * bpt, C), BF16)
    k_rows = ks_ref.shape[1]
    v_blocks = vst_ref.shape[1]

    def scores(kt):
        start = pl.multiple_of(jnp.clip(t0 - kt * T, 0, k_rows - T), Q)
        first_blk = 2 * b + 2 - bpt * (kt + 1) + SEL_PAD
        w0 = pl.multiple_of(jnp.maximum((first_blk >> 3) << 3, 0), 8)
        table = gall_ref[pl.ds(w0, 2 * bpt), :].astype(BF16)
        lhs = jnp.concatenate([ks_ref[0, pl.ds(start, T), :], onehot], axis=1)
        rhs = jnp.concatenate([qt, table, zero_rows], axis=0)
        return _dot(lhs, rhs)

    def values(kt):
        vb = jnp.clip(b - (T // Q) * kt, 0, v_blocks - T // Q)
        return jnp.concatenate([vst_ref[0, vb + c] for c in range(T // Q)], axis=1)

    def tile_step(kt, s_cur, s_nxt, p_cur, p_prv, carry):
        alpha_prev, m = carry
        acc_ref[...] = alpha_prev * acc_ref[...] + _dot(values(kt - 1), p_prv[...])
        s_nxt[...] = scores(kt + 1)
        s = s_cur[...]
        m_new = jnp.maximum(m, jnp.max(s, axis=0, keepdims=True))
        p_cur[...] = jnp.exp2((s - m_new).astype(BF16))
        return jnp.exp2(m - m_new), m_new

    def pair_step(j, carry):
        carry = tile_step(2 * j, s0_ref, s1_ref, p0_ref, p1_ref, carry)
        return tile_step(2 * j + 1, s1_ref, s0_ref, p1_ref, p0_ref, carry)

    s_first = scores(0)
    key_j = lax.broadcasted_iota(jnp.int32, (Q, Q), 0)
    qry_i = lax.broadcasted_iota(jnp.int32, (Q, Q), 1)
    s0_ref[0:T - Q, :] = s_first[0:T - Q]
    s0_ref[T - Q:, :] = s_first[T - Q:] + _tile_heads(jnp.where(key_j <= qry_i, 0.0, NEG_INF))
    p1_ref[...] = jnp.zeros_like(p1_ref)
    acc_ref[...] = jnp.zeros_like(acc_ref)
    n_pairs = (n_tiles + 1) // 2
    alpha_last, _ = lax.fori_loop(
        0, n_pairs, pair_step, (jnp.ones((1, C), F32), jnp.full((1, C), NEG_INF, F32)))
    acc_s = alpha_last * acc_ref[...] + _dot(values(2 * n_pairs - 1), p1_ref[...])
    o_s = acc_s[0:HEAD_DIM] * (1.0 / acc_s[HEAD_DIM:HEAD_DIM + 1])

    kw = jnp.concatenate([r[...] for r in kw_refs], axis=0)
    vwt = jnp.concatenate([r[0] for r in vwt_refs], axis=1)
    nw = WINDOW + Q
    row_w = lax.broadcasted_iota(jnp.int32, (nw, Q), 0)
    diff = lax.broadcasted_iota(jnp.int32, (nw, Q), 1) - row_w + WINDOW
    ok_w = (diff >= 0) & (diff < WINDOW) & (row_w >= WINDOW - t0)
    s_w = _dot(kw, qt) + _tile_heads(jnp.where(ok_w, 0.0, NEG_INF))
    e_w = jnp.exp2((s_w - jnp.max(s_w, axis=0, keepdims=True)).astype(BF16))
    acc_w = _dot(vwt, e_w)
    o_w = acc_w[0:HEAD_DIM] * (1.0 / acc_w[HEAD_DIM:HEAD_DIM + 1])

    gt = gates_ref[...].T
    for h in range(H):
        cols = slice(h * Q, (h + 1) * Q)
        o = (gt[h:h + 1] * o_c[:, cols] + gt[H + h:H + h + 1] * o_s[:, cols]
             + gt[2 * H + h:2 * H + h + 1] * o_w[:, cols])
        o_ref[:, h * HEAD_DIM:(h + 1) * HEAD_DIM] = o.T.astype(BF16)


def _selection_matrix_t(n_cmp, nsp):
    j = np.arange(nsp)[:, None]
    c = np.arange(n_cmp)[None, :]
    d = c - RATIO * j
    w = np.where((d == -1) | (d == RATIO - 1), 1.0, np.where((d >= 0) & (d < RATIO - 1), 2.0, 0.0))
    return jnp.asarray(w, BF16)


def _nsa_attention(proj, gates, kc, vct):
    S = proj.shape[0]
    G = N_KV_GROUPS
    T = SEL_KEY_TILE
    nb = S // Q_BLOCK
    n_cmp = kc.shape[1]
    ns = S // SLC_BLOCK
    nsp = -(-ns // LANES) * LANES
    n_sel = min(N_SLC, ns)
    wselt = _selection_matrix_t(n_cmp, nsp)
    n_win = WINDOW // Q_BLOCK + 1
    cpb = KV_WIDTH // HEAD_DIM
    q_w = N_HEADS * HEAD_DIM
    q_cols = q_w // HEAD_DIM
    kw_col = q_cols + 4 * cpb
    pad = T - Q_BLOCK
    ksp = jnp.pad(proj[:, q_w + 2 * KV_WIDTH:q_w + 3 * KV_WIDTH], ((pad, 0), (0, 0)))
    ksp = ksp.reshape(S + pad, G, HEAD_DIM).transpose(1, 0, 2)
    vsp = jnp.pad(proj[:, q_w + 3 * KV_WIDTH:q_w + 4 * KV_WIDTH], ((pad, 0), (0, 0)))
    vst = vsp.reshape((S + pad) // Q_BLOCK, Q_BLOCK, G, HEAD_DIM).transpose(2, 0, 3, 1)
    extra = jnp.zeros(vst.shape[:2] + (V_EXTRA_ROWS, Q_BLOCK), BF16).at[:, :, 0, :].set(1.0)
    vst = jnp.concatenate([vst, extra], axis=2)
    vwt = proj[:, q_w + 5 * KV_WIDTH:q_w + 6 * KV_WIDTH].reshape(S, G, HEAD_DIM).transpose(1, 2, 0)
    vwt = jnp.concatenate([vwt, jnp.zeros((G, V_EXTRA_ROWS, S), BF16).at[:, 0, :].set(1.0)], axis=1)
    r = np.arange(T)[None, :, None] // SLC_BLOCK
    onehot = jnp.asarray(np.arange(LANES)[None, None, :] == 2 * np.arange(4)[:, None, None] + r, BF16)

    def win_block(b, i):
        return jnp.maximum(b - (n_win - 1) + i, 0)

    in_specs = [
        pl.BlockSpec((Q_BLOCK, KV_WIDTH), lambda g, b: (b, g)),
        pl.BlockSpec((Q_BLOCK, LANES), lambda g, b: (b, g)),
        pl.BlockSpec((1, n_cmp, HEAD_DIM), lambda g, b: (g, 0, 0)),
        pl.BlockSpec((1, HEAD_DIM, n_cmp), lambda g, b: (g, 0, 0)),
        pl.BlockSpec((1,) + ksp.shape[1:], lambda g, b: (g, 0, 0)),
        pl.BlockSpec((1,) + vst.shape[1:], lambda g, b: (g, 0, 0, 0)),
        pl.BlockSpec(wselt.shape, lambda g, b: (0, 0)),
        pl.BlockSpec(onehot.shape, lambda g, b: (0, 0, 0)),
    ]
    in_specs += [pl.BlockSpec((Q_BLOCK, HEAD_DIM), functools.partial(lambda g, b, i: (win_block(b, i), kw_col + g), i=i))
                 for i in range(n_win)]
    in_specs += [pl.BlockSpec((1, vwt.shape[1], Q_BLOCK), functools.partial(lambda g, b, i: (g, 0, win_block(b, i)), i=i))
                 for i in range(n_win)]
    args = [proj, gates, kc, vct, ksp, vst, wselt, onehot] + [proj] * n_win + [vwt] * n_win
    return pl.pallas_call(
        functools.partial(_nsa_attn_kernel, n_sel=n_sel),
        grid=(G, nb),
        in_specs=in_specs,
        out_specs=pl.BlockSpec((Q_BLOCK, KV_WIDTH), lambda g, b: (b, g)),
        out_shape=jax.ShapeDtypeStruct((S, N_HEADS * HEAD_DIM), BF16),
        scratch_shapes=[pltpu.VMEM((nsp + 2 * SEL_PAD, KV_WIDTH), F32),
                        pltpu.VMEM((T, KV_WIDTH), F32), pltpu.VMEM((T, KV_WIDTH), F32),
                        pltpu.VMEM((T, KV_WIDTH), BF16), pltpu.VMEM((T, KV_WIDTH), BF16),
                        pltpu.VMEM((HEAD_DIM + V_EXTRA_ROWS, KV_WIDTH), F32),
                        pltpu.VMEM((HEAD_DIM, KV_WIDTH), F32), pltpu.VMEM((nsp, Q_BLOCK), F32)],
        compiler_params=_cparams(("arbitrary", "arbitrary")),
        name="nsa_attention",
    )(*args)


def _const_spec(shape):
    return pl.BlockSpec(shape, lambda i: (0,) * len(shape), pipeline_mode=pl.Buffered(1))


def _mem_kv_kernel(mem_ref, g_ref, wk_ref, wv_ref, k_ref, v_ref):
    mb = _rms_rows(mem_ref[...], g_ref[...]).astype(BF16)
    k_ref[...] = _dot(mb, wk_ref[...]).astype(BF16)
    v_ref[...] = _dot(mb, wv_ref[...]).astype(BF16)


def _mem_kv(mem, g, wk, wv):
    M, D = mem.shape
    W = wk.shape[1]
    full = lambda shape: pl.BlockSpec(shape, lambda i: (0, 0))
    return pl.pallas_call(
        _mem_kv_kernel,
        grid=(1,),
        in_specs=[full((M, D)), full((1, D)), full((D, W)), full((D, W))],
        out_specs=[full((M, W)), full((M, W))],
        out_shape=[jax.ShapeDtypeStruct((M, W), BF16)] * 2,
        compiler_params=_cparams(("arbitrary",)),
        name="mem_kv",
    )(mem, g, wk, wv)


def _route(h, g_ref, rhi_ref, rlo_ref, u_ref, idx_ref, w_ref):
    u = _rms_rows(h, g_ref[...])
    u_ref[...] = u
    u_hi = u.astype(BF16)
    u_lo = (u - u_hi.astype(F32)).astype(BF16)
    rhi = rhi_ref[...]
    logits = (_dot(u_hi, rhi) + _dot(u_lo, rhi)) + _dot(u_hi, rlo_ref[...])
    lane = lax.broadcasted_iota(jnp.int32, logits.shape, 1)
    lane_f = lane.astype(F32)
    s = jnp.where(lane < N_EXPERTS, logits, REMOVED)
    m1 = jnp.max(s, axis=1, keepdims=True)
    i1 = jnp.min(jnp.where(s == m1, lane_f, float(LANES)), axis=1, keepdims=True)
    s = jnp.where(lane_f == i1, REMOVED, s)
    m2 = jnp.max(s, axis=1, keepdims=True)
    i2 = jnp.min(jnp.where(s == m2, lane_f, float(LANES)), axis=1, keepdims=True)
    e2 = jnp.exp(m2 - m1)
    den = 1.0 + e2
    idx_ref[...] = jnp.where(lane == 0, i1, jnp.where(lane == 1, i2, 0.0)).astype(jnp.int32)
    w_ref[...] = jnp.where(lane == 0, 1.0 / den, jnp.where(lane == 1, e2 / den, 0.0))


def _router_kernel(h_ref, g_ref, rhi_ref, rlo_ref, u_ref, idx_ref, w_ref):
    _route(h_ref[...], g_ref, rhi_ref, rlo_ref, u_ref, idx_ref, w_ref)


def _router(h, g, r_hi, r_lo, tm):
    S, D = h.shape
    rows = lambda width: pl.BlockSpec((tm, width), lambda i: (i, 0))
    return pl.pallas_call(
        _router_kernel,
        grid=(S // tm,),
        in_specs=[rows(D), _const_spec(g.shape), _const_spec(r_hi.shape), _const_spec(r_lo.shape)],
        out_specs=[rows(D), rows(LANES), rows(LANES)],
        out_shape=[jax.ShapeDtypeStruct((S, D), F32), jax.ShapeDtypeStruct((S, LANES), jnp.int32),
                   jax.ShapeDtypeStruct((S, LANES), F32)],
        compiler_params=_cparams(("arbitrary",)),
        name="moe_router",
    )(h, g, r_hi, r_lo)


def _mem_attn_kernel(*refs, pre_proj):
    it = iter(refs)
    if pre_proj:
        a_ref, wpre_ref = next(it), next(it)
    res_ref, g_ref, wq_ref, k_ref, v_ref, wo_ref, o_ref = it

    h = res_ref[...]
    if pre_proj:
        h = h + _dot(a_ref[...], wpre_ref[...])
    ub = _rms_rows(h, g_ref[...]).astype(BF16)
    q = (_dot(ub, wq_ref[...]) * (MEM_HEAD_DIM ** -0.5)).astype(BF16)
    outs = []
    for hd in range(MEM_HEADS):
        cols = slice(hd * MEM_HEAD_DIM, (hd + 1) * MEM_HEAD_DIM)
        s = _dot_t(q[:, cols], k_ref[:, cols])
        e = jnp.exp(s - jnp.max(s, axis=1, keepdims=True))
        p = e / jnp.sum(e, axis=1, keepdims=True)
        outs.append(_dot(p.astype(BF16), v_ref[:, cols]).astype(BF16))
    o_ref[...] = h + _dot(jnp.concatenate(outs, axis=1), wo_ref[...])


def _mem_attn(res, g, wq, k, v, wo, tm, pre_proj=None):
    S, D = res.shape
    rows = lambda width: pl.BlockSpec((tm, width), lambda i: (i, 0))
    args, in_specs = [], []
    if pre_proj is not None:
        a, w_pre = pre_proj
        args += [a, w_pre]
        in_specs += [rows(a.shape[1]), _const_spec(w_pre.shape)]
    args += [res, g, wq, k, v, wo]
    in_specs += [rows(D)] + [_const_spec(x.shape) for x in (g, wq, k, v, wo)]
    return pl.pallas_call(
        functools.partial(_mem_attn_kernel, pre_proj=pre_proj is not None),
        grid=(S // tm,),
        in_specs=in_specs,
        out_specs=rows(D),
        out_shape=jax.ShapeDtypeStruct((S, D), F32),
        compiler_params=_cparams(("arbitrary",)),
        name="mem_attention",
    )(*args)


def _swiglu_tile(ub, wg, wu, wd):
    gate = _dot(ub, wg)
    up = _dot(ub, wu)
    act = (gate * jax.nn.sigmoid(gate)) * up
    return _dot(act.astype(BF16), wd)


def _ffn_kernel(h_ref, g_ref, wg_ref, wu_ref, wd_ref, o_ref, u_ref):
    f = pl.program_id(1)

    @pl.when(f == 0)
    def _():
        h = h_ref[...]
        u_ref[...] = _rms_rows(h, g_ref[...]).astype(BF16)
        o_ref[...] = h

    o_ref[...] += _swiglu_tile(u_ref[...], wg_ref[...], wu_ref[...], wd_ref[...])


def _ffn(h, g, wg, wu, wd, tm, tf):
    S, D = h.shape
    F = wg.shape[1]
    return pl.pallas_call(
        _ffn_kernel,
        grid=(S // tm, F // tf),
        in_specs=[
            pl.BlockSpec((tm, D), lambda i, f: (i, 0)),
            pl.BlockSpec((1, D), lambda i, f: (0, 0)),
            pl.BlockSpec((D, tf), lambda i, f: (0, f)),
            pl.BlockSpec((D, tf), lambda i, f: (0, f)),
            pl.BlockSpec((tf, D), lambda i, f: (f, 0)),
        ],
        out_specs=pl.BlockSpec((tm, D), lambda i, f: (i, 0)),
        out_shape=jax.ShapeDtypeStruct((S, D), F32),
        scratch_shapes=[pltpu.VMEM((tm, D), BF16)],
        compiler_params=_cparams(("arbitrary", "arbitrary")),
        name="dense_swiglu",
    )(h, g, wg, wu, wd)


POOL_HALO = 16


def _pool_kernel(h_ref, halo_ref, g_ref, w_ref, b_ref, scale_ref, o_ref, ext_ref):
    i = pl.program_id(0)
    tm = h_ref.shape[0]
    h = h_ref[...]
    g = g_ref[...]
    u = _rms_rows(h, g)
    halo = _rms_rows(halo_ref[...], g)
    ext_ref[0:POOL_HALO, :] = jnp.where(i > 0, halo, 0.0)
    ext_ref[POOL_HALO:, :] = u
    pos = i * tm + lax.broadcasted_iota(jnp.int32, (tm, 1), 0)
    for gi, w in enumerate(POOL_WINDOWS):
        cols = slice(gi * POOL_GROUP, (gi + 1) * POOL_GROUP)
        tot = u[:, cols]
        for k in range(1, w):
            tot = tot + ext_ref[POOL_HALO - k:POOL_HALO - k + tm, cols]
        cnt = jnp.minimum(pos + 1, w).astype(F32)
        d = tot / cnt - u[:, cols]
        z = _dot(d.astype(BF16), w_ref[gi]) + b_ref[:, cols]
        o_ref[:, cols] = h[:, cols] + z * scale_ref[:, cols]


def _pool_mixer(h, g, w, b, scale, tm):
    S, D = h.shape
    ratio = tm // POOL_HALO
    return pl.pallas_call(
        _pool_kernel,
        grid=(S // tm,),
        in_specs=[
            pl.BlockSpec((tm, D), lambda i: (i, 0)),
            pl.BlockSpec((POOL_HALO, D), lambda i: (jnp.maximum(i * ratio - 1, 0), 0)),
            pl.BlockSpec((1, D), lambda i: (0, 0)),
            pl.BlockSpec(w.shape, lambda i: (0, 0, 0)),
            pl.BlockSpec((1, D), lambda i: (0, 0)),
            pl.BlockSpec((1, D), lambda i: (0, 0)),
        ],
        out_specs=pl.BlockSpec((tm, D), lambda i: (i, 0)),
        out_shape=jax.ShapeDtypeStruct((S, D), F32),
        scratch_shapes=[pltpu.VMEM((tm + POOL_HALO, D), F32)],
        compiler_params=_cparams(("arbitrary",)),
        name="pool_mixer",
    )(h, h, g, w, b, scale)


GATHER_UNROLL = 8


def _start_row_gather(src_hbm, dst_vmem, sem, n_rows, src_row_of):
    def body(r, c):
        pltpu.make_async_copy(src_hbm.at[pl.ds(src_row_of(r), 1), :], dst_vmem.at[pl.ds(r, 1), :], sem).start()
        return c

    lax.fori_loop(0, n_rows, body, 0, unroll=GATHER_UNROLL)


def _wait_row_gather(src_hbm, dst_vmem, sem):
    pltpu.make_async_copy(src_hbm.at[pl.ds(0, dst_vmem.shape[0]), :], dst_vmem, sem).wait()


def _moe_expert_kernel(tok_ref, exp_ref, nused_ref, u_hbm, wg_ref, wu_ref, wd_ref, o_ref, x_ref, xb_ref, sem):
    i = pl.program_id(0)
    f = pl.program_id(1)
    rows = x_ref.shape[1]
    n_used = nused_ref[0]
    active = i < n_used
    slot = i & 1

    def gather(block, into):
        _start_row_gather(u_hbm, x_ref.at[into], sem.at[into], rows, lambda r: tok_ref[block * rows + r])

    @pl.when((i == 0) & (f == 0) & active)
    def _():
        gather(0, 0)

    @pl.when(active & (f == 0))
    def _():
        _wait_row_gather(u_hbm, x_ref.at[slot], sem.at[slot])
        xb_ref[...] = x_ref[slot].astype(BF16)
        o_ref[...] = jnp.zeros_like(o_ref)

        @pl.when(i + 1 < n_used)
        def _():
            gather(i + 1, 1 - slot)

    @pl.when(active)
    def _():
        o_ref[...] += _swiglu_tile(xb_ref[...], wg_ref[0].astype(BF16), wu_ref[0].astype(BF16),
                                   wd_ref[0].astype(BF16))

    @pl.when(jnp.logical_not(active) & (f == 0))
    def _():
        o_ref[...] = jnp.zeros_like(o_ref)


def _moe_experts(tok, blk_exp, n_used, u, wg, wu, wd, tf):
    N, D = u.shape
    n_blk = blk_exp.shape[0]
    F = wg.shape[2]
    rows = MOE_BLOCK
    n_f = F // tf

    def ftile(i, f, nu):
        return jnp.where(i < nu[0], f, n_f - 1)

    grid_spec = pltpu.PrefetchScalarGridSpec(
        num_scalar_prefetch=3,
        grid=(n_blk, n_f),
        in_specs=[
            pl.BlockSpec(memory_space=pl.ANY),
            pl.BlockSpec((1, D, tf), lambda i, f, tok, ex, nu: (ex[i], 0, ftile(i, f, nu))),
            pl.BlockSpec((1, D, tf), lambda i, f, tok, ex, nu: (ex[i], 0, ftile(i, f, nu))),
            pl.BlockSpec((1, tf, D), lambda i, f, tok, ex, nu: (ex[i], ftile(i, f, nu), 0)),
        ],
        out_specs=pl.BlockSpec((rows, D), lambda i, f, tok, ex, nu: (i, 0)),
        scratch_shapes=[pltpu.VMEM((2, rows, D), F32), pltpu.VMEM((rows, D), BF16), pltpu.SemaphoreType.DMA((2,))],
    )
    return pl.pallas_call(
        _moe_expert_kernel,
        grid_spec=grid_spec,
        out_shape=jax.ShapeDtypeStruct((n_blk * rows, D), F32),
        compiler_params=_cparams(("arbitrary", "arbitrary")),
        name="moe_experts",
    )(tok, blk_exp, n_used, u, wg, wu, wd)


def _moe_combine_kernel(pos_ref, h_ref, w_ref, gfin_ref, y_hbm, o_ref, r_ref, sem):
    i = pl.program_id(0)
    n_steps = pl.num_programs(0)
    tm = h_ref.shape[0]
    slot = i & 1

    def gather(step, into):
        for k in range(TOP_K):
            _start_row_gather(y_hbm, r_ref.at[into, k], sem.at[into, k], tm,
                              lambda r: pos_ref[(step * tm + r) * TOP_K + k])

    @pl.when(i == 0)
    def _():
        gather(0, 0)

    for k in range(TOP_K):
        _wait_row_gather(y_hbm, r_ref.at[slot, k], sem.at[slot, k])

    @pl.when(i + 1 < n_steps)
    def _():
        gather(i + 1, 1 - slot)

    w = w_ref[...]
    y = h_ref[...] + (r_ref[slot, 0] * w[:, 0:1] + r_ref[slot, 1] * w[:, 1:2])
    o_ref[...] = _rms_rows(y, gfin_ref[...])


def _moe_combine(pos, h, w, g_final, y, tm):
    S, D = h.shape
    grid_spec = pltpu.PrefetchScalarGridSpec(
        num_scalar_prefetch=1,
        grid=(S // tm,),
        in_specs=[
            pl.BlockSpec((tm, D), lambda i, pos: (i, 0)),
            pl.BlockSpec((tm, LANES), lambda i, pos: (i, 0)),
            pl.BlockSpec((1, D), lambda i, pos: (0, 0)),
            pl.BlockSpec(memory_space=pl.ANY),
        ],
        out_specs=pl.BlockSpec((tm, D), lambda i, pos: (i, 0)),
        scratch_shapes=[pltpu.VMEM((2, TOP_K, tm, D), F32), pltpu.SemaphoreType.DMA((2, TOP_K))],
    )
    return pl.pallas_call(
        _moe_combine_kernel,
        grid_spec=grid_spec,
        out_shape=jax.ShapeDtypeStruct((S, D), F32),
        compiler_params=_cparams(("arbitrary",)),
        name="moe_combine_norm",
    )(pos, h, w, g_final, y)


def _slot_scatter_kernel(dest_ref, tok_ref):
    def zero(i, c):
        tok_ref[i] = 0
        return c

    lax.fori_loop(0, tok_ref.shape[0], zero, 0, unroll=GATHER_UNROLL)

    shift = TOP_K.bit_length() - 1
    assert 1 << shift == TOP_K

    def put(s, c):
        tok_ref[dest_ref[s]] = lax.shift_right_logical(s, shift)
        return c

    lax.fori_loop(0, dest_ref.shape[0], put, 0, unroll=GATHER_UNROLL)


def _slot_scatter(dest, cap):
    return pl.pallas_call(
        _slot_scatter_kernel,
        in_specs=[pl.BlockSpec(memory_space=pltpu.SMEM)],
        out_specs=pl.BlockSpec(memory_space=pltpu.SMEM),
        out_shape=jax.ShapeDtypeStruct((cap,), jnp.int32),
        name="moe_slot_scatter",
    )(dest)


def _moe_dispatch_indices(top_idx):
    N = top_idx.shape[0]
    e_flat = top_idx.reshape(-1)
    onehot = (e_flat[:, None] == jnp.arange(N_EXPERTS, dtype=jnp.int32)[None, :]).astype(jnp.int32)
    csum = jnp.cumsum(onehot, axis=0)
    counts = csum[-1]
    rank = jnp.take_along_axis(csum, e_flat[:, None], axis=1)[:, 0] - 1
    padded = (counts + MOE_BLOCK - 1) // MOE_BLOCK * MOE_BLOCK
    pend = jnp.cumsum(padded)
    poff = pend - padded
    dest = poff[e_flat] + rank
    n_blk = (N * TOP_K + MOE_BLOCK - 1) // MOE_BLOCK + N_EXPERTS
    cap = n_blk * MOE_BLOCK
    tok_buf = _slot_scatter(dest.astype(jnp.int32), cap)
    blk_start = jnp.arange(n_blk, dtype=jnp.int32) * MOE_BLOCK
    blk_exp = jnp.minimum(jnp.sum(pend[None, :] <= blk_start[:, None], axis=1), N_EXPERTS - 1).astype(jnp.int32)
    n_used = (pend[-1] // MOE_BLOCK).astype(jnp.int32).reshape(1)
    return tok_buf, blk_exp, n_used, dest.astype(jnp.int32)


def _rope_tables(pos):
    half = HEAD_DIM // 2
    inv = ROPE_THETA ** (-np.arange(half, dtype=np.float64) / half)
    ang = np.asarray(pos, np.float64)[:, None] * inv[None, :]
    cos, sin = np.cos(ang), np.sin(ang)
    return (jnp.asarray(np.concatenate([cos, cos], axis=1), F32),
            jnp.asarray(np.concatenate([-sin, sin], axis=1), F32))


def _gate_params(w_gates, gate_b):
    D = w_gates.shape[0]
    w = w_gates.reshape(D, 3, N_KV_GROUPS, HEADS_PER_GROUP).transpose(0, 2, 1, 3).reshape(D, N_KV_GROUPS, 3 * HEADS_PER_GROUP)
    w = jnp.pad(w, ((0, 0), (0, 0), (0, LANES - 3 * HEADS_PER_GROUP))).reshape(D, N_KV_GROUPS * LANES)
    b = gate_b.reshape(3, N_KV_GROUPS, HEADS_PER_GROUP).transpose(1, 0, 2).reshape(N_KV_GROUPS, 3 * HEADS_PER_GROUP)
    b = jnp.pad(b, ((0, 0), (0, LANES - 3 * HEADS_PER_GROUP))).reshape(1, N_KV_GROUPS * LANES)
    return w.astype(BF16), b


def _chunked_groups(t):
    S = t.shape[0]
    return (t.reshape(S // CMP_STRIDE, CMP_STRIDE, N_KV_GROUPS, HEAD_DIM)
            .transpose(2, 0, 1, 3).reshape(N_KV_GROUPS, S // CMP_STRIDE, CMP_STRIDE * HEAD_DIM))


def _row_tile(S, want):
    t = min(S, want)
    assert S % t == 0
    return t


def kernel(x, mem, norm_mix, norm_mem_q, norm_mem_kv, norm_ffn, norm_final, nsa_w_in, nsa_gate_b, nsa_pe_k, nsa_pe_v, nsa_cmp_k_w1, nsa_cmp_k_w2, nsa_cmp_v_w1, nsa_cmp_v_w2, nsa_w_out, pool_w, pool_b, pool_scale, mem_wq, mem_wk, mem_wv, mem_wo, ffn_w_gate, ffn_w_up, ffn_w_down, moe_router, moe_w_gate, moe_w_up, moe_w_down):
    B, S, D = x.shape
    assert B == 1 and S % WINDOW == 0 and S >= SLC_BLOCK * N_SLC
    h = x.reshape(S, D)
    memf = mem.reshape(mem.shape[1], D)
    row = lambda v: v.reshape(1, -1)

    n_main = N_HEADS * HEAD_DIM + 6 * KV_WIDTH
    w_in = nsa_w_in[0]
    w_gate, b_gate = _gate_params(w_in[:, n_main:], nsa_gate_b[0])
    cos, sin = _rope_tables(np.arange(S))
    proj, gates = _nsa_proj(h, row(norm_mix[0]), w_in[:, :n_main].astype(BF16), w_gate, b_gate, cos, sin,
                            _row_tile(S, 1024))
    q_w = N_HEADS * HEAD_DIM
    n_chunks = S // CMP_STRIDE
    cmp_pos = np.arange(n_chunks) * CMP_STRIDE + CMP_BLOCK - 1
    ccos, csin = _rope_tables(cmp_pos)
    kc = _compress(_chunked_groups(proj[:, q_w:q_w + KV_WIDTH]), nsa_pe_k[0].reshape(1, -1),
                   nsa_cmp_k_w1[0].astype(BF16), nsa_cmp_k_w2[0].astype(BF16), ccos, csin, True)
    vct = _compress(_chunked_groups(proj[:, q_w + KV_WIDTH:q_w + 2 * KV_WIDTH]), nsa_pe_v[0].reshape(1, -1),
                    nsa_cmp_v_w1[0].astype(BF16), nsa_cmp_v_w2[0].astype(BF16), ccos, csin, False).transpose(0, 2, 1)
    attn = _nsa_attention(proj, gates, kc, vct)

    def mem_layer(h, i, **fused):
        k, v = _mem_kv(memf, row(norm_mem_kv[i]), mem_wk[i].astype(BF16), mem_wv[i].astype(BF16))
        return _mem_attn(h, row(norm_mem_q[i]), mem_wq[i].astype(BF16), k, v, mem_wo[i].astype(BF16),
                         _row_tile(S, 512), **fused)

    h = mem_layer(h, 0, pre_proj=(attn, nsa_w_out[0].astype(BF16)))
    h = _ffn(h, row(norm_ffn[0]), ffn_w_gate[0].astype(BF16), ffn_w_up[0].astype(BF16),
             ffn_w_down[0].astype(BF16), _row_tile(S, 1024), 512)

    h = _pool_mixer(h, row(norm_mix[1]), pool_w[0].astype(BF16), row(pool_b[0]), row(pool_scale[0]),
                    _row_tile(S, 512))
    h = mem_layer(h, 1)
    r = jnp.pad(moe_router[0], ((0, 0), (0, LANES - N_EXPERTS)))
    r_hi = r.astype(BF16)
    r_lo = (r - r_hi.astype(F32)).astype(BF16)
    u, idx, gate_w = _router(h, row(norm_ffn[1]), r_hi, r_lo, _row_tile(S, 512))
    tok_buf, blk_exp, n_used, dest = _moe_dispatch_indices(idx[:, :TOP_K])
    y = _moe_experts(tok_buf, blk_exp, n_used, u, moe_w_gate[0], moe_w_up[0], moe_w_down[0], 512)
    out = _moe_combine(dest, h, gate_w, row(norm_final), y, _row_tile(S, 256))
    return out.reshape(B, S, D)
```

```python
import functools

import numpy as np
import jax
import jax.numpy as jnp
from jax import lax
from jax.experimental import pallas as pl
from jax.experimental.pallas import tpu as pltpu

F32 = jnp.float32
BF16 = jnp.bfloat16

N_HEADS = 16
HEAD_DIM = 128
N_KV_GROUPS = 4
HEADS_PER_GROUP = 4
KV_WIDTH = N_KV_GROUPS * HEAD_DIM
CMP_BLOCK = 32
CMP_STRIDE = 16
SLC_BLOCK = 64
RATIO = SLC_BLOCK // CMP_STRIDE
N_SLC = 16
WINDOW = 512
Q_BLOCK = 128
ROPE_THETA = 10000.0
POOL_WINDOWS = (2, 4, 8, 16)
POOL_GROUP = 512
MEM_HEADS = 4
MEM_HEAD_DIM = 128
N_EXPERTS = 8
TOP_K = 2
MOE_BLOCK = 512
EPS = 1e-6
NEG_INF = -1e30
FORCE = 1e9
REMOVED = -3e38
LOG2E = float(np.log2(np.e))

LANES = 128
VMEM_LIMIT = 56 * 1024 * 1024

SEL_KEY_TILE = 512


def _cparams(sem):
    return pltpu.CompilerParams(dimension_semantics=sem, vmem_limit_bytes=VMEM_LIMIT)


def _rms_rows(xf, g):
    r = lax.rsqrt(jnp.mean(xf * xf, axis=-1, keepdims=True) + EPS)
    return (xf * r) * g


def _dot(a, b):
    return jnp.dot(a, b, preferred_element_type=F32)


def _dot_t(a, b):
    return lax.dot_general(a, b, (((1,), (1,)), ((), ())), preferred_element_type=F32)


def _proj_kernel(x_ref, g_ref, w_ref, wgate_ref, bgate_ref, cos_ref, sin_ref, o_ref, gates_ref, u_ref):
    j = pl.program_id(1)

    @pl.when(j == 0)
    def _():
        ub = _rms_rows(x_ref[...], g_ref[...]).astype(BF16)
        u_ref[...] = ub
        gates_ref[...] = jax.nn.sigmoid(_dot(ub, wgate_ref[...]) + bgate_ref[...])

    acc = _dot(u_ref[...], w_ref[...])
    n_q_tiles = N_HEADS * HEAD_DIM // KV_WIDTH
    is_q = j < n_q_tiles
    is_rope = is_q | (j == n_q_tiles + 2) | (j == n_q_tiles + 4)

    @pl.when(is_rope)
    def _():
        scale = jnp.where(is_q, HEAD_DIM ** -0.5 * LOG2E, 1.0).astype(F32)
        cos = cos_ref[...]
        sin = sin_ref[...]
        for h in range(KV_WIDTH // HEAD_DIM):
            xh = acc[:, h * HEAD_DIM:(h + 1) * HEAD_DIM]
            rot = pltpu.roll(xh, HEAD_DIM // 2, axis=1)
            o_ref[:, h * HEAD_DIM:(h + 1) * HEAD_DIM] = ((xh * cos + rot * sin) * scale).astype(BF16)

    @pl.when(jnp.logical_not(is_rope))
    def _():
        o_ref[...] = acc.astype(BF16)


def _nsa_proj(x, g, w_main, w_gate, b_gate, cos, sin, tm):
    S, D = x.shape
    n_main = w_main.shape[1]
    tn = KV_WIDTH
    n_gate = w_gate.shape[1]
    return pl.pallas_call(
        _proj_kernel,
        grid=(S // tm, n_main // tn),
        in_specs=[
            pl.BlockSpec((tm, D), lambda i, j: (i, 0)),
            pl.BlockSpec((1, D), lambda i, j: (0, 0)),
            pl.BlockSpec((D, tn), lambda i, j: (0, j)),
            pl.BlockSpec((D, n_gate), lambda i, j: (0, 0)),
            pl.BlockSpec((1, n_gate), lambda i, j: (0, 0)),
            pl.BlockSpec((tm, HEAD_DIM), lambda i, j: (i, 0)),
            pl.BlockSpec((tm, HEAD_DIM), lambda i, j: (i, 0)),
        ],
        out_specs=[
            pl.BlockSpec((tm, tn), lambda i, j: (i, j)),
            pl.BlockSpec((tm, n_gate), lambda i, j: (i, 0)),
        ],
        out_shape=[
            jax.ShapeDtypeStruct((S, n_main), BF16),
            jax.ShapeDtypeStruct((S, n_gate), F32),
        ],
        scratch_shapes=[pltpu.VMEM((tm, D), BF16)],
        compiler_params=_cparams(("arbitrary", "arbitrary")),
        name="nsa_proj",
    )(x, g, w_main, w_gate, b_gate, cos, sin)


def _gelu_tanh(x):
    return 0.5 * x * (1.0 + jnp.tanh(np.sqrt(2.0 / np.pi).astype(np.float32) * (x + 0.044715 * (x * x * x))))


def _compress_kernel(t_ref, pe_ref, w1_ref, w2_ref, cos_ref, sin_ref, o_ref, *, apply_rope):
    half = CMP_STRIDE * HEAD_DIM
    t = t_ref[0].astype(F32)
    n = t.shape[0]
    a = _dot((t + pe_ref[:, :half]).astype(BF16), w1_ref[:half, :])
    b = _dot((t + pe_ref[:, half:]).astype(BF16), w1_ref[half:, :])
    hid = a + pltpu.roll(b, n - 1, axis=0)
    out = _dot(_gelu_tanh(hid).astype(BF16), w2_ref[...])
    if apply_rope:
        rot = pltpu.roll(out, HEAD_DIM // 2, axis=1)
        out = out * cos_ref[...] + rot * sin_ref[...]
    o_ref[0] = out.astype(BF16)


def _compress(t, pe_flat, w1, w2, cos, sin, apply_rope):
    G, n, width = t.shape
    hidden = w1.shape[1]
    out_block = (1, n, HEAD_DIM)
    return pl.pallas_call(
        functools.partial(_compress_kernel, apply_rope=apply_rope),
        grid=(G,),
        in_specs=[
            pl.BlockSpec((1, n, width), lambda g: (g, 0, 0)),
            pl.BlockSpec((1, 2 * width), lambda g: (0, 0)),
            pl.BlockSpec((2 * width, hidden), lambda g: (0, 0)),
            pl.BlockSpec((hidden, HEAD_DIM), lambda g: (0, 0)),
            pl.BlockSpec((n, HEAD_DIM), lambda g: (0, 0)),
            pl.BlockSpec((n, HEAD_DIM), lambda g: (0, 0)),
        ],
        out_specs=pl.BlockSpec(out_block, lambda g: (g, 0, 0)),
        out_shape=jax.ShapeDtypeStruct((G,) + out_block[1:], BF16),
        compiler_params=_cparams(("arbitrary",)),
        name="nsa_compress_k" if apply_rope else "nsa_compress_v",
    )(t, pe_flat, w1, w2, cos, sin)


def _softmax_cols(s):
    e = jnp.exp2(s - jnp.max(s, axis=0, keepdims=True))
    return e, jnp.sum(e, axis=0, keepdims=True)


def _tile_heads(x):
    return jnp.concatenate([x] * HEADS_PER_GROUP, axis=1)


SEL_PAD = 16
V_EXTRA_ROWS = 16


def _run_on_shortest_prefix(fn, total, needed, parts):
    piece = total // parts
    sizes = [piece * k for k in range(1, parts + 1)] if piece % LANES == 0 else [total]
    for idx, n in enumerate(sizes):
        cond = needed > (sizes[idx - 1] if idx else 0)
        if idx < len(sizes) - 1:
            cond = cond & (needed <= n)
        pl.when(cond)(functools.partial(fn, n))


def _nsa_attn_kernel(q_ref, gates_ref, kc_ref, vct_ref, ks_ref, vst_ref, wselt_ref, onehot_ref, *rest, n_sel):
    n_win = WINDOW // Q_BLOCK + 1
    kw_refs = rest[:n_win]
    vwt_refs = rest[n_win:2 * n_win]
    o_ref = rest[2 * n_win]
    gall_ref, s0_ref, s1_ref, p0_ref, p1_ref, acc_ref, oc_ref, sblk_ref = rest[2 * n_win + 1:]

    b = pl.program_id(1)
    t0 = b * Q_BLOCK
    Q = Q_BLOCK
    H = HEADS_PER_GROUP
    C = H * Q
    q = q_ref[...]
    qt = jnp.concatenate(
        [q[:, h * HEAD_DIM:(h + 1) * HEAD_DIM].astype(F32).T.astype(BF16) for h in range(H)], axis=1)

    n_cmp = kc_ref.shape[1]
    nsp = wselt_ref.shape[0]
    qi_row = lax.broadcasted_iota(jnp.int32, (1, C), 1) & (Q - 1)

    def compressed(n):
        qi_c = lax.broadcasted_iota(jnp.int32, (n, Q), 1)
        cpos = lax.broadcasted_iota(jnp.int32, (n, Q), 0) * CMP_STRIDE + (CMP_BLOCK - 1)
        bias_c = jnp.where(cpos <= t0 + qi_c, 0.0, NEG_INF)
        e_c, l_c = _softmax_cols(_dot(kc_ref[0, 0:n, :], qt) + _tile_heads(bias_c))
        p_c = e_c * jnp.where(t0 + qi_row >= CMP_BLOCK - 1, 1.0 / l_c, 0.0)
        oc_ref[...] = _dot(vct_ref[0, :, 0:n], p_c.astype(BF16))
        imp = (p_c[:, 0:Q] + p_c[:, Q:2 * Q]) + p_c[:, 2 * Q:3 * Q] + p_c[:, 3 * Q:4 * Q]
        imp_hi = imp.astype(BF16)
        imp_lo = (imp - imp_hi.astype(F32)).astype(BF16)
        wselt = wselt_ref[:, 0:n]
        sblk_ref[...] = _dot(wselt, imp_hi) + _dot(wselt, imp_lo)

    _run_on_shortest_prefix(compressed, n_cmp, (t0 + Q - CMP_BLOCK) // CMP_STRIDE + 1, 4)
    o_c = oc_ref[...]

    def select(rows):
        blk = lax.broadcasted_iota(jnp.int32, (rows, Q), 0)
        qp = t0 + lax.broadcasted_iota(jnp.int32, (rows, Q), 1)
        cur = qp >> 6
        valid = blk * SLC_BLOCK <= qp
        forced = (blk == 0) | (blk == cur) | (blk == cur - 1)
        s_blk = jnp.where(forced, FORCE, jnp.where(valid, sblk_ref[0:rows, :], NEG_INF))
        blk_f = blk.astype(F32)

        def pick(_, carry):
            s, sel = carry
            m = jnp.max(s, axis=0, keepdims=True)
            first = jnp.min(jnp.where(s == m, blk_f, float(rows)), axis=0, keepdims=True)
            hit = blk_f == first
            return jnp.where(hit, REMOVED, s), jnp.where(hit, 1.0, sel)

        _, sel = lax.fori_loop(0, n_sel, pick, (s_blk, jnp.zeros((rows, Q), F32)))
        gall_ref[SEL_PAD:SEL_PAD + rows, :] = _tile_heads(jnp.where((sel > 0.5) & valid, 0.0, NEG_INF))
        if rows < nsp:
            gall_ref[SEL_PAD + rows:SEL_PAD + nsp, :] = jnp.full((nsp - rows, C), NEG_INF, F32)

    _run_on_shortest_prefix(select, nsp, (t0 + Q - 1) // SLC_BLOCK + 1, 2)

    T = SEL_KEY_TILE
    bpt = T // SLC_BLOCK
    n_tiles = (t0 + Q + T - 1) // T
    gall_ref[0:SEL_PAD, :] = jnp.full((SEL_PAD, C), NEG_INF, F32)
    gall_ref[SEL_PAD + nsp:, :] = jnp.full((SEL_PAD, C), NEG_INF, F32)
    onehot = onehot_ref[(b + 1) & 3]
    zero_rows = jnp.zeros((HEAD_DIM - 2 * bpt, C), BF16)
    k_rows = ks_ref.shape[1]
    v_blocks = vst_ref.shape[1]

    def scores(kt):
        start = pl.multiple_of(jnp.clip(t0 - kt * T, 0, k_rows - T), Q)
        first_blk = 2 * b + 2 - bpt * (kt + 1) + SEL_PAD
        w0 = pl.multiple_of(jnp.maximum((first_blk >> 3) << 3, 0), 8)
        table = gall_ref[pl.ds(w0, 2 * bpt), :].astype(BF16)
        lhs = jnp.concatenate([ks_ref[0, pl.ds(start, T), :], onehot], axis=1)
        rhs = jnp.concatenate([qt, table, zero_rows], axis=0)
        return _dot(lhs, rhs)

    def values(kt):
        vb = jnp.clip(b - (T // Q) * kt, 0, v_blocks - T // Q)
        return jnp.concatenate([vst_ref[0, vb + c] for c in range(T // Q)], axis=1)

    def tile_step(kt, s_cur, s_nxt, p_cur, p_prv, carry):
        alpha_prev, m = carry
        acc_ref[...] = alpha_prev * acc_ref[...] + _dot(values(kt - 1), p_prv[...])
        s_nxt[...] = scores(kt + 1)
        s = s_cur[...]
        m_new = jnp.maximum(m, jnp.max(s, axis=0, keepdims=True))
        p_cur[...] = jnp.exp2((s - m_new).astype(BF16))
        return jnp.exp2(m - m_new), m_new

    def pair_step(j, carry):
        carry = tile_step(2 * j, s0_ref, s1_ref, p0_ref, p1_ref, carry)
        return tile_step(2 * j + 1, s1_ref, s0_ref, p1_ref, p0_ref, carry)

    s_first = scores(0)
    key_j = lax.broadcasted_iota(jnp.int32, (Q, Q), 0)
    qry_i = lax.broadcasted_iota(jnp.int32, (Q, Q), 1)
    s0_ref[0:T - Q, :] = s_first[0:T - Q]
    s0_ref[T - Q:, :] = s_first[T - Q:] + _tile_heads(jnp.where(key_j <= qry_i, 0.0, NEG_INF))
    p1_ref[...] = jnp.zeros_like(p1_ref)
    acc_ref[...] = jnp.zeros_like(acc_ref)
    n_pairs = (n_tiles + 1) // 2
    alpha_last, _ = lax.fori_loop(
        0, n_pairs, pair_step, (jnp.ones((1, C), F32), jnp.full((1, C), NEG_INF, F32)))
    acc_s = alpha_last * acc_ref[...] + _dot(values(2 * n_pairs - 1), p1_ref[...])
    o_s = acc_s[0:HEAD_DIM] * (1.0 / acc_s[HEAD_DIM:HEAD_DIM + 1])

    kw = jnp.concatenate([r[...] for r in kw_refs], axis=0)
    vwt = jnp.concatenate([r[0] for r in vwt_refs], axis=1)
    nw = WINDOW + Q
    row_w = lax.broadcasted_iota(jnp.int32, (nw, Q), 0)
    diff = lax.broadcasted_iota(jnp.int32, (nw, Q), 1) - row_w + WINDOW
    ok_w = (diff >= 0) & (diff < WINDOW) & (row_w >= WINDOW - t0)
    s_w = _dot(kw, qt) + _tile_heads(jnp.where(ok_w, 0.0, NEG_INF))
    e_w = jnp.exp2((s_w - jnp.max(s_w, axis=0, keepdims=True)).astype(BF16))
    acc_w = _dot(vwt, e_w)
    o_w = acc_w[0:HEAD_DIM] * (1.0 / acc_w[HEAD_DIM:HEAD_DIM + 1])

    gt = gates_ref[...].T
    for h in range(H):
        cols = slice(h * Q, (h + 1) * Q)
        o = (gt[h:h + 1] * o_c[:, cols] + gt[H + h:H + h + 1] * o_s[:, cols]
             + gt[2 * H + h:2 * H + h + 1] * o_w[:, cols])
        o_ref[:, h * HEAD_DIM:(h + 1) * HEAD_DIM] = o.T.astype(BF16)


def _selection_matrix_t(n_cmp, nsp):
    j = np.arange(nsp)[:, None]
    c = np.arange(n_cmp)[None, :]
    d = c - RATIO * j
    w = np.where((d == -1) | (d == RATIO - 1), 1.0, np.where((d >= 0) & (d < RATIO - 1), 2.0, 0.0))
    return jnp.asarray(w, BF16)


def _nsa_attention(proj, gates, kc, vct):
    S = proj.shape[0]
    G = N_KV_GROUPS
    T = SEL_KEY_TILE
    nb = S // Q_BLOCK
    n_cmp = kc.shape[1]
    ns = S // SLC_BLOCK
    nsp = -(-ns // LANES) * LANES
    n_sel = min(N_SLC, ns)
    wselt = _selection_matrix_t(n_cmp, nsp)
    n_win = WINDOW // Q_BLOCK + 1
    cpb = KV_WIDTH // HEAD_DIM
    q_w = N_HEADS * HEAD_DIM
    q_cols = q_w // HEAD_DIM
    kw_col = q_cols + 4 * cpb
    pad = T - Q_BLOCK
    ksp = jnp.pad(proj[:, q_w + 2 * KV_WIDTH:q_w + 3 * KV_WIDTH], ((pad, 0), (0, 0)))
    ksp = ksp.reshape(S + pad, G, HEAD_DIM).transpose(1, 0, 2)
    vsp = jnp.pad(proj[:, q_w + 3 * KV_WIDTH:q_w + 4 * KV_WIDTH], ((pad, 0), (0, 0)))
    vst = vsp.reshape((S + pad) // Q_BLOCK, Q_BLOCK, G, HEAD_DIM).transpose(2, 0, 3, 1)
    extra = jnp.zeros(vst.shape[:2] + (V_EXTRA_ROWS, Q_BLOCK), BF16).at[:, :, 0, :].set(1.0)
    vst = jnp.concatenate([vst, extra], axis=2)
    vwt = proj[:, q_w + 5 * KV_WIDTH:q_w + 6 * KV_WIDTH].reshape(S, G, HEAD_DIM).transpose(1, 2, 0)
    vwt = jnp.concatenate([vwt, jnp.zeros((G, V_EXTRA_ROWS, S), BF16).at[:, 0, :].set(1.0)], axis=1)
    r = np.arange(T)[None, :, None] // SLC_BLOCK
    onehot = jnp.asarray(np.arange(LANES)[None, None, :] == 2 * np.arange(4)[:, None, None] + r, BF16)

    def win_block(b, i):
        return jnp.maximum(b - (n_win - 1) + i, 0)

    in_specs = [
        pl.BlockSpec((Q_BLOCK, KV_WIDTH), lambda g, b: (b, g)),
        pl.BlockSpec((Q_BLOCK, LANES), lambda g, b: (b, g)),
        pl.BlockSpec((1, n_cmp, HEAD_DIM), lambda g, b: (g, 0, 0)),
        pl.BlockSpec((1, HEAD_DIM, n_cmp), lambda g, b: (g, 0, 0)),
        pl.BlockSpec((1,) + ksp.shape[1:], lambda g, b: (g, 0, 0)),
        pl.BlockSpec((1,) + vst.shape[1:], lambda g, b: (g, 0, 0, 0)),
        pl.BlockSpec(wselt.shape, lambda g, b: (0, 0)),
        pl.BlockSpec(onehot.shape, lambda g, b: (0, 0, 0)),
    ]
    in_specs += [pl.BlockSpec((Q_BLOCK, HEAD_DIM), functools.partial(lambda g, b, i: (win_block(b, i), kw_col + g), i=i))
                 for i in range(n_win)]
    in_specs += [pl.BlockSpec((1, vwt.shape[1], Q_BLOCK), functools.partial(lambda g, b, i: (g, 0, win_block(b, i)), i=i))
                 for i in range(n_win)]
    args = [proj, gates, kc, vct, ksp, vst, wselt, onehot] + [proj] * n_win + [vwt] * n_win
    return pl.pallas_call(
        functools.partial(_nsa_attn_kernel, n_sel=n_sel),
        grid=(G, nb),
        in_specs=in_specs,
        out_specs=pl.BlockSpec((Q_BLOCK, KV_WIDTH), lambda g, b: (b, g)),
        out_shape=jax.ShapeDtypeStruct((S, N_HEADS * HEAD_DIM), BF16),
        scratch_shapes=[pltpu.VMEM((nsp + 2 * SEL_PAD, KV_WIDTH), F32),
                        pltpu.VMEM((T, KV_WIDTH), F32), pltpu.VMEM((T, KV_WIDTH), F32),
                        pltpu.VMEM((T, KV_WIDTH), BF16), pltpu.VMEM((T, KV_WIDTH), BF16),
                        pltpu.VMEM((HEAD_DIM + V_EXTRA_ROWS, KV_WIDTH), F32),
                        pltpu.VMEM((HEAD_DIM, KV_WIDTH), F32), pltpu.VMEM((nsp, Q_BLOCK), F32)],
        compiler_params=_cparams(("arbitrary", "arbitrary")),
        name="nsa_attention",
    )(*args)


def _const_spec(shape):
    return pl.BlockSpec(shape, lambda i: (0,) * len(shape), pipeline_mode=pl.Buffered(1))


def _mem_kv_kernel(mem_ref, g_ref, wk_ref, wv_ref, k_ref, v_ref):
    mb = _rms_rows(mem_ref[...], g_ref[...]).astype(BF16)
    k_ref[...] = _dot(mb, wk_ref[...]).astype(BF16)
    v_ref[...] = _dot(mb, wv_ref[...]).astype(BF16)


def _mem_kv(mem, g, wk, wv):
    M, D = mem.shape
    W = wk.shape[1]
    full = lambda shape: pl.BlockSpec(shape, lambda i: (0, 0))
    return pl.pallas_call(
        _mem_kv_kernel,
        grid=(1,),
        in_specs=[full((M, D)), full((1, D)), full((D, W)), full((D, W))],
        out_specs=[full((M, W)), full((M, W))],
        out_shape=[jax.ShapeDtypeStruct((M, W), BF16)] * 2,
        compiler_params=_cparams(("arbitrary",)),
        name="mem_kv",
    )(mem, g, wk, wv)


def _route(h, g_ref, rhi_ref, rlo_ref, u_ref, idx_ref, w_ref):
    u = _rms_rows(h, g_ref[...])
    u_ref[...] = u
    u_hi = u.astype(BF16)
    u_lo = (u - u_hi.astype(F32)).astype(BF16)
    rhi = rhi_ref[...]
    logits = (_dot(u_hi, rhi) + _dot(u_lo, rhi)) + _dot(u_hi, rlo_ref[...])
    lane = lax.broadcasted_iota(jnp.int32, logits.shape, 1)
    lane_f = lane.astype(F32)
    s = jnp.where(lane < N_EXPERTS, logits, REMOVED)
    m1 = jnp.max(s, axis=1, keepdims=True)
    i1 = jnp.min(jnp.where(s == m1, lane_f, float(LANES)), axis=1, keepdims=True)
    s = jnp.where(lane_f == i1, REMOVED, s)
    m2 = jnp.max(s, axis=1, keepdims=True)
    i2 = jnp.min(jnp.where(s == m2, lane_f, float(LANES)), axis=1, keepdims=True)
    e2 = jnp.exp(m2 - m1)
    den = 1.0 + e2
    idx_ref[...] = jnp.where(lane == 0, i1, jnp.where(lane == 1, i2, 0.0)).astype(jnp.int32)
    w_ref[...] = jnp.where(lane == 0, 1.0 / den, jnp.where(lane == 1, e2 / den, 0.0))


def _router_kernel(h_ref, g_ref, rhi_ref, rlo_ref, u_ref, idx_ref, w_ref):
    _route(h_ref[...], g_ref, rhi_ref, rlo_ref, u_ref, idx_ref, w_ref)


def _router(h, g, r_hi, r_lo, tm):
    S, D = h.shape
    rows = lambda width: pl.BlockSpec((tm, width), lambda i: (i, 0))
    return pl.pallas_call(
        _router_kernel,
        grid=(S // tm,),
        in_specs=[rows(D), _const_spec(g.shape), _const_spec(r_hi.shape), _const_spec(r_lo.shape)],
        out_specs=[rows(D), rows(LANES), rows(LANES)],
        out_shape=[jax.ShapeDtypeStruct((S, D), F32), jax.ShapeDtypeStruct((S, LANES), jnp.int32),
                   jax.ShapeDtypeStruct((S, LANES), F32)],
        compiler_params=_cparams(("arbitrary",)),
        name="moe_router",
    )(h, g, r_hi, r_lo)


def _mem_attn_kernel(*refs, pre_proj):
    it = iter(refs)
    if pre_proj:
        a_ref, wpre_ref = next(it), next(it)
    res_ref, g_ref, wq_ref, k_ref, v_ref, wo_ref, o_ref = it

    h = res_ref[...]
    if pre_proj:
        h = h + _dot(a_ref[...], wpre_ref[...])
    ub = _rms_rows(h, g_ref[...]).astype(BF16)
    q = (_dot(ub, wq_ref[...]) * (MEM_HEAD_DIM ** -0.5)).astype(BF16)
    outs = []
    for hd in range(MEM_HEADS):
        cols = slice(hd * MEM_HEAD_DIM, (hd + 1) * MEM_HEAD_DIM)
        s = _dot_t(q[:, cols], k_ref[:, cols])
        e = jnp.exp(s - jnp.max(s, axis=1, keepdims=True))
        p = e / jnp.sum(e, axis=1, keepdims=True)
        outs.append(_dot(p.astype(BF16), v_ref[:, cols]).astype(BF16))
    o_ref[...] = h + _dot(jnp.concatenate(outs, axis=1), wo_ref[...])


def _mem_attn(res, g, wq, k, v, wo, tm, pre_proj=None):
    S, D = res.shape
    rows = lambda width: pl.BlockSpec((tm, width), lambda i: (i, 0))
    args, in_specs = [], []
    if pre_proj is not None:
        a, w_pre = pre_proj
        args += [a, w_pre]
        in_specs += [rows(a.shape[1]), _const_spec(w_pre.shape)]
    args += [res, g, wq, k, v, wo]
    in_specs += [rows(D)] + [_const_spec(x.shape) for x in (g, wq, k, v, wo)]
    return pl.pallas_call(
        functools.partial(_mem_attn_kernel, pre_proj=pre_proj is not None),
        grid=(S // tm,),
        in_specs=in_specs,
        out_specs=rows(D),
        out_shape=jax.ShapeDtypeStruct((S, D), F32),
        compiler_params=_cparams(("arbitrary",)),
        name="mem_attention",
    )(*args)


def _swiglu_tile(ub, wg, wu, wd):
    gate = _dot(ub, wg)
    up = _dot(ub, wu)
    act = (gate * jax.nn.sigmoid(gate)) * up
    return _dot(act.astype(BF16), wd)


def _ffn_kernel(h_ref, g_ref, wg_ref, wu_ref, wd_ref, o_ref, u_ref):
    f = pl.program_id(1)

    @pl.when(f == 0)
    def _():
        h = h_ref[...]
        u_ref[...] = _rms_rows(h, g_ref[...]).astype(BF16)
        o_ref[...] = h

    o_ref[...] += _swiglu_tile(u_ref[...], wg_ref[...], wu_ref[...], wd_ref[...])


def _ffn(h, g, wg, wu, wd, tm, tf):
    S, D = h.shape
    F = wg.shape[1]
    return pl.pallas_call(
        _ffn_kernel,
        grid=(S // tm, F // tf),
        in_specs=[
            pl.BlockSpec((tm, D), lambda i, f: (i, 0)),
            pl.BlockSpec((1, D), lambda i, f: (0, 0)),
            pl.BlockSpec((D, tf), lambda i, f: (0, f)),
            pl.BlockSpec((D, tf), lambda i, f: (0, f)),
            pl.BlockSpec((tf, D), lambda i, f: (f, 0)),
        ],
        out_specs=pl.BlockSpec((tm, D), lambda i, f: (i, 0)),
        out_shape=jax.ShapeDtypeStruct((S, D), F32),
        scratch_shapes=[pltpu.VMEM((tm, D), BF16)],
        compiler_params=_cparams(("arbitrary", "arbitrary")),
        name="dense_swiglu",
    )(h, g, wg, wu, wd)


POOL_HALO = 16


def _pool_kernel(h_ref, halo_ref, g_ref, w_ref, b_ref, scale_ref, o_ref, ext_ref):
    i = pl.program_id(0)
    tm = h_ref.shape[0]
    h = h_ref[...]
    g = g_ref[...]
    u = _rms_rows(h, g)
    halo = _rms_rows(halo_ref[...], g)
    ext_ref[0:POOL_HALO, :] = jnp.where(i > 0, halo, 0.0)
    ext_ref[POOL_HALO:, :] = u
    pos = i * tm + lax.broadcasted_iota(jnp.int32, (tm, 1), 0)
    for gi, w in enumerate(POOL_WINDOWS):
        cols = slice(gi * POOL_GROUP, (gi + 1) * POOL_GROUP)
        tot = u[:, cols]
        for k in range(1, w):
            tot = tot + ext_ref[POOL_HALO - k:POOL_HALO - k + tm, cols]
        cnt = jnp.minimum(pos + 1, w).astype(F32)
        d = tot / cnt - u[:, cols]
        z = _dot(d.astype(BF16), w_ref[gi]) + b_ref[:, cols]
        o_ref[:, cols] = h[:, cols] + z * scale_ref[:, cols]


def _pool_mixer(h, g, w, b, scale, tm):
    S, D = h.shape
    ratio = tm // POOL_HALO
    return pl.pallas_call(
        _pool_kernel,
        grid=(S // tm,),
        in_specs=[
            pl.BlockSpec((tm, D), lambda i: (i, 0)),
            pl.BlockSpec((POOL_HALO, D), lambda i: (jnp.maximum(i * ratio - 1, 0), 0)),
            pl.BlockSpec((1, D), lambda i: (0, 0)),
            pl.BlockSpec(w.shape, lambda i: (0, 0, 0)),
            pl.BlockSpec((1, D), lambda i: (0, 0)),
            pl.BlockSpec((1, D), lambda i: (0, 0)),
        ],
        out_specs=pl.BlockSpec((tm, D), lambda i: (i, 0)),
        out_shape=jax.ShapeDtypeStruct((S, D), F32),
        scratch_shapes=[pltpu.VMEM((tm + POOL_HALO, D), F32)],
        compiler_params=_cparams(("arbitrary",)),
        name="pool_mixer",
    )(h, h, g, w, b, scale)


GATHER_UNROLL = 8


def _start_row_gather(src_hbm, dst_vmem, sem, n_rows, src_row_of):
    def body(r, c):
        pltpu.make_async_copy(src_hbm.at[pl.ds(src_row_of(r), 1), :], dst_vmem.at[pl.ds(r, 1), :], sem).start()
        return c

    lax.fori_loop(0, n_rows, body, 0, unroll=GATHER_UNROLL)


def _wait_row_gather(src_hbm, dst_vmem, sem):
    pltpu.make_async_copy(src_hbm.at[pl.ds(0, dst_vmem.shape[0]), :], dst_vmem, sem).wait()


def _moe_expert_kernel(tok_ref, exp_ref, nused_ref, u_hbm, wg_ref, wu_ref, wd_ref, o_ref, x_ref, xb_ref, sem):
    i = pl.program_id(0)
    f = pl.program_id(1)
    rows = x_ref.shape[1]
    n_used = nused_ref[0]
    active = i < n_used
    slot = i & 1

    def gather(block, into):
        _start_row_gather(u_hbm, x_ref.at[into], sem.at[into], rows, lambda r: tok_ref[block * rows + r])

    @pl.when((i == 0) & (f == 0) & active)
    def _():
        gather(0, 0)

    @pl.when(active & (f == 0))
    def _():
        _wait_row_gather(u_hbm, x_ref.at[slot], sem.at[slot])
        xb_ref[...] = x_ref[slot].astype(BF16)
        o_ref[...] = jnp.zeros_like(o_ref)

        @pl.when(i + 1 < n_used)
        def _():
            gather(i + 1, 1 - slot)

    @pl.when(active)
    def _():
        o_ref[...] += _swiglu_tile(xb_ref[...], wg_ref[0].astype(BF16), wu_ref[0].astype(BF16), wd_ref[0])

    @pl.when(jnp.logical_not(active) & (f == 0))
    def _():
        o_ref[...] = jnp.zeros_like(o_ref)


def _moe_experts(tok, blk_exp, n_used, u, wg, wu, wd, tf):
    N, D = u.shape
    n_blk = blk_exp.shape[0]
    F = wg.shape[2]
    rows = MOE_BLOCK
    n_f = F // tf

    def ftile(i, f, nu):
        return jnp.where(i < nu[0], f, n_f - 1)

    grid_spec = pltpu.PrefetchScalarGridSpec(
        num_scalar_prefetch=3,
        grid=(n_blk, n_f),
        in_specs=[
            pl.BlockSpec(memory_space=pl.ANY),
            pl.BlockSpec((1, D, tf), lambda i, f, tok, ex, nu: (ex[i], 0, ftile(i, f, nu))),
            pl.BlockSpec((1, D, tf), lambda i, f, tok, ex, nu: (ex[i], 0, ftile(i, f, nu))),
            pl.BlockSpec((1, tf, D), lambda i, f, tok, ex, nu: (ex[i], ftile(i, f, nu), 0)),
        ],
        out_specs=pl.BlockSpec((rows, D), lambda i, f, tok, ex, nu: (i, 0)),
        scratch_shapes=[pltpu.VMEM((2, rows, D), F32), pltpu.VMEM((rows, D), BF16), pltpu.SemaphoreType.DMA((2,))],
    )
    return pl.pallas_call(
        _moe_expert_kernel,
        grid_spec=grid_spec,
        out_shape=jax.ShapeDtypeStruct((n_blk * rows, D), F32),
        compiler_params=_cparams(("arbitrary", "arbitrary")),
        name="moe_experts",
    )(tok, blk_exp, n_used, u, wg, wu, wd)


def _moe_combine_kernel(pos_ref, h_ref, w_ref, gfin_ref, y_hbm, o_ref, r_ref, sem):
    i = pl.program_id(0)
    n_steps = pl.num_programs(0)
    tm = h_ref.shape[0]
    slot = i & 1

    def gather(step, into):
        for k in range(TOP_K):
            _start_row_gather(y_hbm, r_ref.at[into, k], sem.at[into, k], tm,
                              lambda r: pos_ref[(step * tm + r) * TOP_K + k])

    @pl.when(i == 0)
    def _():
        gather(0, 0)

    for k in range(TOP_K):
        _wait_row_gather(y_hbm, r_ref.at[slot, k], sem.at[slot, k])

    @pl.when(i + 1 < n_steps)
    def _():
        gather(i + 1, 1 - slot)

    w = w_ref[...]
    y = h_ref[...] + (r_ref[slot, 0] * w[:, 0:1] + r_ref[slot, 1] * w[:, 1:2])
    o_ref[...] = _rms_rows(y, gfin_ref[...])


def _moe_combine(pos, h, w, g_final, y, tm):
    S, D = h.shape
    grid_spec = pltpu.PrefetchScalarGridSpec(
        num_scalar_prefetch=1,
        grid=(S // tm,),
        in_specs=[
            pl.BlockSpec((tm, D), lambda i, pos: (i, 0)),
            pl.BlockSpec((tm, LANES), lambda i, pos: (i, 0)),
            pl.BlockSpec((1, D), lambda i, pos: (0, 0)),
            pl.BlockSpec(memory_space=pl.ANY),
        ],
        out_specs=pl.BlockSpec((tm, D), lambda i, pos: (i, 0)),
        scratch_shapes=[pltpu.VMEM((2, TOP_K, tm, D), F32), pltpu.SemaphoreType.DMA((2, TOP_K))],
    )
    return pl.pallas_call(
        _moe_combine_kernel,
        grid_spec=grid_spec,
        out_shape=jax.ShapeDtypeStruct((S, D), F32),
        compiler_params=_cparams(("arbitrary",)),
        name="moe_combine_norm",
    )(pos, h, w, g_final, y)


def _slot_scatter_kernel(dest_ref, tok_ref):
    def zero(i, c):
        tok_ref[i] = 0
        return c

    lax.fori_loop(0, tok_ref.shape[0], zero, 0, unroll=GATHER_UNROLL)

    shift = TOP_K.bit_length() - 1
    assert 1 << shift == TOP_K

    def put(s, c):
        tok_ref[dest_ref[s]] = lax.shift_right_logical(s, shift)
        return c

    lax.fori_loop(0, dest_ref.shape[0], put, 0, unroll=GATHER_UNROLL)


def _slot_scatter(dest, cap):
    return pl.pallas_call(
        _slot_scatter_kernel,
        in_specs=[pl.BlockSpec(memory_space=pltpu.SMEM)],
        out_specs=pl.BlockSpec(memory_space=pltpu.SMEM),
        out_shape=jax.ShapeDtypeStruct((cap,), jnp.int32),
        name="moe_slot_scatter",
    )(dest)


def _moe_dispatch_indices(top_idx):
    N = top_idx.shape[0]
    e_flat = top_idx.reshape(-1)
    onehot = (e_flat[:, None] == jnp.arange(N_EXPERTS, dtype=jnp.int32)[None, :]).astype(jnp.int32)
    csum = jnp.cumsum(onehot, axis=0)
    counts = csum[-1]
    rank = jnp.take_along_axis(csum, e_flat[:, None], axis=1)[:, 0] - 1
    padded = (counts + MOE_BLOCK - 1) // MOE_BLOCK * MOE_BLOCK
    pend = jnp.cumsum(padded)
    poff = pend - padded
    dest = poff[e_flat] + rank
    n_blk = (N * TOP_K + MOE_BLOCK - 1) // MOE_BLOCK + N_EXPERTS
    cap = n_blk * MOE_BLOCK
    tok_buf = _slot_scatter(dest.astype(jnp.int32), cap)
    blk_start = jnp.arange(n_blk, dtype=jnp.int32) * MOE_BLOCK
    blk_exp = jnp.minimum(jnp.sum(pend[None, :] <= blk_start[:, None], axis=1), N_EXPERTS - 1).astype(jnp.int32)
    n_used = (pend[-1] // MOE_BLOCK).astype(jnp.int32).reshape(1)
    return tok_buf, blk_exp, n_used, dest.astype(jnp.int32)


def _rope_tables(pos):
    half = HEAD_DIM // 2
    inv = ROPE_THETA ** (-np.arange(half, dtype=np.float64) / half)
    ang = np.asarray(pos, np.float64)[:, None] * inv[None, :]
    cos, sin = np.cos(ang), np.sin(ang)
    return (jnp.asarray(np.concatenate([cos, cos], axis=1), F32),
            jnp.asarray(np.concatenate([-sin, sin], axis=1), F32))


def _gate_params(w_gates, gate_b):
    D = w_gates.shape[0]
    w = w_gates.reshape(D, 3, N_KV_GROUPS, HEADS_PER_GROUP).transpose(0, 2, 1, 3).reshape(D, N_KV_GROUPS, 3 * HEADS_PER_GROUP)
    w = jnp.pad(w, ((0, 0), (0, 0), (0, LANES - 3 * HEADS_PER_GROUP))).reshape(D, N_KV_GROUPS * LANES)
    b = gate_b.reshape(3, N_KV_GROUPS, HEADS_PER_GROUP).transpose(1, 0, 2).reshape(N_KV_GROUPS, 3 * HEADS_PER_GROUP)
    b = jnp.pad(b, ((0, 0), (0, LANES - 3 * HEADS_PER_GROUP))).reshape(1, N_KV_GROUPS * LANES)
    return w.astype(BF16), b


def _chunked_groups(t):
    S = t.shape[0]
    return (t.reshape(S // CMP_STRIDE, CMP_STRIDE, N_KV_GROUPS, HEAD_DIM)
            .transpose(2, 0, 1, 3).reshape(N_KV_GROUPS, S // CMP_STRIDE, CMP_STRIDE * HEAD_DIM))


def _row_tile(S, want):
    t = min(S, want)
    assert S % t == 0
    return t


def kernel(x, mem, norm_mix, norm_mem_q, norm_mem_kv, norm_ffn, norm_final, nsa_w_in, nsa_gate_b, nsa_pe_k, nsa_pe_v, nsa_cmp_k_w1, nsa_cmp_k_w2, nsa_cmp_v_w1, nsa_cmp_v_w2, nsa_w_out, pool_w, pool_b, pool_scale, mem_wq, mem_wk, mem_wv, mem_wo, ffn_w_gate, ffn_w_up, ffn_w_down, moe_router, moe_w_gate, moe_w_up, moe_w_down):
    B, S, D = x.shape
    assert B == 1 and S % WINDOW == 0 and S >= SLC_BLOCK * N_SLC
    h = x.reshape(S, D)
    memf = mem.reshape(mem.shape[1], D)
    row = lambda v: v.reshape(1, -1)

    n_main = N_HEADS * HEAD_DIM + 6 * KV_WIDTH
    w_in = nsa_w_in[0]
    w_gate, b_gate = _gate_params(w_in[:, n_main:], nsa_gate_b[0])
    cos, sin = _rope_tables(np.arange(S))
    proj, gates = _nsa_proj(h, row(norm_mix[0]), w_in[:, :n_main].astype(BF16), w_gate, b_gate, cos, sin,
                            _row_tile(S, 1024))
    q_w = N_HEADS * HEAD_DIM
    n_chunks = S // CMP_STRIDE
    cmp_pos = np.arange(n_chunks) * CMP_STRIDE + CMP_BLOCK - 1
    ccos, csin = _rope_tables(cmp_pos)
    kc = _compress(_chunked_groups(proj[:, q_w:q_w + KV_WIDTH]), nsa_pe_k[0].reshape(1, -1),
                   nsa_cmp_k_w1[0].astype(BF16), nsa_cmp_k_w2[0].astype(BF16), ccos, csin, True)
    vct = _compress(_chunked_groups(proj[:, q_w + KV_WIDTH:q_w + 2 * KV_WIDTH]), nsa_pe_v[0].reshape(1, -1),
                    nsa_cmp_v_w1[0].astype(BF16), nsa_cmp_v_w2[0].astype(BF16), ccos, csin, False).transpose(0, 2, 1)
    attn = _nsa_attention(proj, gates, kc, vct)

    def mem_layer(h, i, **fused):
        k, v = _mem_kv(memf, row(norm_mem_kv[i]), mem_wk[i].astype(BF16), mem_wv[i].astype(BF16))
        return _mem_attn(h, row(norm_mem_q[i]), mem_wq[i].astype(BF16), k, v, mem_wo[i].astype(BF16),
                         _row_tile(S, 512), **fused)

    h = mem_layer(h, 0, pre_proj=(attn, nsa_w_out[0].astype(BF16)))
    h = _ffn(h, row(norm_ffn[0]), ffn_w_gate[0].astype(BF16), ffn_w_up[0].astype(BF16),
             ffn_w_down[0].astype(BF16), _row_tile(S, 1024), 512)

    h = _pool_mixer(h, row(norm_mix[1]), pool_w[0].astype(BF16), row(pool_b[0]), row(pool_scale[0]),
                    _row_tile(S, 512))
    h = mem_layer(h, 1)
    r = jnp.pad(moe_router[0], ((0, 0), (0, LANES - N_EXPERTS)))
    r_hi = r.astype(BF16)
    r_lo = (r - r_hi.astype(F32)).astype(BF16)
    u, idx, gate_w = _router(h, row(norm_ffn[1]), r_hi, r_lo, _row_tile(S, 512))
    tok_buf, blk_exp, n_used, dest = _moe_dispatch_indices(idx[:, :TOP_K])
    y = _moe_experts(tok_buf, blk_exp, n_used, u, moe_w_gate[0], moe_w_up[0], moe_w_down[0].astype(BF16), 512)
    out = _moe_combine(dest, h, gate_w, row(norm_final), y, _row_tile(S, 256))
    return out.reshape(B, S, D)
```

```python
import functools

import numpy as np
import jax
import jax.numpy as jnp
from jax import lax
from jax.experimental import pallas as pl
from jax.experimental.pallas import tpu as pltpu

F32 = jnp.float32
BF16 = jnp.bfloat16

N_HEADS = 16
HEAD_DIM = 128
N_KV_GROUPS = 4
HEADS_PER_GROUP = 4
KV_WIDTH = N_KV_GROUPS * HEAD_DIM
CMP_BLOCK = 32
CMP_STRIDE = 16
SLC_BLOCK = 64
RATIO = SLC_BLOCK // CMP_STRIDE
N_SLC = 16
WINDOW = 512
Q_BLOCK = 128
ROPE_THETA = 10000.0
POOL_WINDOWS = (2, 4, 8, 16)
POOL_GROUP = 512
MEM_HEADS = 4
MEM_HEAD_DIM = 128
N_EXPERTS = 8
TOP_K = 2
MOE_BLOCK = 512
EPS = 1e-6
NEG_INF = -1e30
REMOVED = -3e38
LOG2E = float(np.log2(np.e))

LANES = 128
VMEM_LIMIT = 56 * 1024 * 1024

SEL_KEY_TILE = 512


def _cparams(sem):
    return pltpu.CompilerParams(dimension_semantics=sem, vmem_limit_bytes=VMEM_LIMIT)


def _rms_rows(xf, g):
    r = lax.rsqrt(jnp.mean(xf * xf, axis=-1, keepdims=True) + EPS)
    return (xf * r) * g


def _dot(a, b):
    return jnp.dot(a, b, preferred_element_type=F32)


def _dot_t(a, b):
    return lax.dot_general(a, b, (((1,), (1,)), ((), ())), preferred_element_type=F32)


def _proj_kernel(x_ref, g_ref, w_ref, wgate_ref, bgate_ref, cos_ref, sin_ref, o_ref, gates_ref, u_ref):
    j = pl.program_id(1)

    @pl.when(j == 0)
    def _():
        ub = _rms_rows(x_ref[...], g_ref[...]).astype(BF16)
        u_ref[...] = ub
        gates_ref[...] = jax.nn.sigmoid(_dot(ub, wgate_ref[...]) + bgate_ref[...])

    acc = _dot(u_ref[...], w_ref[...])
    n_q_tiles = N_HEADS * HEAD_DIM // KV_WIDTH
    is_q = j < n_q_tiles
    is_rope = is_q | (j == n_q_tiles + 2) | (j == n_q_tiles + 4)

    @pl.when(is_rope)
    def _():
        scale = jnp.where(is_q, HEAD_DIM ** -0.5 * LOG2E, 1.0).astype(F32)
        cos = cos_ref[...]
        sin = sin_ref[...]
        for h in range(KV_WIDTH // HEAD_DIM):
            xh = acc[:, h * HEAD_DIM:(h + 1) * HEAD_DIM]
            rot = pltpu.roll(xh, HEAD_DIM // 2, axis=1)
            o_ref[:, h * HEAD_DIM:(h + 1) * HEAD_DIM] = ((xh * cos + rot * sin) * scale).astype(BF16)

    @pl.when(jnp.logical_not(is_rope))
    def _():
        o_ref[...] = acc.astype(BF16)


def _nsa_proj(x, g, w_main, w_gate, b_gate, cos, sin, tm):
    S, D = x.shape
    n_main = w_main.shape[1]
    tn = KV_WIDTH
    n_gate = w_gate.shape[1]
    return pl.pallas_call(
        _proj_kernel,
        grid=(S // tm, n_main // tn),
        in_specs=[
            pl.BlockSpec((tm, D), lambda i, j: (i, 0)),
            pl.BlockSpec((1, D), lambda i, j: (0, 0)),
            pl.BlockSpec((D, tn), lambda i, j: (0, j)),
            pl.BlockSpec((D, n_gate), lambda i, j: (0, 0)),
            pl.BlockSpec((1, n_gate), lambda i, j: (0, 0)),
            pl.BlockSpec((tm, HEAD_DIM), lambda i, j: (i, 0)),
            pl.BlockSpec((tm, HEAD_DIM), lambda i, j: (i, 0)),
        ],
        out_specs=[
            pl.BlockSpec((tm, tn), lambda i, j: (i, j)),
            pl.BlockSpec((tm, n_gate), lambda i, j: (i, 0)),
        ],
        out_shape=[
            jax.ShapeDtypeStruct((S, n_main), BF16),
            jax.ShapeDtypeStruct((S, n_gate), F32),
        ],
        scratch_shapes=[pltpu.VMEM((tm, D), BF16)],
        compiler_params=_cparams(("arbitrary", "arbitrary")),
        name="nsa_proj",
    )(x, g, w_main, w_gate, b_gate, cos, sin)


def _gelu_tanh(x):
    return 0.5 * x * (1.0 + jnp.tanh(np.sqrt(2.0 / np.pi).astype(np.float32) * (x + 0.044715 * (x * x * x))))


def _compress_kernel(t_ref, pe_ref, w1_ref, w2_ref, cos_ref, sin_ref, o_ref, *, apply_rope):
    half = CMP_STRIDE * HEAD_DIM
    t = t_ref[0].astype(F32)
    n = t.shape[0]
    a = _dot((t + pe_ref[:, :half]).astype(BF16), w1_ref[:half, :])
    b = _dot((t + pe_ref[:, half:]).astype(BF16), w1_ref[half:, :])
    hid = a + pltpu.roll(b, n - 1, axis=0)
    out = _dot(_gelu_tanh(hid).astype(BF16), w2_ref[...])
    if apply_rope:
        rot = pltpu.roll(out, HEAD_DIM // 2, axis=1)
        out = out * cos_ref[...] + rot * sin_ref[...]
    o_ref[0] = out.astype(BF16)


def _compress(t, pe_flat, w1, w2, cos, sin, apply_rope):
    G, n, width = t.shape
    hidden = w1.shape[1]
    out_block = (1, n, HEAD_DIM)
    return pl.pallas_call(
        functools.partial(_compress_kernel, apply_rope=apply_rope),
        grid=(G,),
        in_specs=[
            pl.BlockSpec((1, n, width), lambda g: (g, 0, 0)),
            pl.BlockSpec((1, 2 * width), lambda g: (0, 0)),
            pl.BlockSpec((2 * width, hidden), lambda g: (0, 0)),
            pl.BlockSpec((hidden, HEAD_DIM), lambda g: (0, 0)),
            pl.BlockSpec((n, HEAD_DIM), lambda g: (0, 0)),
            pl.BlockSpec((n, HEAD_DIM), lambda g: (0, 0)),
        ],
        out_specs=pl.BlockSpec(out_block, lambda g: (g, 0, 0)),
        out_shape=jax.ShapeDtypeStruct((G,) + out_block[1:], BF16),
        compiler_params=_cparams(("arbitrary",)),
        name="nsa_compress_k" if apply_rope else "nsa_compress_v",
    )(t, pe_flat, w1, w2, cos, sin)


def _softmax_cols(s):
    e = jnp.exp2(s - jnp.max(s, axis=0, keepdims=True))
    return e, jnp.sum(e, axis=0, keepdims=True)


def _tile_heads(x):
    return jnp.concatenate([x] * HEADS_PER_GROUP, axis=1)


N_FORCED = 3
SEL_PAD = 16
V_EXTRA_ROWS = 16


def _run_on_shortest_prefix(fn, total, needed, parts):
    piece = total // parts
    sizes = [piece * k for k in range(1, parts + 1)] if piece % LANES == 0 else [total]
    for idx, n in enumerate(sizes):
        cond = needed > (sizes[idx - 1] if idx else 0)
        if idx < len(sizes) - 1:
            cond = cond & (needed <= n)
        pl.when(cond)(functools.partial(fn, n))


def _nsa_attn_kernel(q_ref, gates_ref, kc_ref, vct_ref, ks_ref, vst_ref, wselt_ref, onehot_ref, *rest, n_sel):
    n_win = WINDOW // Q_BLOCK + 1
    kw_refs = rest[:n_win]
    vwt_refs = rest[n_win:2 * n_win]
    o_ref = rest[2 * n_win]
    gall_ref, s0_ref, s1_ref, p0_ref, p1_ref, acc_ref, oc_ref, sblk_ref = rest[2 * n_win + 1:]

    b = pl.program_id(1)
    t0 = b * Q_BLOCK
    Q = Q_BLOCK
    H = HEADS_PER_GROUP
    C = H * Q
    q = q_ref[...]
    qt = jnp.concatenate(
        [q[:, h * HEAD_DIM:(h + 1) * HEAD_DIM].astype(F32).T.astype(BF16) for h in range(H)], axis=1)

    n_cmp = kc_ref.shape[1]
    nsp = wselt_ref.shape[0]
    qi_row = lax.broadcasted_iota(jnp.int32, (1, C), 1) & (Q - 1)

    def compressed(n):
        qi_c = lax.broadcasted_iota(jnp.int32, (n, Q), 1)
        cpos = lax.broadcasted_iota(jnp.int32, (n, Q), 0) * CMP_STRIDE + (CMP_BLOCK - 1)
        bias_c = jnp.where(cpos <= t0 + qi_c, 0.0, NEG_INF)
        e_c, l_c = _softmax_cols(_dot(kc_ref[0, 0:n, :], qt) + _tile_heads(bias_c))
        p_c = e_c * jnp.where(t0 + qi_row >= CMP_BLOCK - 1, 1.0 / l_c, 0.0)
        oc_ref[...] = _dot(vct_ref[0, :, 0:n], p_c.astype(BF16))
        imp = (p_c[:, 0:Q] + p_c[:, Q:2 * Q]) + p_c[:, 2 * Q:3 * Q] + p_c[:, 3 * Q:4 * Q]
        imp_hi = imp.astype(BF16)
        imp_lo = (imp - imp_hi.astype(F32)).astype(BF16)
        wselt = wselt_ref[:, 0:n]
        sblk_ref[...] = _dot(wselt, imp_hi) + _dot(wselt, imp_lo)

    _run_on_shortest_prefix(compressed, n_cmp, (t0 + Q - CMP_BLOCK) // CMP_STRIDE + 1, 4)
    o_c = oc_ref[...]

    def select(rows):
        blk = lax.broadcasted_iota(jnp.int32, (rows, Q), 0)
        qp = t0 + lax.broadcasted_iota(jnp.int32, (rows, Q), 1)
        cur = qp >> 6
        valid = blk * SLC_BLOCK <= qp
        forced = (blk == 0) | (blk == cur) | (blk == cur - 1)
        s_blk = jnp.where(forced, REMOVED, jnp.where(valid, sblk_ref[0:rows, :], NEG_INF))
        blk_f = blk.astype(F32)

        def pick(_, carry):
            s, sel = carry
            m = jnp.max(s, axis=0, keepdims=True)
            first = jnp.min(jnp.where(s == m, blk_f, float(rows)), axis=0, keepdims=True)
            hit = blk_f == first
            return jnp.where(hit, REMOVED, s), jnp.where(hit, 1.0, sel)

        _, sel = lax.fori_loop(0, n_sel - N_FORCED, pick, (s_blk, jnp.where(forced, 1.0, 0.0)))
        gall_ref[SEL_PAD:SEL_PAD + rows, :] = _tile_heads(jnp.where((sel > 0.5) & valid, 0.0, NEG_INF))
        if rows < nsp:
            gall_ref[SEL_PAD + rows:SEL_PAD + nsp, :] = jnp.full((nsp - rows, C), NEG_INF, F32)

    _run_on_shortest_prefix(select, nsp, (t0 + Q - 1) // SLC_BLOCK + 1, 2)

    T = SEL_KEY_TILE
    bpt = T // SLC_BLOCK
    n_tiles = (t0 + Q + T - 1) // T
    gall_ref[0:SEL_PAD, :] = jnp.full((SEL_PAD, C), NEG_INF, F32)
    gall_ref[SEL_PAD + nsp:, :] = jnp.full((SEL_PAD, C), NEG_INF, F32)
    onehot = onehot_ref[(b + 1) & 3]
    zero_rows = jnp.zeros((HEAD_DIM - 2 * bpt, C), BF16)
    k_rows = ks_ref.shape[1]
    v_blocks = vst_ref.shape[1]

    def scores(kt):
        start = pl.multiple_of(jnp.clip(t0 - kt * T, 0, k_rows - T), Q)
        first_blk = 2 * b + 2 - bpt * (kt + 1) + SEL_PAD
        w0 = pl.multiple_of(jnp.maximum((first_blk >> 3) << 3, 0), 8)
        table = gall_ref[pl.ds(w0, 2 * bpt), :].astype(BF16)
        lhs = jnp.concatenate([ks_ref[0, pl.ds(start, T), :], onehot], axis=1)
        rhs = jnp.concatenate([qt, table, zero_rows], axis=0)
        return _dot(lhs, rhs)

    def values(kt):
        vb = jnp.clip(b - (T // Q) * kt, 0, v_blocks - T // Q)
        return jnp.concatenate([vst_ref[0, vb + c] for c in range(T // Q)], axis=1)

    def tile_step(kt, s_cur, s_nxt, p_cur, p_prv, carry):
        alpha_prev, m = carry
        acc_ref[...] = alpha_prev * acc_ref[...] + _dot(values(kt - 1), p_prv[...])
        s_nxt[...] = scores(kt + 1)
        s = s_cur[...]
        m_new = jnp.maximum(m, jnp.max(s, axis=0, keepdims=True))
        p_cur[...] = jnp.exp2((s - m_new).astype(BF16))
        return jnp.exp2(m - m_new), m_new

    def pair_step(j, carry):
        carry = tile_step(2 * j, s0_ref, s1_ref, p0_ref, p1_ref, carry)
        return tile_step(2 * j + 1, s1_ref, s0_ref, p1_ref, p0_ref, carry)

    s_first = scores(0)
    key_j = lax.broadcasted_iota(jnp.int32, (Q, Q), 0)
    qry_i = lax.broadcasted_iota(jnp.int32, (Q, Q), 1)
    s0_ref[0:T - Q, :] = s_first[0:T - Q]
    s0_ref[T - Q:, :] = s_first[T - Q:] + _tile_heads(jnp.where(key_j <= qry_i, 0.0, NEG_INF))
    p1_ref[...] = jnp.zeros_like(p1_ref)
    acc_ref[...] = jnp.zeros_like(acc_ref)
    n_pairs = (n_tiles + 1) // 2
    alpha_last, _ = lax.fori_loop(
        0, n_pairs, pair_step, (jnp.ones((1, C), F32), jnp.full((1, C), NEG_INF, F32)))
    acc_s = alpha_last * acc_ref[...] + _dot(values(2 * n_pairs - 1), p1_ref[...])
    o_s = acc_s[0:HEAD_DIM] * (1.0 / acc_s[HEAD_DIM:HEAD_DIM + 1])

    kw = jnp.concatenate([r[...] for r in kw_refs], axis=0)
    vwt = jnp.concatenate([r[0] for r in vwt_refs], axis=1)
    nw = WINDOW + Q
    row_w = lax.broadcasted_iota(jnp.int32, (nw, Q), 0)
    diff = lax.broadcasted_iota(jnp.int32, (nw, Q), 1) - row_w + WINDOW
    ok_w = (diff >= 0) & (diff < WINDOW) & (row_w >= WINDOW - t0)
    s_w = _dot(kw, qt) + _tile_heads(jnp.where(ok_w, 0.0, NEG_INF))
    e_w = jnp.exp2((s_w - jnp.max(s_w, axis=0, keepdims=True)).astype(BF16))
    acc_w = _dot(vwt, e_w)
    o_w = acc_w[0:HEAD_DIM] * (1.0 / acc_w[HEAD_DIM:HEAD_DIM + 1])

    gt = gates_ref[...].T
    for h in range(H):
        cols = slice(h * Q, (h + 1) * Q)
        o = (gt[h:h + 1] * o_c[:, cols] + gt[H + h:H + h + 1] * o_s[:, cols]
             + gt[2 * H + h:2 * H + h + 1] * o_w[:, cols])
        o_ref[:, h * HEAD_DIM:(h + 1) * HEAD_DIM] = o.T.astype(BF16)


def _selection_matrix_t(n_cmp, nsp):
    j = np.arange(nsp)[:, None]
    c = np.arange(n_cmp)[None, :]
    d = c - RATIO * j
    w = np.where((d == -1) | (d == RATIO - 1), 1.0, np.where((d >= 0) & (d < RATIO - 1), 2.0, 0.0))
    return jnp.asarray(w, BF16)


def _nsa_attention(proj, gates, kc, vct):
    S = proj.shape[0]
    G = N_KV_GROUPS
    T = SEL_KEY_TILE
    nb = S // Q_BLOCK
    n_cmp = kc.shape[1]
    ns = S // SLC_BLOCK
    nsp = -(-ns // LANES) * LANES
    n_sel = min(N_SLC, ns)
    wselt = _selection_matrix_t(n_cmp, nsp)
    n_win = WINDOW // Q_BLOCK + 1
    cpb = KV_WIDTH // HEAD_DIM
    q_w = N_HEADS * HEAD_DIM
    q_cols = q_w // HEAD_DIM
    kw_col = q_cols + 4 * cpb
    pad = T - Q_BLOCK
    ksp = jnp.pad(proj[:, q_w + 2 * KV_WIDTH:q_w + 3 * KV_WIDTH], ((pad, 0), (0, 0)))
    ksp = ksp.reshape(S + pad, G, HEAD_DIM).transpose(1, 0, 2)
    vsp = jnp.pad(proj[:, q_w + 3 * KV_WIDTH:q_w + 4 * KV_WIDTH], ((pad, 0), (0, 0)))
    vst = vsp.reshape((S + pad) // Q_BLOCK, Q_BLOCK, G, HEAD_DIM).transpose(2, 0, 3, 1)
    extra = jnp.zeros(vst.shape[:2] + (V_EXTRA_ROWS, Q_BLOCK), BF16).at[:, :, 0, :].set(1.0)
    vst = jnp.concatenate([vst, extra], axis=2)
    vwt = proj[:, q_w + 5 * KV_WIDTH:q_w + 6 * KV_WIDTH].reshape(S, G, HEAD_DIM).transpose(1, 2, 0)
    vwt = jnp.concatenate([vwt, jnp.zeros((G, V_EXTRA_ROWS, S), BF16).at[:, 0, :].set(1.0)], axis=1)
    r = np.arange(T)[None, :, None] // SLC_BLOCK
    onehot = jnp.asarray(np.arange(LANES)[None, None, :] == 2 * np.arange(4)[:, None, None] + r, BF16)

    def win_block(b, i):
        return jnp.maximum(b - (n_win - 1) + i, 0)

    in_specs = [
        pl.BlockSpec((Q_BLOCK, KV_WIDTH), lambda g, b: (b, g)),
        pl.BlockSpec((Q_BLOCK, LANES), lambda g, b: (b, g)),
        pl.BlockSpec((1, n_cmp, HEAD_DIM), lambda g, b: (g, 0, 0)),
        pl.BlockSpec((1, HEAD_DIM, n_cmp), lambda g, b: (g, 0, 0)),
        pl.BlockSpec((1,) + ksp.shape[1:], lambda g, b: (g, 0, 0)),
        pl.BlockSpec((1,) + vst.shape[1:], lambda g, b: (g, 0, 0, 0)),
        pl.BlockSpec(wselt.shape, lambda g, b: (0, 0)),
        pl.BlockSpec(onehot.shape, lambda g, b: (0, 0, 0)),
    ]
    in_specs += [pl.BlockSpec((Q_BLOCK, HEAD_DIM), functools.partial(lambda g, b, i: (win_block(b, i), kw_col + g), i=i))
                 for i in range(n_win)]
    in_specs += [pl.BlockSpec((1, vwt.shape[1], Q_BLOCK), functools.partial(lambda g, b, i: (g, 0, win_block(b, i)), i=i))
                 for i in range(n_win)]
    args = [proj, gates, kc, vct, ksp, vst, wselt, onehot] + [proj] * n_win + [vwt] * n_win
    return pl.pallas_call(
        functools.partial(_nsa_attn_kernel, n_sel=n_sel),
        grid=(G, nb),
        in_specs=in_specs,
        out_specs=pl.BlockSpec((Q_BLOCK, KV_WIDTH), lambda g, b: (b, g)),
        out_shape=jax.ShapeDtypeStruct((S, N_HEADS * HEAD_DIM), BF16),
        scratch_shapes=[pltpu.VMEM((nsp + 2 * SEL_PAD, KV_WIDTH), F32),
                        pltpu.VMEM((T, KV_WIDTH), F32), pltpu.VMEM((T, KV_WIDTH), F32),
                        pltpu.VMEM((T, KV_WIDTH), BF16), pltpu.VMEM((T, KV_WIDTH), BF16),
                        pltpu.VMEM((HEAD_DIM + V_EXTRA_ROWS, KV_WIDTH), F32),
                        pltpu.VMEM((HEAD_DIM, KV_WIDTH), F32), pltpu.VMEM((nsp, Q_BLOCK), F32)],
        compiler_params=_cparams(("arbitrary", "arbitrary")),
        name="nsa_attention",
    )(*args)


def _const_spec(shape):
    return pl.BlockSpec(shape, lambda i: (0,) * len(shape), pipeline_mode=pl.Buffered(1))


def _mem_kv_kernel(mem_ref, g_ref, wk_ref, wv_ref, k_ref, v_ref):
    mb = _rms_rows(mem_ref[...], g_ref[...]).astype(BF16)
    k_ref[...] = _dot(mb, wk_ref[...]).astype(BF16)
    v_ref[...] = _dot(mb, wv_ref[...]).astype(BF16)


def _mem_kv(mem, g, wk, wv):
    M, D = mem.shape
    W = wk.shape[1]
    full = lambda shape: pl.BlockSpec(shape, lambda i: (0, 0))
    return pl.pallas_call(
        _mem_kv_kernel,
        grid=(1,),
        in_specs=[full((M, D)), full((1, D)), full((D, W)), full((D, W))],
        out_specs=[full((M, W)), full((M, W))],
        out_shape=[jax.ShapeDtypeStruct((M, W), BF16)] * 2,
        compiler_params=_cparams(("arbitrary",)),
        name="mem_kv",
    )(mem, g, wk, wv)


def _route(h, g_ref, rhi_ref, rlo_ref, u_ref, idx_ref, w_ref):
    u = _rms_rows(h, g_ref[...])
    u_ref[...] = u
    u_hi = u.astype(BF16)
    u_lo = (u - u_hi.astype(F32)).astype(BF16)
    rhi = rhi_ref[...]
    logits = (_dot(u_hi, rhi) + _dot(u_lo, rhi)) + _dot(u_hi, rlo_ref[...])
    lane = lax.broadcasted_iota(jnp.int32, logits.shape, 1)
    lane_f = lane.astype(F32)
    s = jnp.where(lane < N_EXPERTS, logits, REMOVED)
    m1 = jnp.max(s, axis=1, keepdims=True)
    i1 = jnp.min(jnp.where(s == m1, lane_f, float(LANES)), axis=1, keepdims=True)
    s = jnp.where(lane_f == i1, REMOVED, s)
    m2 = jnp.max(s, axis=1, keepdims=True)
    i2 = jnp.min(jnp.where(s == m2, lane_f, float(LANES)), axis=1, keepdims=True)
    e2 = jnp.exp(m2 - m1)
    den = 1.0 + e2
    idx_ref[...] = jnp.where(lane == 0, i1, jnp.where(lane == 1, i2, 0.0)).astype(jnp.int32)
    w_ref[...] = jnp.where(lane == 0, 1.0 / den, jnp.where(lane == 1, e2 / den, 0.0))


def _router_kernel(h_ref, g_ref, rhi_ref, rlo_ref, u_ref, idx_ref, w_ref):
    _route(h_ref[...], g_ref, rhi_ref, rlo_ref, u_ref, idx_ref, w_ref)


def _router(h, g, r_hi, r_lo, tm):
    S, D = h.shape
    rows = lambda width: pl.BlockSpec((tm, width), lambda i: (i, 0))
    return pl.pallas_call(
        _router_kernel,
        grid=(S // tm,),
        in_specs=[rows(D), _const_spec(g.shape), _const_spec(r_hi.shape), _const_spec(r_lo.shape)],
        out_specs=[rows(D), rows(LANES), rows(LANES)],
        out_shape=[jax.ShapeDtypeStruct((S, D), F32), jax.ShapeDtypeStruct((S, LANES), jnp.int32),
                   jax.ShapeDtypeStruct((S, LANES), F32)],
        compiler_params=_cparams(("arbitrary",)),
        name="moe_router",
    )(h, g, r_hi, r_lo)


def _mem_attn_kernel(*refs, pre_proj):
    it = iter(refs)
    if pre_proj:
        a_ref, wpre_ref = next(it), next(it)
    res_ref, g_ref, wq_ref, k_ref, v_ref, wo_ref, o_ref = it

    h = res_ref[...]
    if pre_proj:
        h = h + _dot(a_ref[...], wpre_ref[...])
    ub = _rms_rows(h, g_ref[...]).astype(BF16)
    q = (_dot(ub, wq_ref[...]) * (MEM_HEAD_DIM ** -0.5)).astype(BF16)
    outs = []
    for hd in range(MEM_HEADS):
        cols = slice(hd * MEM_HEAD_DIM, (hd + 1) * MEM_HEAD_DIM)
        s = _dot_t(q[:, cols], k_ref[:, cols])
        e = jnp.exp(s - jnp.max(s, axis=1, keepdims=True))
        p = e / jnp.sum(e, axis=1, keepdims=True)
        outs.append(_dot(p.astype(BF16), v_ref[:, cols]).astype(BF16))
    o_ref[...] = h + _dot(jnp.concatenate(outs, axis=1), wo_ref[...])


def _mem_attn(res, g, wq, k, v, wo, tm, pre_proj=None):
    S, D = res.shape
    rows = lambda width: pl.BlockSpec((tm, width), lambda i: (i, 0))
    args, in_specs = [], []
    if pre_proj is not None:
        a, w_pre = pre_proj
        args += [a, w_pre]
        in_specs += [rows(a.shape[1]), _const_spec(w_pre.shape)]
    args += [res, g, wq, k, v, wo]
    in_specs += [rows(D)] + [_const_spec(x.shape) for x in (g, wq, k, v, wo)]
    return pl.pallas_call(
        functools.partial(_mem_attn_kernel, pre_proj=pre_proj is not None),
        grid=(S // tm,),
        in_specs=in_specs,
        out_specs=rows(D),
        out_shape=jax.ShapeDtypeStruct((S, D), F32),
        compiler_params=_cparams(("arbitrary",)),
        name="mem_attention",
    )(*args)


def _swiglu_tile(ub, wg, wu, wd):
    gate = _dot(ub, wg)
    up = _dot(ub, wu)
    act = (gate * jax.nn.sigmoid(gate)) * up
    return _dot(act.astype(BF16), wd)


def _ffn_kernel(h_ref, g_ref, wg_ref, wu_ref, wd_ref, o_ref, u_ref):
    f = pl.program_id(1)

    @pl.when(f == 0)
    def _():
        h = h_ref[...]
        u_ref[...] = _rms_rows(h, g_ref[...]).astype(BF16)
        o_ref[...] = h

    o_ref[...] += _swiglu_tile(u_ref[...], wg_ref[...], wu_ref[...], wd_ref[...])


def _ffn(h, g, wg, wu, wd, tm, tf):
    S, D = h.shape
    F = wg.shape[1]
    return pl.pallas_call(
        _ffn_kernel,
        grid=(S // tm, F // tf),
        in_specs=[
            pl.BlockSpec((tm, D), lambda i, f: (i, 0)),
            pl.BlockSpec((1, D), lambda i, f: (0, 0)),
            pl.BlockSpec((D, tf), lambda i, f: (0, f)),
            pl.BlockSpec((D, tf), lambda i, f: (0, f)),
            pl.BlockSpec((tf, D), lambda i, f: (f, 0)),
        ],
        out_specs=pl.BlockSpec((tm, D), lambda i, f: (i, 0)),
        out_shape=jax.ShapeDtypeStruct((S, D), F32),
        scratch_shapes=[pltpu.VMEM((tm, D), BF16)],
        compiler_params=_cparams(("arbitrary", "arbitrary")),
        name="dense_swiglu",
    )(h, g, wg, wu, wd)


POOL_HALO = 16


def _pool_kernel(h_ref, halo_ref, g_ref, w_ref, b_ref, scale_ref, o_ref, ext_ref):
    i = pl.program_id(0)
    tm = h_ref.shape[0]
    h = h_ref[...]
    g = g_ref[...]
    u = _rms_rows(h, g)
    halo = _rms_rows(halo_ref[...], g)
    ext_ref[0:POOL_HALO, :] = jnp.where(i > 0, halo, 0.0)
    ext_ref[POOL_HALO:, :] = u
    pos = i * tm + lax.broadcasted_iota(jnp.int32, (tm, 1), 0)
    for gi, w in enumerate(POOL_WINDOWS):
        cols = slice(gi * POOL_GROUP, (gi + 1) * POOL_GROUP)
        tot = u[:, cols]
        for k in range(1, w):
            tot = tot + ext_ref[POOL_HALO - k:POOL_HALO - k + tm, cols]
        cnt = jnp.minimum(pos + 1, w).astype(F32)
        d = tot / cnt - u[:, cols]
        z = _dot(d.astype(BF16), w_ref[gi]) + b_ref[:, cols]
        o_ref[:, cols] = h[:, cols] + z * scale_ref[:, cols]


def _pool_mixer(h, g, w, b, scale, tm):
    S, D = h.shape
    ratio = tm // POOL_HALO
    return pl.pallas_call(
        _pool_kernel,
        grid=(S // tm,),
        in_specs=[
            pl.BlockSpec((tm, D), lambda i: (i, 0)),
            pl.BlockSpec((POOL_HALO, D), lambda i: (jnp.maximum(i * ratio - 1, 0), 0)),
            pl.BlockSpec((1, D), lambda i: (0, 0)),
            pl.BlockSpec(w.shape, lambda i: (0, 0, 0)),
            pl.BlockSpec((1, D), lambda i: (0, 0)),
            pl.BlockSpec((1, D), lambda i: (0, 0)),
        ],
        out_specs=pl.BlockSpec((tm, D), lambda i: (i, 0)),
        out_shape=jax.ShapeDtypeStruct((S, D), F32),
        scratch_shapes=[pltpu.VMEM((tm + POOL_HALO, D), F32)],
        compiler_params=_cparams(("arbitrary",)),
        name="pool_mixer",
    )(h, h, g, w, b, scale)


GATHER_UNROLL = 8


def _start_row_gather(src_hbm, dst_vmem, sem, n_rows, src_row_of):
    def body(r, c):
        pltpu.make_async_copy(src_hbm.at[pl.ds(src_row_of(r), 1), :], dst_vmem.at[pl.ds(r, 1), :], sem).start()
        return c

    lax.fori_loop(0, n_rows, body, 0, unroll=GATHER_UNROLL)


def _wait_row_gather(src_hbm, dst_vmem, sem):
    pltpu.make_async_copy(src_hbm.at[pl.ds(0, dst_vmem.shape[0]), :], dst_vmem, sem).wait()


def _moe_expert_kernel(tok_ref, exp_ref, nused_ref, u_hbm, wg_ref, wu_ref, wd_ref, o_ref, x_ref, xb_ref, sem):
    i = pl.program_id(0)
    f = pl.program_id(1)
    rows = x_ref.shape[1]
    n_used = nused_ref[0]
    active = i < n_used
    slot = i & 1

    def gather(block, into):
        _start_row_gather(u_hbm, x_ref.at[into], sem.at[into], rows, lambda r: tok_ref[block * rows + r])

    @pl.when((i == 0) & (f == 0) & active)
    def _():
        gather(0, 0)

    @pl.when(active & (f == 0))
    def _():
        _wait_row_gather(u_hbm, x_ref.at[slot], sem.at[slot])
        xb_ref[...] = x_ref[slot].astype(BF16)
        o_ref[...] = jnp.zeros_like(o_ref)

        @pl.when(i + 1 < n_used)
        def _():
            gather(i + 1, 1 - slot)

    @pl.when(active)
    def _():
        o_ref[...] += _swiglu_tile(xb_ref[...], wg_ref[0].astype(BF16), wu_ref[0].astype(BF16), wd_ref[0])

    @pl.when(jnp.logical_not(active) & (f == 0))
    def _():
        o_ref[...] = jnp.zeros_like(o_ref)


def _moe_experts(tok, blk_exp, n_used, u, wg, wu, wd, tf):
    N, D = u.shape
    n_blk = blk_exp.shape[0]
    F = wg.shape[2]
    rows = MOE_BLOCK
    n_f = F // tf

    def ftile(i, f, nu):
        return jnp.where(i < nu[0], f, n_f - 1)

    grid_spec = pltpu.PrefetchScalarGridSpec(
        num_scalar_prefetch=3,
        grid=(n_blk, n_f),
        in_specs=[
            pl.BlockSpec(memory_space=pl.ANY),
            pl.BlockSpec((1, D, tf), lambda i, f, tok, ex, nu: (ex[i], 0, ftile(i, f, nu))),
            pl.BlockSpec((1, D, tf), lambda i, f, tok, ex, nu: (ex[i], 0, ftile(i, f, nu))),
            pl.BlockSpec((1, tf, D), lambda i, f, tok, ex, nu: (ex[i], ftile(i, f, nu), 0)),
        ],
        out_specs=pl.BlockSpec((rows, D), lambda i, f, tok, ex, nu: (i, 0)),
        scratch_shapes=[pltpu.VMEM((2, rows, D), F32), pltpu.VMEM((rows, D), BF16), pltpu.SemaphoreType.DMA((2,))],
    )
    return pl.pallas_call(
        _moe_expert_kernel,
        grid_spec=grid_spec,
        out_shape=jax.ShapeDtypeStruct((n_blk * rows, D), F32),
        compiler_params=_cparams(("arbitrary", "arbitrary")),
        name="moe_experts",
    )(tok, blk_exp, n_used, u, wg, wu, wd)


def _moe_combine_kernel(pos_ref, h_ref, w_ref, gfin_ref, y_hbm, o_ref, r_ref, sem):
    i = pl.program_id(0)
    n_steps = pl.num_programs(0)
    tm = h_ref.shape[0]
    slot = i & 1

    def gather(step, into):
        for k in range(TOP_K):
            _start_row_gather(y_hbm, r_ref.at[into, k], sem.at[into, k], tm,
                              lambda r: pos_ref[(step * tm + r) * TOP_K + k])

    @pl.when(i == 0)
    def _():
        gather(0, 0)

    for k in range(TOP_K):
        _wait_row_gather(y_hbm, r_ref.at[slot, k], sem.at[slot, k])

    @pl.when(i + 1 < n_steps)
    def _():
        gather(i + 1, 1 - slot)

    w = w_ref[...]
    y = h_ref[...] + (r_ref[slot, 0] * w[:, 0:1] + r_ref[slot, 1] * w[:, 1:2])
    o_ref[...] = _rms_rows(y, gfin_ref[...])


def _moe_combine(pos, h, w, g_final, y, tm):
    S, D = h.shape
    grid_spec = pltpu.PrefetchScalarGridSpec(
        num_scalar_prefetch=1,
        grid=(S // tm,),
        in_specs=[
            pl.BlockSpec((tm, D), lambda i, pos: (i, 0)),
            pl.BlockSpec((tm, LANES), lambda i, pos: (i, 0)),
            pl.BlockSpec((1, D), lambda i, pos: (0, 0)),
            pl.BlockSpec(memory_space=pl.ANY),
        ],
        out_specs=pl.BlockSpec((tm, D), lambda i, pos: (i, 0)),
        scratch_shapes=[pltpu.VMEM((2, TOP_K, tm, D), F32), pltpu.SemaphoreType.DMA((2, TOP_K))],
    )
    return pl.pallas_call(
        _moe_combine_kernel,
        grid_spec=grid_spec,
        out_shape=jax.ShapeDtypeStruct((S, D), F32),
        compiler_params=_cparams(("arbitrary",)),
        name="moe_combine_norm",
    )(pos, h, w, g_final, y)


def _slot_scatter_kernel(dest_ref, tok_ref):
    def zero(i, c):
        tok_ref[i] = 0
        return c

    lax.fori_loop(0, tok_ref.shape[0], zero, 0, unroll=GATHER_UNROLL)

    shift = TOP_K.bit_length() - 1
    assert 1 << shift == TOP_K

    def put(s, c):
        tok_ref[dest_ref[s]] = lax.shift_right_logical(s, shift)
        return c

    lax.fori_loop(0, dest_ref.shape[0], put, 0, unroll=GATHER_UNROLL)


def _slot_scatter(dest, cap):
    return pl.pallas_call(
        _slot_scatter_kernel,
        in_specs=[pl.BlockSpec(memory_space=pltpu.SMEM)],
        out_specs=pl.BlockSpec(memory_space=pltpu.SMEM),
        out_shape=jax.ShapeDtypeStruct((cap,), jnp.int32),
        name="moe_slot_scatter",
    )(dest)


def _moe_dispatch_indices(top_idx):
    N = top_idx.shape[0]
    e_flat = top_idx.reshape(-1)
    onehot = (e_flat[:, None] == jnp.arange(N_EXPERTS, dtype=jnp.int32)[None, :]).astype(jnp.int32)
    csum = jnp.cumsum(onehot, axis=0)
    counts = csum[-1]
    rank = jnp.take_along_axis(csum, e_flat[:, None], axis=1)[:, 0] - 1
    padded = (counts + MOE_BLOCK - 1) // MOE_BLOCK * MOE_BLOCK
    pend = jnp.cumsum(padded)
    poff = pend - padded
    dest = poff[e_flat] + rank
    n_blk = (N * TOP_K + MOE_BLOCK - 1) // MOE_BLOCK + N_EXPERTS
    cap = n_blk * MOE_BLOCK
    tok_buf = _slot_scatter(dest.astype(jnp.int32), cap)
    blk_start = jnp.arange(n_blk, dtype=jnp.int32) * MOE_BLOCK
    blk_exp = jnp.minimum(jnp.sum(pend[None, :] <= blk_start[:, None], axis=1), N_EXPERTS - 1).astype(jnp.int32)
    n_used = (pend[-1] // MOE_BLOCK).astype(jnp.int32).reshape(1)
    return tok_buf, blk_exp, n_used, dest.astype(jnp.int32)


def _rope_tables(pos):
    half = HEAD_DIM // 2
    inv = ROPE_THETA ** (-np.arange(half, dtype=np.float64) / half)
    ang = np.asarray(pos, np.float64)[:, None] * inv[None, :]
    cos, sin = np.cos(ang), np.sin(ang)
    return (jnp.asarray(np.concatenate([cos, cos], axis=1), F32),
            jnp.asarray(np.concatenate([-sin, sin], axis=1), F32))


def _gate_params(w_gates, gate_b):
    D = w_gates.shape[0]
    w = w_gates.reshape(D, 3, N_KV_GROUPS, HEADS_PER_GROUP).transpose(0, 2, 1, 3).reshape(D, N_KV_GROUPS, 3 * HEADS_PER_GROUP)
    w = jnp.pad(w, ((0, 0), (0, 0), (0, LANES - 3 * HEADS_PER_GROUP))).reshape(D, N_KV_GROUPS * LANES)
    b = gate_b.reshape(3, N_KV_GROUPS, HEADS_PER_GROUP).transpose(1, 0, 2).reshape(N_KV_GROUPS, 3 * HEADS_PER_GROUP)
    b = jnp.pad(b, ((0, 0), (0, LANES - 3 * HEADS_PER_GROUP))).reshape(1, N_KV_GROUPS * LANES)
    return w.astype(BF16), b


def _chunked_groups(t):
    S = t.shape[0]
    return (t.reshape(S // CMP_STRIDE, CMP_STRIDE, N_KV_GROUPS, HEAD_DIM)
            .transpose(2, 0, 1, 3).reshape(N_KV_GROUPS, S // CMP_STRIDE, CMP_STRIDE * HEAD_DIM))


def _row_tile(S, want):
    t = min(S, want)
    assert S % t == 0
    return t


def kernel(x, mem, norm_mix, norm_mem_q, norm_mem_kv, norm_ffn, norm_final, nsa_w_in, nsa_gate_b, nsa_pe_k, nsa_pe_v, nsa_cmp_k_w1, nsa_cmp_k_w2, nsa_cmp_v_w1, nsa_cmp_v_w2, nsa_w_out, pool_w, pool_b, pool_scale, mem_wq, mem_wk, mem_wv, mem_wo, ffn_w_gate, ffn_w_up, ffn_w_down, moe_router, moe_w_gate, moe_w_up, moe_w_down):
    B, S, D = x.shape
    assert B == 1 and S % WINDOW == 0 and S >= SLC_BLOCK * N_SLC
    h = x.reshape(S, D)
    memf = mem.reshape(mem.shape[1], D)
    row = lambda v: v.reshape(1, -1)

    n_main = N_HEADS * HEAD_DIM + 6 * KV_WIDTH
    w_in = nsa_w_in[0]
    w_gate, b_gate = _gate_params(w_in[:, n_main:], nsa_gate_b[0])
    cos, sin = _rope_tables(np.arange(S))
    proj, gates = _nsa_proj(h, row(norm_mix[0]), w_in[:, :n_main].astype(BF16), w_gate, b_gate, cos, sin,
                            _row_tile(S, 1024))
    q_w = N_HEADS * HEAD_DIM
    n_chunks = S // CMP_STRIDE
    cmp_pos = np.arange(n_chunks) * CMP_STRIDE + CMP_BLOCK - 1
    ccos, csin = _rope_tables(cmp_pos)
    kc = _compress(_chunked_groups(proj[:, q_w:q_w + KV_WIDTH]), nsa_pe_k[0].reshape(1, -1),
                   nsa_cmp_k_w1[0].astype(BF16), nsa_cmp_k_w2[0].astype(BF16), ccos, csin, True)
    vct = _compress(_chunked_groups(proj[:, q_w + KV_WIDTH:q_w + 2 * KV_WIDTH]), nsa_pe_v[0].reshape(1, -1),
                    nsa_cmp_v_w1[0].astype(BF16), nsa_cmp_v_w2[0].astype(BF16), ccos, csin, False).transpose(0, 2, 1)
    attn = _nsa_attention(proj, gates, kc, vct)

    def mem_layer(h, i, **fused):
        k, v = _mem_kv(memf, row(norm_mem_kv[i]), mem_wk[i].astype(BF16), mem_wv[i].astype(BF16))
        return _mem_attn(h, row(norm_mem_q[i]), mem_wq[i].astype(BF16), k, v, mem_wo[i].astype(BF16),
                         _row_tile(S, 512), **fused)

    h = mem_layer(h, 0, pre_proj=(attn, nsa_w_out[0].astype(BF16)))
    h = _ffn(h, row(norm_ffn[0]), ffn_w_gate[0].astype(BF16), ffn_w_up[0].astype(BF16),
             ffn_w_down[0].astype(BF16), _row_tile(S, 1024), 512)

    h = _pool_mixer(h, row(norm_mix[1]), pool_w[0].astype(BF16), row(pool_b[0]), row(pool_scale[0]),
                    _row_tile(S, 512))
    h = mem_layer(h, 1)
    r = jnp.pad(moe_router[0], ((0, 0), (0, LANES - N_EXPERTS)))
    r_hi = r.astype(BF16)
    r_lo = (r - r_hi.astype(F32)).astype(BF16)
    u, idx, gate_w = _router(h, row(norm_ffn[1]), r_hi, r_lo, _row_tile(S, 512))
    tok_buf, blk_exp, n_used, dest = _moe_dispatch_indices(idx[:, :TOP_K])
    y = _moe_experts(tok_buf, blk_exp, n_used, u, moe_w_gate[0], moe_w_up[0], moe_w_down[0].astype(BF16), 512)
    out = _moe_combine(dest, h, gate_w, row(norm_final), y, _row_tile(S, 256))
    return out.reshape(B, S, D)
```

```python
import functools

import numpy as np
import jax
import jax.numpy as jnp
from jax import lax
from jax.experimental import pallas as pl
from jax.experimental.pallas import tpu as pltpu

F32 = jnp.float32
BF16 = jnp.bfloat16

N_HEADS = 16
HEAD_DIM = 128
N_KV_GROUPS = 4
HEADS_PER_GROUP = 4
KV_WIDTH = N_KV_GROUPS * HEAD_DIM
CMP_BLOCK = 32
CMP_STRIDE = 16
SLC_BLOCK = 64
RATIO = SLC_BLOCK // CMP_STRIDE
N_SLC = 16
WINDOW = 512
Q_BLOCK = 128
ROPE_THETA = 10000.0
POOL_WINDOWS = (2, 4, 8, 16)
POOL_GROUP = 512
MEM_HEADS = 4
MEM_HEAD_DIM = 128
N_EXPERTS = 8
TOP_K = 2
MOE_BLOCK = 512
EPS = 1e-6
NEG_INF = -1e30
REMOVED = -3e38
LOG2E = float(np.log2(np.e))

LANES = 128
VMEM_LIMIT = 56 * 1024 * 1024

SEL_KEY_TILE = 512


def _cparams(sem):
    return pltpu.CompilerParams(dimension_semantics=sem, vmem_limit_bytes=VMEM_LIMIT)


def _rms_rows(xf, g):
    r = lax.rsqrt(jnp.mean(xf * xf, axis=-1, keepdims=True) + EPS)
    return (xf * r) * g


def _dot(a, b):
    return jnp.dot(a, b, preferred_element_type=F32)


def _dot_t(a, b):
    return lax.dot_general(a, b, (((1,), (1,)), ((), ())), preferred_element_type=F32)


def _proj_kernel(x_ref, g_ref, w_ref, wgate_ref, bgate_ref, cos_ref, sin_ref, o_ref, gates_ref, u_ref):
    j = pl.program_id(1)

    @pl.when(j == 0)
    def _():
        ub = _rms_rows(x_ref[...], g_ref[...]).astype(BF16)
        u_ref[...] = ub
        gates_ref[...] = jax.nn.sigmoid(_dot(ub, wgate_ref[...]) + bgate_ref[...])

    acc = _dot(u_ref[...], w_ref[...])
    n_q_tiles = N_HEADS * HEAD_DIM // KV_WIDTH
    is_q = j < n_q_tiles
    is_rope = is_q | (j == n_q_tiles + 2) | (j == n_q_tiles + 4)

    @pl.when(is_rope)
    def _():
        scale = jnp.where(is_q, HEAD_DIM ** -0.5 * LOG2E, 1.0).astype(F32)
        cos = cos_ref[...]
        sin = sin_ref[...]
        for h in range(KV_WIDTH // HEAD_DIM):
            xh = acc[:, h * HEAD_DIM:(h + 1) * HEAD_DIM]
            rot = pltpu.roll(xh, HEAD_DIM // 2, axis=1)
            o_ref[:, h * HEAD_DIM:(h + 1) * HEAD_DIM] = ((xh * cos + rot * sin) * scale).astype(BF16)

    @pl.when(jnp.logical_not(is_rope))
    def _():
        o_ref[...] = acc.astype(BF16)


def _nsa_proj(x, g, w_main, w_gate, b_gate, cos, sin, tm):
    S, D = x.shape
    n_main = w_main.shape[1]
    tn = KV_WIDTH
    n_gate = w_gate.shape[1]
    return pl.pallas_call(
        _proj_kernel,
        grid=(S // tm, n_main // tn),
        in_specs=[
            pl.BlockSpec((tm, D), lambda i, j: (i, 0)),
            pl.BlockSpec((1, D), lambda i, j: (0, 0)),
            pl.BlockSpec((D, tn), lambda i, j: (0, j)),
            pl.BlockSpec((D, n_gate), lambda i, j: (0, 0)),
            pl.BlockSpec((1, n_gate), lambda i, j: (0, 0)),
            pl.BlockSpec((tm, HEAD_DIM), lambda i, j: (i, 0)),
            pl.BlockSpec((tm, HEAD_DIM), lambda i, j: (i, 0)),
        ],
        out_specs=[
            pl.BlockSpec((tm, tn), lambda i, j: (i, j)),
            pl.BlockSpec((tm, n_gate), lambda i, j: (i, 0)),
        ],
        out_shape=[
            jax.ShapeDtypeStruct((S, n_main), BF16),
            jax.ShapeDtypeStruct((S, n_gate), F32),
        ],
        scratch_shapes=[pltpu.VMEM((tm, D), BF16)],
        compiler_params=_cparams(("arbitrary", "arbitrary")),
        name="nsa_proj",
    )(x, g, w_main, w_gate, b_gate, cos, sin)


def _gelu_tanh(x):
    return 0.5 * x * (1.0 + jnp.tanh(np.sqrt(2.0 / np.pi).astype(np.float32) * (x + 0.044715 * (x * x * x))))


def _compress_kernel(t_ref, pe_ref, w1_ref, w2_ref, cos_ref, sin_ref, o_ref, *, apply_rope):
    half = CMP_STRIDE * HEAD_DIM
    t = t_ref[0].astype(F32)
    n = t.shape[0]
    a = _dot((t + pe_ref[:, :half]).astype(BF16), w1_ref[:half, :])
    b = _dot((t + pe_ref[:, half:]).astype(BF16), w1_ref[half:, :])
    hid = a + pltpu.roll(b, n - 1, axis=0)
    out = _dot(_gelu_tanh(hid).astype(BF16), w2_ref[...])
    if apply_rope:
        rot = pltpu.roll(out, HEAD_DIM // 2, axis=1)
        out = out * cos_ref[...] + rot * sin_ref[...]
    o_ref[0] = out.astype(BF16)


def _compress(t, pe_flat, w1, w2, cos, sin, apply_rope):
    G, n, width = t.shape
    hidden = w1.shape[1]
    out_block = (1, n, HEAD_DIM)
    return pl.pallas_call(
        functools.partial(_compress_kernel, apply_rope=apply_rope),
        grid=(G,),
        in_specs=[
            pl.BlockSpec((1, n, width), lambda g: (g, 0, 0)),
            pl.BlockSpec((1, 2 * width), lambda g: (0, 0)),
            pl.BlockSpec((2 * width, hidden), lambda g: (0, 0)),
            pl.BlockSpec((hidden, HEAD_DIM), lambda g: (0, 0)),
            pl.BlockSpec((n, HEAD_DIM), lambda g: (0, 0)),
            pl.BlockSpec((n, HEAD_DIM), lambda g: (0, 0)),
        ],
        out_specs=pl.BlockSpec(out_block, lambda g: (g, 0, 0)),
        out_shape=jax.ShapeDtypeStruct((G,) + out_block[1:], BF16),
        compiler_params=_cparams(("arbitrary",)),
        name="nsa_compress_k" if apply_rope else "nsa_compress_v",
    )(t, pe_flat, w1, w2, cos, sin)


def _softmax_cols(s):
    e = jnp.exp2(s - jnp.max(s, axis=0, keepdims=True))
    return e, jnp.sum(e, axis=0, keepdims=True)


def _tile_heads(x):
    return jnp.concatenate([x] * HEADS_PER_GROUP, axis=1)


N_FORCED = 3
SEL_PAD = 16
V_EXTRA_ROWS = 16


def _run_on_shortest_prefix(fn, total, needed, piece):
    sizes = list(range(piece, total + 1, piece)) if total % piece == 0 else [total]
    for idx, n in enumerate(sizes):
        cond = needed > (sizes[idx - 1] if idx else 0)
        if idx < len(sizes) - 1:
            cond = cond & (needed <= n)
        pl.when(cond)(functools.partial(fn, n))


def _nsa_attn_kernel(q_ref, gates_ref, kc_ref, vct_ref, ks_ref, vst_ref, wselt_ref, onehot_ref, *rest, n_sel):
    n_win = WINDOW // Q_BLOCK + 1
    kw_refs = rest[:n_win]
    vwt_refs = rest[n_win:2 * n_win]
    o_ref = rest[2 * n_win]
    gall_ref, s0_ref, s1_ref, p0_ref, p1_ref, acc_ref, oc_ref, sblk_ref = rest[2 * n_win + 1:]

    b = pl.program_id(1)
    t0 = b * Q_BLOCK
    Q = Q_BLOCK
    H = HEADS_PER_GROUP
    C = H * Q
    q = q_ref[...]
    qt = jnp.concatenate(
        [q[:, h * HEAD_DIM:(h + 1) * HEAD_DIM].astype(F32).T.astype(BF16) for h in range(H)], axis=1)

    n_cmp = kc_ref.shape[1]
    nsp = wselt_ref.shape[0]
    qi_row = lax.broadcasted_iota(jnp.int32, (1, C), 1) & (Q - 1)

    def compressed(n):
        qi_c = lax.broadcasted_iota(jnp.int32, (n, Q), 1)
        cpos = lax.broadcasted_iota(jnp.int32, (n, Q), 0) * CMP_STRIDE + (CMP_BLOCK - 1)
        bias_c = jnp.where(cpos <= t0 + qi_c, 0.0, NEG_INF)
        e_c, l_c = _softmax_cols(_dot(kc_ref[0, 0:n, :], qt) + _tile_heads(bias_c))
        p_c = e_c * jnp.where(t0 + qi_row >= CMP_BLOCK - 1, 1.0 / l_c, 0.0)
        oc_ref[...] = _dot(vct_ref[0, :, 0:n], p_c.astype(BF16))
        imp = (p_c[:, 0:Q] + p_c[:, Q:2 * Q]) + p_c[:, 2 * Q:3 * Q] + p_c[:, 3 * Q:4 * Q]
        imp_hi = imp.astype(BF16)
        imp_lo = (imp - imp_hi.astype(F32)).astype(BF16)
        wselt = wselt_ref[:, 0:n]
        sblk_ref[...] = _dot(wselt, imp_hi) + _dot(wselt, imp_lo)

    _run_on_shortest_prefix(compressed, n_cmp, (t0 + Q - CMP_BLOCK) // CMP_STRIDE + 1, LANES)
    o_c = oc_ref[...]

    def select(rows):
        blk = lax.broadcasted_iota(jnp.int32, (rows, Q), 0)
        qp = t0 + lax.broadcasted_iota(jnp.int32, (rows, Q), 1)
        cur = qp >> 6
        valid = blk * SLC_BLOCK <= qp
        forced = (blk == 0) | (blk == cur) | (blk == cur - 1)
        s_blk = jnp.where(forced, REMOVED, jnp.where(valid, sblk_ref[0:rows, :], NEG_INF))
        blk_f = blk.astype(F32)

        def pick(_, s):
            m = jnp.max(s, axis=0, keepdims=True)
            first = jnp.min(jnp.where(s == m, blk_f, float(rows)), axis=0, keepdims=True)
            return jnp.where(blk_f == first, REMOVED, s)

        s_blk = lax.fori_loop(0, n_sel - N_FORCED, pick, s_blk)
        gall_ref[SEL_PAD:SEL_PAD + rows, :] = _tile_heads(jnp.where((s_blk == REMOVED) & valid, 0.0, NEG_INF))
        if rows < nsp:
            gall_ref[SEL_PAD + rows:SEL_PAD + nsp, :] = jnp.full((nsp - rows, C), NEG_INF, F32)

    _run_on_shortest_prefix(select, nsp, (t0 + Q - 1) // SLC_BLOCK + 1, LANES // 2)

    T = SEL_KEY_TILE
    bpt = T // SLC_BLOCK
    n_tiles = (t0 + Q + T - 1) // T
    gall_ref[0:SEL_PAD, :] = jnp.full((SEL_PAD, C), NEG_INF, F32)
    gall_ref[SEL_PAD + nsp:, :] = jnp.full((SEL_PAD, C), NEG_INF, F32)
    onehot = onehot_ref[(b + 1) & 3]
    zero_rows = jnp.zeros((HEAD_DIM - 2 * bpt, C), BF16)
    k_rows = ks_ref.shape[1]
    v_blocks = vst_ref.shape[1]

    def scores(kt):
        start = pl.multiple_of(jnp.clip(t0 - kt * T, 0, k_rows - T), Q)
        first_blk = 2 * b + 2 - bpt * (kt + 1) + SEL_PAD
        w0 = pl.multiple_of(jnp.maximum((first_blk >> 3) << 3, 0), 8)
        table = gall_ref[pl.ds(w0, 2 * bpt), :].astype(BF16)
        lhs = jnp.concatenate([ks_ref[0, pl.ds(start, T), :], onehot], axis=1)
        rhs = jnp.concatenate([qt, table, zero_rows], axis=0)
        return _dot(lhs, rhs)

    def values(kt):
        vb = jnp.clip(b - (T // Q) * kt, 0, v_blocks - T // Q)
        return jnp.concatenate([vst_ref[0, vb + c] for c in range(T // Q)], axis=1)

    def tile_step(kt, s_cur, s_nxt, p_cur, p_prv, carry):
        alpha_prev, m = carry
        acc_ref[...] = alpha_prev * acc_ref[...] + _dot(values(kt - 1), p_prv[...])
        s_nxt[...] = scores(kt + 1)
        s = s_cur[...]
        m_new = jnp.maximum(m, jnp.max(s, axis=0, keepdims=True))
        p_cur[...] = jnp.exp2((s - m_new).astype(BF16))
        return jnp.exp2(m - m_new), m_new

    def pair_step(j, carry):
        carry = tile_step(2 * j, s0_ref, s1_ref, p0_ref, p1_ref, carry)
        return tile_step(2 * j + 1, s1_ref, s0_ref, p1_ref, p0_ref, carry)

    s_first = scores(0)
    key_j = lax.broadcasted_iota(jnp.int32, (Q, Q), 0)
    qry_i = lax.broadcasted_iota(jnp.int32, (Q, Q), 1)
    s0_ref[0:T - Q, :] = s_first[0:T - Q]
    s0_ref[T - Q:, :] = s_first[T - Q:] + _tile_heads(jnp.where(key_j <= qry_i, 0.0, NEG_INF))
    p1_ref[...] = jnp.zeros_like(p1_ref)
    acc_ref[...] = jnp.zeros_like(acc_ref)
    n_pairs = (n_tiles + 1) // 2
    alpha_last, _ = lax.fori_loop(
        0, n_pairs, pair_step, (jnp.ones((1, C), F32), jnp.full((1, C), NEG_INF, F32)))
    acc_s = alpha_last * acc_ref[...] + _dot(values(2 * n_pairs - 1), p1_ref[...])
    o_s = acc_s[0:HEAD_DIM] * (1.0 / acc_s[HEAD_DIM:HEAD_DIM + 1])

    kw = jnp.concatenate([r[...] for r in kw_refs], axis=0)
    vwt = jnp.concatenate([r[0] for r in vwt_refs], axis=1)
    nw = WINDOW + Q
    row_w = lax.broadcasted_iota(jnp.int32, (nw, Q), 0)
    diff = lax.broadcasted_iota(jnp.int32, (nw, Q), 1) - row_w + WINDOW
    ok_w = (diff >= 0) & (diff < WINDOW) & (row_w >= WINDOW - t0)
    s_w = _dot(kw, qt) + _tile_heads(jnp.where(ok_w, 0.0, NEG_INF))
    e_w = jnp.exp2((s_w - jnp.max(s_w, axis=0, keepdims=True)).astype(BF16))
    acc_w = _dot(vwt, e_w)
    o_w = acc_w[0:HEAD_DIM] * (1.0 / acc_w[HEAD_DIM:HEAD_DIM + 1])

    gt = gates_ref[...].T
    for h in range(H):
        cols = slice(h * Q, (h + 1) * Q)
        o = (gt[h:h + 1] * o_c[:, cols] + gt[H + h:H + h + 1] * o_s[:, cols]
             + gt[2 * H + h:2 * H + h + 1] * o_w[:, cols])
        o_ref[:, h * HEAD_DIM:(h + 1) * HEAD_DIM] = o.T.astype(BF16)


def _selection_matrix_t(n_cmp, nsp):
    j = np.arange(nsp)[:, None]
    c = np.arange(n_cmp)[None, :]
    d = c - RATIO * j
    w = np.where((d == -1) | (d == RATIO - 1), 1.0, np.where((d >= 0) & (d < RATIO - 1), 2.0, 0.0))
    return jnp.asarray(w, BF16)


def _nsa_attention(proj, gates, kc, vct):
    S = proj.shape[0]
    G = N_KV_GROUPS
    T = SEL_KEY_TILE
    nb = S // Q_BLOCK
    n_cmp = kc.shape[1]
    ns = S // SLC_BLOCK
    nsp = -(-ns // LANES) * LANES
    n_sel = min(N_SLC, ns)
    wselt = _selection_matrix_t(n_cmp, nsp)
    n_win = WINDOW // Q_BLOCK + 1
    cpb = KV_WIDTH // HEAD_DIM
    q_w = N_HEADS * HEAD_DIM
    q_cols = q_w // HEAD_DIM
    kw_col = q_cols + 4 * cpb
    pad = T - Q_BLOCK
    ksp = jnp.pad(proj[:, q_w + 2 * KV_WIDTH:q_w + 3 * KV_WIDTH], ((pad, 0), (0, 0)))
    ksp = ksp.reshape(S + pad, G, HEAD_DIM).transpose(1, 0, 2)
    vsp = jnp.pad(proj[:, q_w + 3 * KV_WIDTH:q_w + 4 * KV_WIDTH], ((pad, 0), (0, 0)))
    vst = vsp.reshape((S + pad) // Q_BLOCK, Q_BLOCK, G, HEAD_DIM).transpose(2, 0, 3, 1)
    extra = jnp.zeros(vst.shape[:2] + (V_EXTRA_ROWS, Q_BLOCK), BF16).at[:, :, 0, :].set(1.0)
    vst = jnp.concatenate([vst, extra], axis=2)
    vwt = proj[:, q_w + 5 * KV_WIDTH:q_w + 6 * KV_WIDTH].reshape(S, G, HEAD_DIM).transpose(1, 2, 0)
    vwt = jnp.concatenate([vwt, jnp.zeros((G, V_EXTRA_ROWS, S), BF16).at[:, 0, :].set(1.0)], axis=1)
    r = np.arange(T)[None, :, None] // SLC_BLOCK
    onehot = jnp.asarray(np.arange(LANES)[None, None, :] == 2 * np.arange(4)[:, None, None] + r, BF16)

    def win_block(b, i):
        return jnp.maximum(b - (n_win - 1) + i, 0)

    in_specs = [
        pl.BlockSpec((Q_BLOCK, KV_WIDTH), lambda g, b: (b, g)),
        pl.BlockSpec((Q_BLOCK, LANES), lambda g, b: (b, g)),
        pl.BlockSpec((1, n_cmp, HEAD_DIM), lambda g, b: (g, 0, 0)),
        pl.BlockSpec((1, HEAD_DIM, n_cmp), lambda g, b: (g, 0, 0)),
        pl.BlockSpec((1,) + ksp.shape[1:], lambda g, b: (g, 0, 0)),
        pl.BlockSpec((1,) + vst.shape[1:], lambda g, b: (g, 0, 0, 0)),
        pl.BlockSpec(wselt.shape, lambda g, b: (0, 0)),
        pl.BlockSpec(onehot.shape, lambda g, b: (0, 0, 0)),
    ]
    in_specs += [pl.BlockSpec((Q_BLOCK, HEAD_DIM), functools.partial(lambda g, b, i: (win_block(b, i), kw_col + g), i=i))
                 for i in range(n_win)]
    in_specs += [pl.BlockSpec((1, vwt.shape[1], Q_BLOCK), functools.partial(lambda g, b, i: (g, 0, win_block(b, i)), i=i))
                 for i in range(n_win)]
    args = [proj, gates, kc, vct, ksp, vst, wselt, onehot] + [proj] * n_win + [vwt] * n_win
    return pl.pallas_call(
        functools.partial(_nsa_attn_kernel, n_sel=n_sel),
        grid=(G, nb),
        in_specs=in_specs,
        out_specs=pl.BlockSpec((Q_BLOCK, KV_WIDTH), lambda g, b: (b, g)),
        out_shape=jax.ShapeDtypeStruct((S, N_HEADS * HEAD_DIM), BF16),
        scratch_shapes=[pltpu.VMEM((nsp + 2 * SEL_PAD, KV_WIDTH), F32),
                        pltpu.VMEM((T, KV_WIDTH), F32), pltpu.VMEM((T, KV_WIDTH), F32),
                        pltpu.VMEM((T, KV_WIDTH), BF16), pltpu.VMEM((T, KV_WIDTH), BF16),
                        pltpu.VMEM((HEAD_DIM + V_EXTRA_ROWS, KV_WIDTH), F32),
                        pltpu.VMEM((HEAD_DIM, KV_WIDTH), F32), pltpu.VMEM((nsp, Q_BLOCK), F32)],
        compiler_params=_cparams(("arbitrary", "arbitrary")),
        name="nsa_attention",
    )(*args)


def _const_spec(shape):
    return pl.BlockSpec(shape, lambda i: (0,) * len(shape), pipeline_mode=pl.Buffered(1))


def _mem_kv_kernel(mem_ref, g_ref, wk_ref, wv_ref, k_ref, v_ref):
    mb = _rms_rows(mem_ref[...], g_ref[...]).astype(BF16)
    k_ref[...] = _dot(mb, wk_ref[...]).astype(BF16)
    v_ref[...] = _dot(mb, wv_ref[...]).astype(BF16)


def _mem_kv(mem, g, wk, wv):
    M, D = mem.shape
    W = wk.shape[1]
    full = lambda shape: pl.BlockSpec(shape, lambda i: (0, 0))
    return pl.pallas_call(
        _mem_kv_kernel,
        grid=(1,),
        in_specs=[full((M, D)), full((1, D)), full((D, W)), full((D, W))],
        out_specs=[full((M, W)), full((M, W))],
        out_shape=[jax.ShapeDtypeStruct((M, W), BF16)] * 2,
        compiler_params=_cparams(("arbitrary",)),
        name="mem_kv",
    )(mem, g, wk, wv)


def _route(h, g_ref, rhi_ref, rlo_ref, u_ref, idx_ref, w_ref):
    u = _rms_rows(h, g_ref[...])
    u_ref[...] = u
    u_hi = u.astype(BF16)
    u_lo = (u - u_hi.astype(F32)).astype(BF16)
    rhi = rhi_ref[...]
    logits = (_dot(u_hi, rhi) + _dot(u_lo, rhi)) + _dot(u_hi, rlo_ref[...])
    lane = lax.broadcasted_iota(jnp.int32, logits.shape, 1)
    lane_f = lane.astype(F32)
    s = jnp.where(lane < N_EXPERTS, logits, REMOVED)
    m1 = jnp.max(s, axis=1, keepdims=True)
    i1 = jnp.min(jnp.where(s == m1, lane_f, float(LANES)), axis=1, keepdims=True)
    s = jnp.where(lane_f == i1, REMOVED, s)
    m2 = jnp.max(s, axis=1, keepdims=True)
    i2 = jnp.min(jnp.where(s == m2, lane_f, float(LANES)), axis=1, keepdims=True)
    e2 = jnp.exp(m2 - m1)
    den = 1.0 + e2
    idx_ref[...] = jnp.where(lane == 0, i1, jnp.where(lane == 1, i2, 0.0)).astype(jnp.int32)
    w_ref[...] = jnp.where(lane == 0, 1.0 / den, jnp.where(lane == 1, e2 / den, 0.0))


def _router_kernel(h_ref, g_ref, rhi_ref, rlo_ref, u_ref, idx_ref, w_ref):
    _route(h_ref[...], g_ref, rhi_ref, rlo_ref, u_ref, idx_ref, w_ref)


def _router(h, g, r_hi, r_lo, tm):
    S, D = h.shape
    rows = lambda width: pl.BlockSpec((tm, width), lambda i: (i, 0))
    return pl.pallas_call(
        _router_kernel,
        grid=(S // tm,),
        in_specs=[rows(D), _const_spec(g.shape), _const_spec(r_hi.shape), _const_spec(r_lo.shape)],
        out_specs=[rows(D), rows(LANES), rows(LANES)],
        out_shape=[jax.ShapeDtypeStruct((S, D), F32), jax.ShapeDtypeStruct((S, LANES), jnp.int32),
                   jax.ShapeDtypeStruct((S, LANES), F32)],
        compiler_params=_cparams(("arbitrary",)),
        name="moe_router",
    )(h, g, r_hi, r_lo)


def _mem_attn_kernel(*refs, pre_proj):
    it = iter(refs)
    if pre_proj:
        a_ref, wpre_ref = next(it), next(it)
    res_ref, g_ref, wq_ref, k_ref, v_ref, wo_ref, o_ref = it

    h = res_ref[...]
    if pre_proj:
        h = h + _dot(a_ref[...], wpre_ref[...])
    ub = _rms_rows(h, g_ref[...]).astype(BF16)
    q = (_dot(ub, wq_ref[...]) * (MEM_HEAD_DIM ** -0.5)).astype(BF16)
    outs = []
    for hd in range(MEM_HEADS):
        cols = slice(hd * MEM_HEAD_DIM, (hd + 1) * MEM_HEAD_DIM)
        s = _dot_t(q[:, cols], k_ref[:, cols])
        e = jnp.exp(s - jnp.max(s, axis=1, keepdims=True))
        p = e / jnp.sum(e, axis=1, keepdims=True)
        outs.append(_dot(p.astype(BF16), v_ref[:, cols]).astype(BF16))
    o_ref[...] = h + _dot(jnp.concatenate(outs, axis=1), wo_ref[...])


def _mem_attn(res, g, wq, k, v, wo, tm, pre_proj=None):
    S, D = res.shape
    rows = lambda width: pl.BlockSpec((tm, width), lambda i: (i, 0))
    args, in_specs = [], []
    if pre_proj is not None:
        a, w_pre = pre_proj
        args += [a, w_pre]
        in_specs += [rows(a.shape[1]), _const_spec(w_pre.shape)]
    args += [res, g, wq, k, v, wo]
    in_specs += [rows(D)] + [_const_spec(x.shape) for x in (g, wq, k, v, wo)]
    return pl.pallas_call(
        functools.partial(_mem_attn_kernel, pre_proj=pre_proj is not None),
        grid=(S // tm,),
        in_specs=in_specs,
        out_specs=rows(D),
        out_shape=jax.ShapeDtypeStruct((S, D), F32),
        compiler_params=_cparams(("arbitrary",)),
        name="mem_attention",
    )(*args)


def _swiglu_tile(ub, wg, wu, wd):
    gate = _dot(ub, wg)
    up = _dot(ub, wu)
    act = (gate * jax.nn.sigmoid(gate)) * up
    return _dot(act.astype(BF16), wd)


def _ffn_kernel(h_ref, g_ref, wg_ref, wu_ref, wd_ref, o_ref, u_ref):
    f = pl.program_id(1)

    @pl.when(f == 0)
    def _():
        h = h_ref[...]
        u_ref[...] = _rms_rows(h, g_ref[...]).astype(BF16)
        o_ref[...] = h

    o_ref[...] += _swiglu_tile(u_ref[...], wg_ref[...], wu_ref[...], wd_ref[...])


def _ffn(h, g, wg, wu, wd, tm, tf):
    S, D = h.shape
    F = wg.shape[1]
    return pl.pallas_call(
        _ffn_kernel,
        grid=(S // tm, F // tf),
        in_specs=[
            pl.BlockSpec((tm, D), lambda i, f: (i, 0)),
            pl.BlockSpec((1, D), lambda i, f: (0, 0)),
            pl.BlockSpec((D, tf), lambda i, f: (0, f)),
            pl.BlockSpec((D, tf), lambda i, f: (0, f)),
            pl.BlockSpec((tf, D), lambda i, f: (f, 0)),
        ],
        out_specs=pl.BlockSpec((tm, D), lambda i, f: (i, 0)),
        out_shape=jax.ShapeDtypeStruct((S, D), F32),
        scratch_shapes=[pltpu.VMEM((tm, D), BF16)],
        compiler_params=_cparams(("arbitrary", "arbitrary")),
        name="dense_swiglu",
    )(h, g, wg, wu, wd)


POOL_HALO = 16


def _pool_kernel(h_ref, halo_ref, g_ref, w_ref, b_ref, scale_ref, o_ref, ext_ref):
    i = pl.program_id(0)
    tm = h_ref.shape[0]
    h = h_ref[...]
    g = g_ref[...]
    u = _rms_rows(h, g)
    halo = _rms_rows(halo_ref[...], g)
    ext_ref[0:POOL_HALO, :] = jnp.where(i > 0, halo, 0.0)
    ext_ref[POOL_HALO:, :] = u
    pos = i * tm + lax.broadcasted_iota(jnp.int32, (tm, 1), 0)
    for gi, w in enumerate(POOL_WINDOWS):
        cols = slice(gi * POOL_GROUP, (gi + 1) * POOL_GROUP)
        tot = u[:, cols]
        for k in range(1, w):
            tot = tot + ext_ref[POOL_HALO - k:POOL_HALO - k + tm, cols]
        cnt = jnp.minimum(pos + 1, w).astype(F32)
        d = tot / cnt - u[:, cols]
        z = _dot(d.astype(BF16), w_ref[gi]) + b_ref[:, cols]
        o_ref[:, cols] = h[:, cols] + z * scale_ref[:, cols]


def _pool_mixer(h, g, w, b, scale, tm):
    S, D = h.shape
    ratio = tm // POOL_HALO
    return pl.pallas_call(
        _pool_kernel,
        grid=(S // tm,),
        in_specs=[
            pl.BlockSpec((tm, D), lambda i: (i, 0)),
            pl.BlockSpec((POOL_HALO, D), lambda i: (jnp.maximum(i * ratio - 1, 0), 0)),
            pl.BlockSpec((1, D), lambda i: (0, 0)),
            pl.BlockSpec(w.shape, lambda i: (0, 0, 0)),
            pl.BlockSpec((1, D), lambda i: (0, 0)),
            pl.BlockSpec((1, D), lambda i: (0, 0)),
        ],
        out_specs=pl.BlockSpec((tm, D), lambda i: (i, 0)),
        out_shape=jax.ShapeDtypeStruct((S, D), F32),
        scratch_shapes=[pltpu.VMEM((tm + POOL_HALO, D), F32)],
        compiler_params=_cparams(("arbitrary",)),
        name="pool_mixer",
    )(h, h, g, w, b, scale)


GATHER_UNROLL = 8


def _start_row_gather(src_hbm, dst_vmem, sem, n_rows, src_row_of):
    def body(r, c):
        pltpu.make_async_copy(src_hbm.at[pl.ds(src_row_of(r), 1), :], dst_vmem.at[pl.ds(r, 1), :], sem).start()
        return c

    lax.fori_loop(0, n_rows, body, 0, unroll=GATHER_UNROLL)


def _wait_row_gather(src_hbm, dst_vmem, sem):
    pltpu.make_async_copy(src_hbm.at[pl.ds(0, dst_vmem.shape[0]), :], dst_vmem, sem).wait()


def _moe_expert_kernel(tok_ref, exp_ref, nused_ref, u_hbm, wg_ref, wu_ref, wd_ref, o_ref, x_ref, xb_ref, sem):
    i = pl.program_id(0)
    f = pl.program_id(1)
    rows = x_ref.shape[1]
    n_used = nused_ref[0]
    active = i < n_used
    slot = i & 1

    def gather(block, into):
        _start_row_gather(u_hbm, x_ref.at[into], sem.at[into], rows, lambda r: tok_ref[block * rows + r])

    @pl.when((i == 0) & (f == 0) & active)
    def _():
        gather(0, 0)

    @pl.when(active & (f == 0))
    def _():
        _wait_row_gather(u_hbm, x_ref.at[slot], sem.at[slot])
        xb_ref[...] = x_ref[slot].astype(BF16)
        o_ref[...] = jnp.zeros_like(o_ref)

        @pl.when(i + 1 < n_used)
        def _():
            gather(i + 1, 1 - slot)

    @pl.when(active)
    def _():
        o_ref[...] += _swiglu_tile(xb_ref[...], wg_ref[0].astype(BF16), wu_ref[0].astype(BF16), wd_ref[0])

    @pl.when(jnp.logical_not(active) & (f == 0))
    def _():
        o_ref[...] = jnp.zeros_like(o_ref)


def _moe_experts(tok, blk_exp, n_used, u, wg, wu, wd, tf):
    N, D = u.shape
    n_blk = blk_exp.shape[0]
    F = wg.shape[2]
    rows = MOE_BLOCK
    n_f = F // tf

    def ftile(i, f, nu):
        return jnp.where(i < nu[0], f, n_f - 1)

    grid_spec = pltpu.PrefetchScalarGridSpec(
        num_scalar_prefetch=3,
        grid=(n_blk, n_f),
        in_specs=[
            pl.BlockSpec(memory_space=pl.ANY),
            pl.BlockSpec((1, D, tf), lambda i, f, tok, ex, nu: (ex[i], 0, ftile(i, f, nu))),
            pl.BlockSpec((1, D, tf), lambda i, f, tok, ex, nu: (ex[i], 0, ftile(i, f, nu))),
            pl.BlockSpec((1, tf, D), lambda i, f, tok, ex, nu: (ex[i], ftile(i, f, nu), 0)),
        ],
        out_specs=pl.BlockSpec((rows, D), lambda i, f, tok, ex, nu: (i, 0)),
        scratch_shapes=[pltpu.VMEM((2, rows, D), F32), pltpu.VMEM((rows, D), BF16), pltpu.SemaphoreType.DMA((2,))],
    )
    return pl.pallas_call(
        _moe_expert_kernel,
        grid_spec=grid_spec,
        out_shape=jax.ShapeDtypeStruct((n_blk * rows, D), F32),
        compiler_params=_cparams(("arbitrary", "arbitrary")),
        name="moe_experts",
    )(tok, blk_exp, n_used, u, wg, wu, wd)


def _moe_combine_kernel(pos_ref, h_ref, w_ref, gfin_ref, y_hbm, o_ref, r_ref, sem):
    i = pl.program_id(0)
    n_steps = pl.num_programs(0)
    tm = h_ref.shape[0]
    slot = i & 1

    def gather(step, into):
        for k in range(TOP_K):
            _start_row_gather(y_hbm, r_ref.at[into, k], sem.at[into, k], tm,
                              lambda r: pos_ref[(step * tm + r) * TOP_K + k])

    @pl.when(i == 0)
    def _():
        gather(0, 0)

    for k in range(TOP_K):
        _wait_row_gather(y_hbm, r_ref.at[slot, k], sem.at[slot, k])

    @pl.when(i + 1 < n_steps)
    def _():
        gather(i + 1, 1 - slot)

    w = w_ref[...]
    y = h_ref[...] + (r_ref[slot, 0] * w[:, 0:1] + r_ref[slot, 1] * w[:, 1:2])
    o_ref[...] = _rms_rows(y, gfin_ref[...])


def _moe_combine(pos, h, w, g_final, y, tm):
    S, D = h.shape
    grid_spec = pltpu.PrefetchScalarGridSpec(
        num_scalar_prefetch=1,
        grid=(S // tm,),
        in_specs=[
            pl.BlockSpec((tm, D), lambda i, pos: (i, 0)),
            pl.BlockSpec((tm, LANES), lambda i, pos: (i, 0)),
            pl.BlockSpec((1, D), lambda i, pos: (0, 0)),
            pl.BlockSpec(memory_space=pl.ANY),
        ],
        out_specs=pl.BlockSpec((tm, D), lambda i, pos: (i, 0)),
        scratch_shapes=[pltpu.VMEM((2, TOP_K, tm, D), F32), pltpu.SemaphoreType.DMA((2, TOP_K))],
    )
    return pl.pallas_call(
        _moe_combine_kernel,
        grid_spec=grid_spec,
        out_shape=jax.ShapeDtypeStruct((S, D), F32),
        compiler_params=_cparams(("arbitrary",)),
        name="moe_combine_norm",
    )(pos, h, w, g_final, y)


def _slot_scatter_kernel(dest_ref, tok_ref):
    def zero(i, c):
        tok_ref[i] = 0
        return c

    lax.fori_loop(0, tok_ref.shape[0], zero, 0, unroll=GATHER_UNROLL)

    shift = TOP_K.bit_length() - 1
    assert 1 << shift == TOP_K

    def put(s, c):
        tok_ref[dest_ref[s]] = lax.shift_right_logical(s, shift)
        return c

    lax.fori_loop(0, dest_ref.shape[0], put, 0, unroll=GATHER_UNROLL)


def _slot_scatter(dest, cap):
    return pl.pallas_call(
        _slot_scatter_kernel,
        in_specs=[pl.BlockSpec(memory_space=pltpu.SMEM)],
        out_specs=pl.BlockSpec(memory_space=pltpu.SMEM),
        out_shape=jax.ShapeDtypeStruct((cap,), jnp.int32),
        name="moe_slot_scatter",
    )(dest)


def _moe_dispatch_indices(top_idx):
    N = top_idx.shape[0]
    e_flat = top_idx.reshape(-1)
    onehot = (e_flat[:, None] == jnp.arange(N_EXPERTS, dtype=jnp.int32)[None, :]).astype(jnp.int32)
    csum = jnp.cumsum(onehot, axis=0)
    counts = csum[-1]
    rank = jnp.take_along_axis(csum, e_flat[:, None], axis=1)[:, 0] - 1
    padded = (counts + MOE_BLOCK - 1) // MOE_BLOCK * MOE_BLOCK
    pend = jnp.cumsum(padded)
    poff = pend - padded
    dest = poff[e_flat] + rank
    n_blk = (N * TOP_K + MOE_BLOCK - 1) // MOE_BLOCK + N_EXPERTS
    cap = n_blk * MOE_BLOCK
    tok_buf = _slot_scatter(dest.astype(jnp.int32), cap)
    blk_start = jnp.arange(n_blk, dtype=jnp.int32) * MOE_BLOCK
    blk_exp = jnp.minimum(jnp.sum(pend[None, :] <= blk_start[:, None], axis=1), N_EXPERTS - 1).astype(jnp.int32)
    n_used = (pend[-1] // MOE_BLOCK).astype(jnp.int32).reshape(1)
    return tok_buf, blk_exp, n_used, dest.astype(jnp.int32)


def _rope_tables(pos):
    half = HEAD_DIM // 2
    inv = ROPE_THETA ** (-np.arange(half, dtype=np.float64) / half)
    ang = np.asarray(pos, np.float64)[:, None] * inv[None, :]
    cos, sin = np.cos(ang), np.sin(ang)
    return (jnp.asarray(np.concatenate([cos, cos], axis=1), F32),
            jnp.asarray(np.concatenate([-sin, sin], axis=1), F32))


def _gate_params(w_gates, gate_b):
    D = w_gates.shape[0]
    w = w_gates.reshape(D, 3, N_KV_GROUPS, HEADS_PER_GROUP).transpose(0, 2, 1, 3).reshape(D, N_KV_GROUPS, 3 * HEADS_PER_GROUP)
    w = jnp.pad(w, ((0, 0), (0, 0), (0, LANES - 3 * HEADS_PER_GROUP))).reshape(D, N_KV_GROUPS * LANES)
    b = gate_b.reshape(3, N_KV_GROUPS, HEADS_PER_GROUP).transpose(1, 0, 2).reshape(N_KV_GROUPS, 3 * HEADS_PER_GROUP)
    b = jnp.pad(b, ((0, 0), (0, LANES - 3 * HEADS_PER_GROUP))).reshape(1, N_KV_GROUPS * LANES)
    return w.astype(BF16), b


def _chunked_groups(t):
    S = t.shape[0]
    return (t.reshape(S // CMP_STRIDE, CMP_STRIDE, N_KV_GROUPS, HEAD_DIM)
            .transpose(2, 0, 1, 3).reshape(N_KV_GROUPS, S // CMP_STRIDE, CMP_STRIDE * HEAD_DIM))


def _row_tile(S, want):
    t = min(S, want)
    assert S % t == 0
    return t


def kernel(x, mem, norm_mix, norm_mem_q, norm_mem_kv, norm_ffn, norm_final, nsa_w_in, nsa_gate_b, nsa_pe_k, nsa_pe_v, nsa_cmp_k_w1, nsa_cmp_k_w2, nsa_cmp_v_w1, nsa_cmp_v_w2, nsa_w_out, pool_w, pool_b, pool_scale, mem_wq, mem_wk, mem_wv, mem_wo, ffn_w_gate, ffn_w_up, ffn_w_down, moe_router, moe_w_gate, moe_w_up, moe_w_down):
    B, S, D = x.shape
    assert B == 1 and S % WINDOW == 0 and S >= SLC_BLOCK * N_SLC
    h = x.reshape(S, D)
    memf = mem.reshape(mem.shape[1], D)
    row = lambda v: v.reshape(1, -1)

    n_main = N_HEADS * HEAD_DIM + 6 * KV_WIDTH
    w_in = nsa_w_in[0]
    w_gate, b_gate = _gate_params(w_in[:, n_main:], nsa_gate_b[0])
    cos, sin = _rope_tables(np.arange(S))
    proj, gates = _nsa_proj(h, row(norm_mix[0]), w_in[:, :n_main].astype(BF16), w_gate, b_gate, cos, sin,
                            _row_tile(S, 1024))
    q_w = N_HEADS * HEAD_DIM
    n_chunks = S // CMP_STRIDE
    cmp_pos = np.arange(n_chunks) * CMP_STRIDE + CMP_BLOCK - 1
    ccos, csin = _rope_tables(cmp_pos)
    kc = _compress(_chunked_groups(proj[:, q_w:q_w + KV_WIDTH]), nsa_pe_k[0].reshape(1, -1),
                   nsa_cmp_k_w1[0].astype(BF16), nsa_cmp_k_w2[0].astype(BF16), ccos, csin, True)
    vct = _compress(_chunked_groups(proj[:, q_w + KV_WIDTH:q_w + 2 * KV_WIDTH]), nsa_pe_v[0].reshape(1, -1),
                    nsa_cmp_v_w1[0].astype(BF16), nsa_cmp_v_w2[0].astype(BF16), ccos, csin, False).transpose(0, 2, 1)
    attn = _nsa_attention(proj, gates, kc, vct)

    def mem_layer(h, i, **fused):
        k, v = _mem_kv(memf, row(norm_mem_kv[i]), mem_wk[i].astype(BF16), mem_wv[i].astype(BF16))
        return _mem_attn(h, row(norm_mem_q[i]), mem_wq[i].astype(BF16), k, v, mem_wo[i].astype(BF16),
                         _row_tile(S, 512), **fused)

    h = mem_layer(h, 0, pre_proj=(attn, nsa_w_out[0].astype(BF16)))
    h = _ffn(h, row(norm_ffn[0]), ffn_w_gate[0].astype(BF16), ffn_w_up[0].astype(BF16),
             ffn_w_down[0].astype(BF16), _row_tile(S, 1024), 512)

    h = _pool_mixer(h, row(norm_mix[1]), pool_w[0].astype(BF16), row(pool_b[0]), row(pool_scale[0]),
                    _row_tile(S, 512))
    h = mem_layer(h, 1)
    r = jnp.pad(moe_router[0], ((0, 0), (0, LANES - N_EXPERTS)))
    r_hi = r.astype(BF16)
    r_lo = (r - r_hi.astype(F32)).astype(BF16)
    u, idx, gate_w = _router(h, row(norm_ffn[1]), r_hi, r_lo, _row_tile(S, 512))
    tok_buf, blk_exp, n_used, dest = _moe_dispatch_indices(idx[:, :TOP_K])
    y = _moe_experts(tok_buf, blk_exp, n_used, u, moe_w_gate[0], moe_w_up[0], moe_w_down[0].astype(BF16), 512)
    out = _moe_combine(dest, h, gate_w, row(norm_final), y, _row_tile(S, 256))
    return out.reshape(B, S, D)
```

```python
import functools

import numpy as np
import jax
import jax.numpy as jnp
from jax import lax
from jax.experimental import pallas as pl
from jax.experimental.pallas import tpu as pltpu

F32 = jnp.float32
BF16 = jnp.bfloat16

N_HEADS = 16
HEAD_DIM = 128
N_KV_GROUPS = 4
HEADS_PER_GROUP = 4
KV_WIDTH = N_KV_GROUPS * HEAD_DIM
CMP_BLOCK = 32
CMP_STRIDE = 16
SLC_BLOCK = 64
RATIO = SLC_BLOCK // CMP_STRIDE
N_SLC = 16
WINDOW = 512
Q_BLOCK = 128
ROPE_THETA = 10000.0
POOL_WINDOWS = (2, 4, 8, 16)
POOL_GROUP = 512
MEM_HEADS = 4
MEM_HEAD_DIM = 128
N_EXPERTS = 8
TOP_K = 2
MOE_BLOCK = 512
EPS = 1e-6
NEG_INF = -1e30
REMOVED = -3e38
LOG2E = float(np.log2(np.e))

LANES = 128
VMEM_LIMIT = 56 * 1024 * 1024

SEL_KEY_TILE = 512


def _cparams(sem):
    return pltpu.CompilerParams(dimension_semantics=sem, vmem_limit_bytes=VMEM_LIMIT)


def _rms_rows(xf, g):
    r = lax.rsqrt(jnp.mean(xf * xf, axis=-1, keepdims=True) + EPS)
    return (xf * r) * g


def _dot(a, b):
    return jnp.dot(a, b, preferred_element_type=F32)


def _dot_t(a, b):
    return lax.dot_general(a, b, (((1,), (1,)), ((), ())), preferred_element_type=F32)


def _proj_kernel(x_ref, g_ref, w_ref, wgate_ref, bgate_ref, cos_ref, sin_ref, o_ref, gates_ref, u_ref):
    j = pl.program_id(1)

    @pl.when(j == 0)
    def _():
        ub = _rms_rows(x_ref[...], g_ref[...]).astype(BF16)
        u_ref[...] = ub
        gates_ref[...] = jax.nn.sigmoid(_dot(ub, wgate_ref[...]) + bgate_ref[...])

    acc = _dot(u_ref[...], w_ref[...])
    n_q_tiles = N_HEADS * HEAD_DIM // KV_WIDTH
    is_q = j < n_q_tiles
    is_rope = is_q | (j == n_q_tiles + 2) | (j == n_q_tiles + 4)

    @pl.when(is_rope)
    def _():
        scale = jnp.where(is_q, HEAD_DIM ** -0.5 * LOG2E, 1.0).astype(F32)
        cos = cos_ref[...]
        sin = sin_ref[...]
        for h in range(KV_WIDTH // HEAD_DIM):
            xh = acc[:, h * HEAD_DIM:(h + 1) * HEAD_DIM]
            rot = pltpu.roll(xh, HEAD_DIM // 2, axis=1)
            o_ref[:, h * HEAD_DIM:(h + 1) * HEAD_DIM] = ((xh * cos + rot * sin) * scale).astype(BF16)

    @pl.when(jnp.logical_not(is_rope))
    def _():
        o_ref[...] = acc.astype(BF16)


def _nsa_proj(x, g, w_main, w_gate, b_gate, cos, sin, tm):
    S, D = x.shape
    n_main = w_main.shape[1]
    tn = KV_WIDTH
    n_gate = w_gate.shape[1]
    return pl.pallas_call(
        _proj_kernel,
        grid=(S // tm, n_main // tn),
        in_specs=[
            pl.BlockSpec((tm, D), lambda i, j: (i, 0)),
            pl.BlockSpec((1, D), lambda i, j: (0, 0)),
            pl.BlockSpec((D, tn), lambda i, j: (0, j)),
            pl.BlockSpec((D, n_gate), lambda i, j: (0, 0)),
            pl.BlockSpec((1, n_gate), lambda i, j: (0, 0)),
            pl.BlockSpec((tm, HEAD_DIM), lambda i, j: (i, 0)),
            pl.BlockSpec((tm, HEAD_DIM), lambda i, j: (i, 0)),
        ],
        out_specs=[
            pl.BlockSpec((tm, tn), lambda i, j: (i, j)),
            pl.BlockSpec((tm, n_gate), lambda i, j: (i, 0)),
        ],
        out_shape=[
            jax.ShapeDtypeStruct((S, n_main), BF16),
            jax.ShapeDtypeStruct((S, n_gate), F32),
        ],
        scratch_shapes=[pltpu.VMEM((tm, D), BF16)],
        compiler_params=_cparams(("arbitrary", "arbitrary")),
        name="nsa_proj",
    )(x, g, w_main, w_gate, b_gate, cos, sin)


def _gelu_tanh(x):
    return 0.5 * x * (1.0 + jnp.tanh(np.sqrt(2.0 / np.pi).astype(np.float32) * (x + 0.044715 * (x * x * x))))


def _compress_kernel(t_ref, pe_ref, w1_ref, w2_ref, cos_ref, sin_ref, o_ref, *, apply_rope):
    half = CMP_STRIDE * HEAD_DIM
    t = t_ref[0].astype(F32)
    n = t.shape[0]
    a = _dot((t + pe_ref[:, :half]).astype(BF16), w1_ref[:half, :])
    b = _dot((t + pe_ref[:, half:]).astype(BF16), w1_ref[half:, :])
    hid = a + pltpu.roll(b, n - 1, axis=0)
    out = _dot(_gelu_tanh(hid).astype(BF16), w2_ref[...])
    if apply_rope:
        rot = pltpu.roll(out, HEAD_DIM // 2, axis=1)
        out = out * cos_ref[...] + rot * sin_ref[...]
    o_ref[0] = out.astype(BF16)


def _compress(t, pe_flat, w1, w2, cos, sin, apply_rope):
    G, n, width = t.shape
    hidden = w1.shape[1]
    out_block = (1, n, HEAD_DIM)
    return pl.pallas_call(
        functools.partial(_compress_kernel, apply_rope=apply_rope),
        grid=(G,),
        in_specs=[
            pl.BlockSpec((1, n, width), lambda g: (g, 0, 0)),
            pl.BlockSpec((1, 2 * width), lambda g: (0, 0)),
            pl.BlockSpec((2 * width, hidden), lambda g: (0, 0)),
            pl.BlockSpec((hidden, HEAD_DIM), lambda g: (0, 0)),
            pl.BlockSpec((n, HEAD_DIM), lambda g: (0, 0)),
            pl.BlockSpec((n, HEAD_DIM), lambda g: (0, 0)),
        ],
        out_specs=pl.BlockSpec(out_block, lambda g: (g, 0, 0)),
        out_shape=jax.ShapeDtypeStruct((G,) + out_block[1:], BF16),
        compiler_params=_cparams(("arbitrary",)),
        name="nsa_compress_k" if apply_rope else "nsa_compress_v",
    )(t, pe_flat, w1, w2, cos, sin)


def _softmax_cols(s):
    e = jnp.exp2(s - jnp.max(s, axis=0, keepdims=True))
    return e, jnp.sum(e, axis=0, keepdims=True)


def _tile_heads(x):
    return jnp.concatenate([x] * HEADS_PER_GROUP, axis=1)


N_FORCED = 3
SEL_PAD = 16
V_EXTRA_ROWS = 16


def _run_on_shortest_prefix(fn, total, needed, piece):
    sizes = list(range(piece, total + 1, piece)) if total % piece == 0 else [total]
    for idx, n in enumerate(sizes):
        cond = needed > (sizes[idx - 1] if idx else 0)
        if idx < len(sizes) - 1:
            cond = cond & (needed <= n)
        pl.when(cond)(functools.partial(fn, n))


def _nsa_attn_kernel(q_ref, gates_ref, kc_ref, vct_ref, ks_ref, vst_ref, wselt_ref, onehot_ref, *rest, n_sel):
    n_win = WINDOW // Q_BLOCK + 1
    kw_refs = rest[:n_win]
    vwt_refs = rest[n_win:2 * n_win]
    o_ref = rest[2 * n_win]
    gall_ref, s0_ref, s1_ref, p0_ref, p1_ref, acc_ref, oc_ref, sblk_ref = rest[2 * n_win + 1:]

    b = pl.program_id(1)
    t0 = b * Q_BLOCK
    Q = Q_BLOCK
    H = HEADS_PER_GROUP
    C = H * Q
    q = q_ref[...]
    qt = jnp.concatenate(
        [q[:, h * HEAD_DIM:(h + 1) * HEAD_DIM].astype(F32).T.astype(BF16) for h in range(H)], axis=1)

    n_cmp = kc_ref.shape[1]
    nsp = wselt_ref.shape[0]
    qi_row = lax.broadcasted_iota(jnp.int32, (1, C), 1) & (Q - 1)

    def compressed(n):
        qi_c = lax.broadcasted_iota(jnp.int32, (n, Q), 1)
        cpos = lax.broadcasted_iota(jnp.int32, (n, Q), 0) * CMP_STRIDE + (CMP_BLOCK - 1)
        bias_c = jnp.where(cpos <= t0 + qi_c, 0.0, NEG_INF)
        e_c, l_c = _softmax_cols(_dot(kc_ref[0, 0:n, :], qt) + _tile_heads(bias_c))
        p_c = e_c * jnp.where(t0 + qi_row >= CMP_BLOCK - 1, 1.0 / l_c, 0.0)
        oc_ref[...] = _dot(vct_ref[0, :, 0:n], p_c.astype(BF16))
        imp = (p_c[:, 0:Q] + p_c[:, Q:2 * Q]) + p_c[:, 2 * Q:3 * Q] + p_c[:, 3 * Q:4 * Q]
        imp_hi = imp.astype(BF16)
        imp_lo = (imp - imp_hi.astype(F32)).astype(BF16)
        wselt = wselt_ref[:, 0:n]
        sblk_ref[...] = _dot(wselt, imp_hi) + _dot(wselt, imp_lo)

    _run_on_shortest_prefix(compressed, n_cmp, (t0 + Q - CMP_BLOCK) // CMP_STRIDE + 1, LANES)
    o_c = oc_ref[...]

    def select(rows):
        blk = lax.broadcasted_iota(jnp.int32, (rows, Q), 0)
        qp = t0 + lax.broadcasted_iota(jnp.int32, (rows, Q), 1)
        cur = qp >> 6
        valid = blk * SLC_BLOCK <= qp
        forced = (blk == 0) | (blk == cur) | (blk == cur - 1)
        s_blk = jnp.where(forced, REMOVED, jnp.where(valid, sblk_ref[0:rows, :], NEG_INF))
        blk_f = blk.astype(F32)

        def pick(_, s):
            m = jnp.max(s, axis=0, keepdims=True)
            first = jnp.min(jnp.where(s == m, blk_f, float(rows)), axis=0, keepdims=True)
            return jnp.where(blk_f == first, REMOVED, s)

        s_blk = lax.fori_loop(0, n_sel - N_FORCED, pick, s_blk)
        gall_ref[SEL_PAD:SEL_PAD + rows, :] = _tile_heads(jnp.where((s_blk == REMOVED) & valid, 0.0, NEG_INF))
        if rows < nsp:
            gall_ref[SEL_PAD + rows:SEL_PAD + nsp, :] = jnp.full((nsp - rows, C), NEG_INF, F32)

    _run_on_shortest_prefix(select, nsp, (t0 + Q - 1) // SLC_BLOCK + 1, LANES // 2)

    T = SEL_KEY_TILE
    bpt = T // SLC_BLOCK
    n_tiles = (t0 + Q + T - 1) // T
    gall_ref[0:SEL_PAD, :] = jnp.full((SEL_PAD, C), NEG_INF, F32)
    gall_ref[SEL_PAD + nsp:, :] = jnp.full((SEL_PAD, C), NEG_INF, F32)
    onehot = onehot_ref[(b + 1) & 3]
    zero_rows = jnp.zeros((HEAD_DIM - 2 * bpt, C), BF16)
    k_rows = ks_ref.shape[1]
    v_blocks = vst_ref.shape[1]

    def scores(kt):
        start = pl.multiple_of(jnp.clip(t0 - kt * T, 0, k_rows - T), Q)
        first_blk = 2 * b + 2 - bpt * (kt + 1) + SEL_PAD
        w0 = pl.multiple_of(jnp.maximum((first_blk >> 3) << 3, 0), 8)
        table = gall_ref[pl.ds(w0, 2 * bpt), :].astype(BF16)
        lhs = jnp.concatenate([ks_ref[0, pl.ds(start, T), :], onehot], axis=1)
        rhs = jnp.concatenate([qt, table, zero_rows], axis=0)
        return _dot(lhs, rhs)

    def values(kt):
        vb = jnp.clip(b - (T // Q) * kt, 0, v_blocks - T // Q)
        return jnp.concatenate([vst_ref[0, vb + c] for c in range(T // Q)], axis=1)

    def tile_step(kt, s_cur, s_nxt, p_cur, p_prv, carry):
        alpha_prev, m = carry
        acc_ref[...] = alpha_prev * acc_ref[...] + _dot(values(kt - 1), p_prv[...])
        s_nxt[...] = scores(kt + 1)
        s = s_cur[...]
        m_new = jnp.maximum(m, jnp.max(s, axis=0, keepdims=True))
        p_cur[...] = jnp.exp2((s - m_new).astype(BF16))
        return jnp.exp2(m - m_new), m_new

    def pair_step(j, carry):
        carry = tile_step(2 * j, s0_ref, s1_ref, p0_ref, p1_ref, carry)
        return tile_step(2 * j + 1, s1_ref, s0_ref, p1_ref, p0_ref, carry)

    s_first = scores(0)
    key_j = lax.broadcasted_iota(jnp.int32, (Q, Q), 0)
    qry_i = lax.broadcasted_iota(jnp.int32, (Q, Q), 1)
    s0_ref[0:T - Q, :] = s_first[0:T - Q]
    s0_ref[T - Q:, :] = s_first[T - Q:] + _tile_heads(jnp.where(key_j <= qry_i, 0.0, NEG_INF))
    p1_ref[...] = jnp.zeros_like(p1_ref)
    acc_ref[...] = jnp.zeros_like(acc_ref)
    n_pairs = (n_tiles + 1) // 2
    alpha_last, _ = lax.fori_loop(
        0, n_pairs, pair_step, (jnp.ones((1, C), F32), jnp.full((1, C), NEG_INF, F32)))
    acc_s = alpha_last * acc_ref[...] + _dot(values(2 * n_pairs - 1), p1_ref[...])
    o_s = acc_s[0:HEAD_DIM] * (1.0 / acc_s[HEAD_DIM:HEAD_DIM + 1])

    kw = jnp.concatenate([r[...] for r in kw_refs], axis=0)
    vwt = jnp.concatenate([r[0] for r in vwt_refs], axis=1)
    nw = WINDOW + Q
    row_w = lax.broadcasted_iota(jnp.int32, (nw, Q), 0)
    diff = lax.broadcasted_iota(jnp.int32, (nw, Q), 1) - row_w + WINDOW
    ok_w = (diff >= 0) & (diff < WINDOW) & (row_w >= WINDOW - t0)
    s_w = _dot(kw, qt) + _tile_heads(jnp.where(ok_w, 0.0, NEG_INF))
    e_w = jnp.exp2((s_w - jnp.max(s_w, axis=0, keepdims=True)).astype(BF16))
    acc_w = _dot(vwt, e_w)
    o_w = acc_w[0:HEAD_DIM] * (1.0 / acc_w[HEAD_DIM:HEAD_DIM + 1])

    gt = gates_ref[...].T
    for h in range(H):
        cols = slice(h * Q, (h + 1) * Q)
        o = (gt[h:h + 1] * o_c[:, cols] + gt[H + h:H + h + 1] * o_s[:, cols]
             + gt[2 * H + h:2 * H + h + 1] * o_w[:, cols])
        o_ref[:, h * HEAD_DIM:(h + 1) * HEAD_DIM] = o.T.astype(BF16)


def _selection_matrix_t(n_cmp, nsp):
    j = np.arange(nsp)[:, None]
    c = np.arange(n_cmp)[None, :]
    d = c - RATIO * j
    w = np.where((d == -1) | (d == RATIO - 1), 1.0, np.where((d >= 0) & (d < RATIO - 1), 2.0, 0.0))
    return jnp.asarray(w, BF16)


def _nsa_attention(proj, gates, kc, vct):
    S = proj.shape[0]
    G = N_KV_GROUPS
    T = SEL_KEY_TILE
    nb = S // Q_BLOCK
    n_cmp = kc.shape[1]
    ns = S // SLC_BLOCK
    nsp = -(-ns // LANES) * LANES
    n_sel = min(N_SLC, ns)
    wselt = _selection_matrix_t(n_cmp, nsp)
    n_win = WINDOW // Q_BLOCK + 1
    cpb = KV_WIDTH // HEAD_DIM
    q_w = N_HEADS * HEAD_DIM
    q_cols = q_w // HEAD_DIM
    kw_col = q_cols + 4 * cpb
    pad = T - Q_BLOCK
    ksp = jnp.pad(proj[:, q_w + 2 * KV_WIDTH:q_w + 3 * KV_WIDTH], ((pad, 0), (0, 0)))
    ksp = ksp.reshape(S + pad, G, HEAD_DIM).transpose(1, 0, 2)
    vsp = jnp.pad(proj[:, q_w + 3 * KV_WIDTH:q_w + 4 * KV_WIDTH], ((pad, 0), (0, 0)))
    vst = vsp.reshape((S + pad) // Q_BLOCK, Q_BLOCK, G, HEAD_DIM).transpose(2, 0, 3, 1)
    extra = jnp.zeros(vst.shape[:2] + (V_EXTRA_ROWS, Q_BLOCK), BF16).at[:, :, 0, :].set(1.0)
    vst = jnp.concatenate([vst, extra], axis=2)
    vwt = proj[:, q_w + 5 * KV_WIDTH:q_w + 6 * KV_WIDTH].reshape(S, G, HEAD_DIM).transpose(1, 2, 0)
    vwt = jnp.concatenate([vwt, jnp.zeros((G, V_EXTRA_ROWS, S), BF16).at[:, 0, :].set(1.0)], axis=1)
    r = np.arange(T)[None, :, None] // SLC_BLOCK
    onehot = jnp.asarray(np.arange(LANES)[None, None, :] == 2 * np.arange(4)[:, None, None] + r, BF16)

    def win_block(b, i):
        return jnp.maximum(b - (n_win - 1) + i, 0)

    in_specs = [
        pl.BlockSpec((Q_BLOCK, KV_WIDTH), lambda g, b: (b, g)),
        pl.BlockSpec((Q_BLOCK, LANES), lambda g, b: (b, g)),
        pl.BlockSpec((1, n_cmp, HEAD_DIM), lambda g, b: (g, 0, 0)),
        pl.BlockSpec((1, HEAD_DIM, n_cmp), lambda g, b: (g, 0, 0)),
        pl.BlockSpec((1,) + ksp.shape[1:], lambda g, b: (g, 0, 0)),
        pl.BlockSpec((1,) + vst.shape[1:], lambda g, b: (g, 0, 0, 0)),
        pl.BlockSpec(wselt.shape, lambda g, b: (0, 0)),
        pl.BlockSpec(onehot.shape, lambda g, b: (0, 0, 0)),
    ]
    in_specs += [pl.BlockSpec((Q_BLOCK, HEAD_DIM), functools.partial(lambda g, b, i: (win_block(b, i), kw_col + g), i=i))
                 for i in range(n_win)]
    in_specs += [pl.BlockSpec((1, vwt.shape[1], Q_BLOCK), functools.partial(lambda g, b, i: (g, 0, win_block(b, i)), i=i))
                 for i in range(n_win)]
    args = [proj, gates, kc, vct, ksp, vst, wselt, onehot] + [proj] * n_win + [vwt] * n_win
    return pl.pallas_call(
        functools.partial(_nsa_attn_kernel, n_sel=n_sel),
        grid=(G, nb),
        in_specs=in_specs,
        out_specs=pl.BlockSpec((Q_BLOCK, KV_WIDTH), lambda g, b: (b, g)),
        out_shape=jax.ShapeDtypeStruct((S, N_HEADS * HEAD_DIM), BF16),
        scratch_shapes=[pltpu.VMEM((nsp + 2 * SEL_PAD, KV_WIDTH), F32),
                        pltpu.VMEM((T, KV_WIDTH), F32), pltpu.VMEM((T, KV_WIDTH), F32),
                        pltpu.VMEM((T, KV_WIDTH), BF16), pltpu.VMEM((T, KV_WIDTH), BF16),
                        pltpu.VMEM((HEAD_DIM + V_EXTRA_ROWS, KV_WIDTH), F32),
                        pltpu.VMEM((HEAD_DIM, KV_WIDTH), F32), pltpu.VMEM((nsp, Q_BLOCK), F32)],
        compiler_params=_cparams(("arbitrary", "arbitrary")),
        name="nsa_attention",
    )(*args)


def _const_spec(shape):
    return pl.BlockSpec(shape, lambda i: (0,) * len(shape), pipeline_mode=pl.Buffered(1))


def _mem_kv_kernel(mem_ref, g_ref, wk_ref, wv_ref, k_ref, v_ref):
    mb = _rms_rows(mem_ref[...], g_ref[...]).astype(BF16)
    k_ref[...] = _dot(mb, wk_ref[...]).astype(BF16)
    v_ref[...] = _dot(mb, wv_ref[...]).astype(BF16)


def _mem_kv(mem, g, wk, wv):
    M, D = mem.shape
    W = wk.shape[1]
    full = lambda shape: pl.BlockSpec(shape, lambda i: (0, 0))
    return pl.pallas_call(
        _mem_kv_kernel,
        grid=(1,),
        in_specs=[full((M, D)), full((1, D)), full((D, W)), full((D, W))],
        out_specs=[full((M, W)), full((M, W))],
        out_shape=[jax.ShapeDtypeStruct((M, W), BF16)] * 2,
        compiler_params=_cparams(("arbitrary",)),
        name="mem_kv",
    )(mem, g, wk, wv)


def _route(h, g_ref, rhi_ref, rlo_ref, u_ref, idx_ref, w_ref):
    u = _rms_rows(h, g_ref[...])
    u_ref[...] = u
    u_hi = u.astype(BF16)
    u_lo = (u - u_hi.astype(F32)).astype(BF16)
    rhi = rhi_ref[...]
    logits = (_dot(u_hi, rhi) + _dot(u_lo, rhi)) + _dot(u_hi, rlo_ref[...])
    lane = lax.broadcasted_iota(jnp.int32, logits.shape, 1)
    lane_f = lane.astype(F32)
    s = jnp.where(lane < N_EXPERTS, logits, REMOVED)
    m1 = jnp.max(s, axis=1, keepdims=True)
    i1 = jnp.min(jnp.where(s == m1, lane_f, float(LANES)), axis=1, keepdims=True)
    s = jnp.where(lane_f == i1, REMOVED, s)
    m2 = jnp.max(s, axis=1, keepdims=True)
    i2 = jnp.min(jnp.where(s == m2, lane_f, float(LANES)), axis=1, keepdims=True)
    e2 = jnp.exp(m2 - m1)
    den = 1.0 + e2
    idx_ref[...] = jnp.where(lane == 0, i1, jnp.where(lane == 1, i2, 0.0)).astype(jnp.int32)
    w_ref[...] = jnp.where(lane == 0, 1.0 / den, jnp.where(lane == 1, e2 / den, 0.0))


def _router_kernel(h_ref, g_ref, rhi_ref, rlo_ref, u_ref, idx_ref, w_ref):
    _route(h_ref[...], g_ref, rhi_ref, rlo_ref, u_ref, idx_ref, w_ref)


def _router(h, g, r_hi, r_lo, tm):
    S, D = h.shape
    rows = lambda width: pl.BlockSpec((tm, width), lambda i: (i, 0))
    return pl.pallas_call(
        _router_kernel,
        grid=(S // tm,),
        in_specs=[rows(D), _const_spec(g.shape), _const_spec(r_hi.shape), _const_spec(r_lo.shape)],
        out_specs=[rows(D), rows(LANES), rows(LANES)],
        out_shape=[jax.ShapeDtypeStruct((S, D), F32), jax.ShapeDtypeStruct((S, LANES), jnp.int32),
                   jax.ShapeDtypeStruct((S, LANES), F32)],
        compiler_params=_cparams(("arbitrary",)),
        name="moe_router",
    )(h, g, r_hi, r_lo)


def _mem_attn_kernel(*refs, pre_proj):
    it = iter(refs)
    if pre_proj:
        a_ref, wpre_ref = next(it), next(it)
    res_ref, g_ref, wq_ref, k_ref, v_ref, wo_ref, o_ref = it

    h = res_ref[...]
    if pre_proj:
        h = h + _dot(a_ref[...], wpre_ref[...])
    ub = _rms_rows(h, g_ref[...]).astype(BF16)
    q = (_dot(ub, wq_ref[...]) * (MEM_HEAD_DIM ** -0.5)).astype(BF16)
    outs = []
    for hd in range(MEM_HEADS):
        cols = slice(hd * MEM_HEAD_DIM, (hd + 1) * MEM_HEAD_DIM)
        s = _dot_t(q[:, cols], k_ref[:, cols])
        e = jnp.exp(s - jnp.max(s, axis=1, keepdims=True))
        p = e / jnp.sum(e, axis=1, keepdims=True)
        outs.append(_dot(p.astype(BF16), v_ref[:, cols]).astype(BF16))
    o_ref[...] = h + _dot(jnp.concatenate(outs, axis=1), wo_ref[...])


def _mem_attn(res, g, wq, k, v, wo, tm, pre_proj=None):
    S, D = res.shape
    rows = lambda width: pl.BlockSpec((tm, width), lambda i: (i, 0))
    args, in_specs = [], []
    if pre_proj is not None:
        a, w_pre = pre_proj
        args += [a, w_pre]
        in_specs += [rows(a.shape[1]), _const_spec(w_pre.shape)]
    args += [res, g, wq, k, v, wo]
    in_specs += [rows(D)] + [_const_spec(x.shape) for x in (g, wq, k, v, wo)]
    return pl.pallas_call(
        functools.partial(_mem_attn_kernel, pre_proj=pre_proj is not None),
        grid=(S // tm,),
        in_specs=in_specs,
        out_specs=rows(D),
        out_shape=jax.ShapeDtypeStruct((S, D), F32),
        compiler_params=_cparams(("arbitrary",)),
        name="mem_attention",
    )(*args)


def _swiglu_tile(ub, wg, wu, wd):
    gate = _dot(ub, wg)
    up = _dot(ub, wu)
    act = (gate * jax.nn.sigmoid(gate)) * up
    return _dot(act.astype(BF16), wd)


def _ffn_kernel(h_ref, g_ref, wg_ref, wu_ref, wd_ref, o_ref, u_ref):
    f = pl.program_id(1)

    @pl.when(f == 0)
    def _():
        h = h_ref[...]
        u_ref[...] = _rms_rows(h, g_ref[...]).astype(BF16)
        o_ref[...] = h

    o_ref[...] += _swiglu_tile(u_ref[...], wg_ref[...], wu_ref[...], wd_ref[...])


def _ffn(h, g, wg, wu, wd, tm, tf):
    S, D = h.shape
    F = wg.shape[1]
    return pl.pallas_call(
        _ffn_kernel,
        grid=(S // tm, F // tf),
        in_specs=[
            pl.BlockSpec((tm, D), lambda i, f: (i, 0)),
            pl.BlockSpec((1, D), lambda i, f: (0, 0)),
            pl.BlockSpec((D, tf), lambda i, f: (0, f)),
            pl.BlockSpec((D, tf), lambda i, f: (0, f)),
            pl.BlockSpec((tf, D), lambda i, f: (f, 0)),
        ],
        out_specs=pl.BlockSpec((tm, D), lambda i, f: (i, 0)),
        out_shape=jax.ShapeDtypeStruct((S, D), F32),
        scratch_shapes=[pltpu.VMEM((tm, D), BF16)],
        compiler_params=_cparams(("arbitrary", "arbitrary")),
        name="dense_swiglu",
    )(h, g, wg, wu, wd)


POOL_HALO = 16


def _pool_kernel(h_ref, halo_ref, g_ref, w_ref, b_ref, scale_ref, o_ref, ext_ref):
    i = pl.program_id(0)
    tm = h_ref.shape[0]
    h = h_ref[...]
    g = g_ref[...]
    u = _rms_rows(h, g)
    halo = _rms_rows(halo_ref[...], g)
    ext_ref[0:POOL_HALO, :] = jnp.where(i > 0, halo, 0.0)
    ext_ref[POOL_HALO:, :] = u
    pos = i * tm + lax.broadcasted_iota(jnp.int32, (tm, 1), 0)
    for gi, w in enumerate(POOL_WINDOWS):
        cols = slice(gi * POOL_GROUP, (gi + 1) * POOL_GROUP)
        tot = u[:, cols]
        for k in range(1, w):
            tot = tot + ext_ref[POOL_HALO - k:POOL_HALO - k + tm, cols]
        cnt = jnp.minimum(pos + 1, w).astype(F32)
        d = tot / cnt - u[:, cols]
        z = _dot(d.astype(BF16), w_ref[gi]) + b_ref[:, cols]
        o_ref[:, cols] = h[:, cols] + z * scale_ref[:, cols]


def _pool_mixer(h, g, w, b, scale, tm):
    S, D = h.shape
    ratio = tm // POOL_HALO
    return pl.pallas_call(
        _pool_kernel,
        grid=(S // tm,),
        in_specs=[
            pl.BlockSpec((tm, D), lambda i: (i, 0)),
            pl.BlockSpec((POOL_HALO, D), lambda i: (jnp.maximum(i * ratio - 1, 0), 0)),
            pl.BlockSpec((1, D), lambda i: (0, 0)),
            pl.BlockSpec(w.shape, lambda i: (0, 0, 0)),
            pl.BlockSpec((1, D), lambda i: (0, 0)),
            pl.BlockSpec((1, D), lambda i: (0, 0)),
        ],
        out_specs=pl.BlockSpec((tm, D), lambda i: (i, 0)),
        out_shape=jax.ShapeDtypeStruct((S, D), F32),
        scratch_shapes=[pltpu.VMEM((tm + POOL_HALO, D), F32)],
        compiler_params=_cparams(("arbitrary",)),
        name="pool_mixer",
    )(h, h, g, w, b, scale)


GATHER_UNROLL = 8


def _start_row_gather(src_hbm, dst_vmem, sem, n_rows, src_row_of):
    def body(r, c):
        pltpu.make_async_copy(src_hbm.at[pl.ds(src_row_of(r), 1), :], dst_vmem.at[pl.ds(r, 1), :], sem).start()
        return c

    lax.fori_loop(0, n_rows, body, 0, unroll=GATHER_UNROLL)


def _wait_row_gather(src_hbm, dst_vmem, sem):
    pltpu.make_async_copy(src_hbm.at[pl.ds(0, dst_vmem.shape[0]), :], dst_vmem, sem).wait()


def _moe_expert_kernel(tok_ref, exp_ref, nused_ref, u_hbm, wg_ref, wu_ref, wd_ref, o_ref, x_ref, xb_ref, sem):
    i = pl.program_id(0)
    f = pl.program_id(1)
    rows = x_ref.shape[1]
    n_used = nused_ref[0]
    active = i < n_used
    slot = i & 1

    def gather(block, into):
        _start_row_gather(u_hbm, x_ref.at[into], sem.at[into], rows, lambda r: tok_ref[block * rows + r])

    @pl.when((i == 0) & (f == 0) & active)
    def _():
        gather(0, 0)

    @pl.when(active & (f == 0))
    def _():
        _wait_row_gather(u_hbm, x_ref.at[slot], sem.at[slot])
        xb_ref[...] = x_ref[slot].astype(BF16)
        o_ref[...] = jnp.zeros_like(o_ref)

        @pl.when(i + 1 < n_used)
        def _():
            gather(i + 1, 1 - slot)

    @pl.when(active)
    def _():
        o_ref[...] += _swiglu_tile(xb_ref[...], wg_ref[0], wu_ref[0].astype(BF16), wd_ref[0].astype(BF16))

    @pl.when(jnp.logical_not(active) & (f == 0))
    def _():
        o_ref[...] = jnp.zeros_like(o_ref)


def _moe_experts(tok, blk_exp, n_used, u, wg, wu, wd, tf):
    N, D = u.shape
    n_blk = blk_exp.shape[0]
    F = wg.shape[2]
    rows = MOE_BLOCK
    n_f = F // tf

    def ftile(i, f, nu):
        return jnp.where(i < nu[0], f, n_f - 1)

    grid_spec = pltpu.PrefetchScalarGridSpec(
        num_scalar_prefetch=3,
        grid=(n_blk, n_f),
        in_specs=[
            pl.BlockSpec(memory_space=pl.ANY),
            pl.BlockSpec((1, D, tf), lambda i, f, tok, ex, nu: (ex[i], 0, ftile(i, f, nu))),
            pl.BlockSpec((1, D, tf), lambda i, f, tok, ex, nu: (ex[i], 0, ftile(i, f, nu))),
            pl.BlockSpec((1, tf, D), lambda i, f, tok, ex, nu: (ex[i], ftile(i, f, nu), 0)),
        ],
        out_specs=pl.BlockSpec((rows, D), lambda i, f, tok, ex, nu: (i, 0)),
        scratch_shapes=[pltpu.VMEM((2, rows, D), F32), pltpu.VMEM((rows, D), BF16), pltpu.SemaphoreType.DMA((2,))],
    )
    return pl.pallas_call(
        _moe_expert_kernel,
        grid_spec=grid_spec,
        out_shape=jax.ShapeDtypeStruct((n_blk * rows, D), F32),
        compiler_params=_cparams(("arbitrary", "arbitrary")),
        name="moe_experts",
    )(tok, blk_exp, n_used, u, wg, wu, wd)


def _moe_combine_kernel(pos_ref, h_ref, w_ref, gfin_ref, y_hbm, o_ref, r_ref, sem):
    i = pl.program_id(0)
    n_steps = pl.num_programs(0)
    tm = h_ref.shape[0]
    slot = i & 1

    def gather(step, into):
        for k in range(TOP_K):
            _start_row_gather(y_hbm, r_ref.at[into, k], sem.at[into, k], tm,
                              lambda r: pos_ref[(step * tm + r) * TOP_K + k])

    @pl.when(i == 0)
    def _():
        gather(0, 0)

    for k in range(TOP_K):
        _wait_row_gather(y_hbm, r_ref.at[slot, k], sem.at[slot, k])

    @pl.when(i + 1 < n_steps)
    def _():
        gather(i + 1, 1 - slot)

    w = w_ref[...]
    y = h_ref[...] + (r_ref[slot, 0] * w[:, 0:1] + r_ref[slot, 1] * w[:, 1:2])
    o_ref[...] = _rms_rows(y, gfin_ref[...])


def _moe_combine(pos, h, w, g_final, y, tm):
    S, D = h.shape
    grid_spec = pltpu.PrefetchScalarGridSpec(
        num_scalar_prefetch=1,
        grid=(S // tm,),
        in_specs=[
            pl.BlockSpec((tm, D), lambda i, pos: (i, 0)),
            pl.BlockSpec((tm, LANES), lambda i, pos: (i, 0)),
            pl.BlockSpec((1, D), lambda i, pos: (0, 0)),
            pl.BlockSpec(memory_space=pl.ANY),
        ],
        out_specs=pl.BlockSpec((tm, D), lambda i, pos: (i, 0)),
        scratch_shapes=[pltpu.VMEM((2, TOP_K, tm, D), F32), pltpu.SemaphoreType.DMA((2, TOP_K))],
    )
    return pl.pallas_call(
        _moe_combine_kernel,
        grid_spec=grid_spec,
        out_shape=jax.ShapeDtypeStruct((S, D), F32),
        compiler_params=_cparams(("arbitrary",)),
        name="moe_combine_norm",
    )(pos, h, w, g_final, y)


def _slot_scatter_kernel(dest_ref, tok_ref):
    def zero(i, c):
        tok_ref[i] = 0
        return c

    lax.fori_loop(0, tok_ref.shape[0], zero, 0, unroll=GATHER_UNROLL)

    shift = TOP_K.bit_length() - 1
    assert 1 << shift == TOP_K

    def put(s, c):
        tok_ref[dest_ref[s]] = lax.shift_right_logical(s, shift)
        return c

    lax.fori_loop(0, dest_ref.shape[0], put, 0, unroll=GATHER_UNROLL)


def _slot_scatter(dest, cap):
    return pl.pallas_call(
        _slot_scatter_kernel,
        in_specs=[pl.BlockSpec(memory_space=pltpu.SMEM)],
        out_specs=pl.BlockSpec(memory_space=pltpu.SMEM),
        out_shape=jax.ShapeDtypeStruct((cap,), jnp.int32),
        name="moe_slot_scatter",
    )(dest)


def _moe_dispatch_indices(top_idx):
    N = top_idx.shape[0]
    e_flat = top_idx.reshape(-1)
    onehot = (e_flat[:, None] == jnp.arange(N_EXPERTS, dtype=jnp.int32)[None, :]).astype(jnp.int32)
    csum = jnp.cumsum(onehot, axis=0)
    counts = csum[-1]
    rank = jnp.take_along_axis(csum, e_flat[:, None], axis=1)[:, 0] - 1
    padded = (counts + MOE_BLOCK - 1) // MOE_BLOCK * MOE_BLOCK
    pend = jnp.cumsum(padded)
    poff = pend - padded
    dest = poff[e_flat] + rank
    n_blk = (N * TOP_K + MOE_BLOCK - 1) // MOE_BLOCK + N_EXPERTS
    cap = n_blk * MOE_BLOCK
    tok_buf = _slot_scatter(dest.astype(jnp.int32), cap)
    blk_start = jnp.arange(n_blk, dtype=jnp.int32) * MOE_BLOCK
    blk_exp = jnp.minimum(jnp.sum(pend[None, :] <= blk_start[:, None], axis=1), N_EXPERTS - 1).astype(jnp.int32)
    n_used = (pend[-1] // MOE_BLOCK).astype(jnp.int32).reshape(1)
    return tok_buf, blk_exp, n_used, dest.astype(jnp.int32)


def _rope_tables(pos):
    half = HEAD_DIM // 2
    inv = ROPE_THETA ** (-np.arange(half, dtype=np.float64) / half)
    ang = np.asarray(pos, np.float64)[:, None] * inv[None, :]
    cos, sin = np.cos(ang), np.sin(ang)
    return (jnp.asarray(np.concatenate([cos, cos], axis=1), F32),
            jnp.asarray(np.concatenate([-sin, sin], axis=1), F32))


def _gate_params(w_gates, gate_b):
    D = w_gates.shape[0]
    w = w_gates.reshape(D, 3, N_KV_GROUPS, HEADS_PER_GROUP).transpose(0, 2, 1, 3).reshape(D, N_KV_GROUPS, 3 * HEADS_PER_GROUP)
    w = jnp.pad(w, ((0, 0), (0, 0), (0, LANES - 3 * HEADS_PER_GROUP))).reshape(D, N_KV_GROUPS * LANES)
    b = gate_b.reshape(3, N_KV_GROUPS, HEADS_PER_GROUP).transpose(1, 0, 2).reshape(N_KV_GROUPS, 3 * HEADS_PER_GROUP)
    b = jnp.pad(b, ((0, 0), (0, LANES - 3 * HEADS_PER_GROUP))).reshape(1, N_KV_GROUPS * LANES)
    return w.astype(BF16), b


def _chunked_groups(t):
    S = t.shape[0]
    return (t.reshape(S // CMP_STRIDE, CMP_STRIDE, N_KV_GROUPS, HEAD_DIM)
            .transpose(2, 0, 1, 3).reshape(N_KV_GROUPS, S // CMP_STRIDE, CMP_STRIDE * HEAD_DIM))


def _row_tile(S, want):
    t = min(S, want)
    assert S % t == 0
    return t


def kernel(x, mem, norm_mix, norm_mem_q, norm_mem_kv, norm_ffn, norm_final, nsa_w_in, nsa_gate_b, nsa_pe_k, nsa_pe_v, nsa_cmp_k_w1, nsa_cmp_k_w2, nsa_cmp_v_w1, nsa_cmp_v_w2, nsa_w_out, pool_w, pool_b, pool_scale, mem_wq, mem_wk, mem_wv, mem_wo, ffn_w_gate, ffn_w_up, ffn_w_down, moe_router, moe_w_gate, moe_w_up, moe_w_down):
    B, S, D = x.shape
    assert B == 1 and S % WINDOW == 0 and S >= SLC_BLOCK * N_SLC
    h = x.reshape(S, D)
    memf = mem.reshape(mem.shape[1], D)
    row = lambda v: v.reshape(1, -1)

    n_main = N_HEADS * HEAD_DIM + 6 * KV_WIDTH
    w_in = nsa_w_in[0]
    w_gate, b_gate = _gate_params(w_in[:, n_main:], nsa_gate_b[0])
    cos, sin = _rope_tables(np.arange(S))
    proj, gates = _nsa_proj(h, row(norm_mix[0]), w_in[:, :n_main].astype(BF16), w_gate, b_gate, cos, sin,
                            _row_tile(S, 1024))
    q_w = N_HEADS * HEAD_DIM
    n_chunks = S // CMP_STRIDE
    cmp_pos = np.arange(n_chunks) * CMP_STRIDE + CMP_BLOCK - 1
    ccos, csin = _rope_tables(cmp_pos)
    kc = _compress(_chunked_groups(proj[:, q_w:q_w + KV_WIDTH]), nsa_pe_k[0].reshape(1, -1),
                   nsa_cmp_k_w1[0].astype(BF16), nsa_cmp_k_w2[0].astype(BF16), ccos, csin, True)
    vct = _compress(_chunked_groups(proj[:, q_w + KV_WIDTH:q_w + 2 * KV_WIDTH]), nsa_pe_v[0].reshape(1, -1),
                    nsa_cmp_v_w1[0].astype(BF16), nsa_cmp_v_w2[0].astype(BF16), ccos, csin, False).transpose(0, 2, 1)
    attn = _nsa_attention(proj, gates, kc, vct)

    def mem_layer(h, i, **fused):
        k, v = _mem_kv(memf, row(norm_mem_kv[i]), mem_wk[i].astype(BF16), mem_wv[i].astype(BF16))
        return _mem_attn(h, row(norm_mem_q[i]), mem_wq[i].astype(BF16), k, v, mem_wo[i].astype(BF16),
                         _row_tile(S, 512), **fused)

    h = mem_layer(h, 0, pre_proj=(attn, nsa_w_out[0].astype(BF16)))
    h = _ffn(h, row(norm_ffn[0]), ffn_w_gate[0].astype(BF16), ffn_w_up[0].astype(BF16),
             ffn_w_down[0].astype(BF16), _row_tile(S, 1024), 512)

    h = _pool_mixer(h, row(norm_mix[1]), pool_w[0].astype(BF16), row(pool_b[0]), row(pool_scale[0]),
                    _row_tile(S, 512))
    h = mem_layer(h, 1)
    r = jnp.pad(moe_router[0], ((0, 0), (0, LANES - N_EXPERTS)))
    r_hi = r.astype(BF16)
    r_lo = (r - r_hi.astype(F32)).astype(BF16)
    u, idx, gate_w = _router(h, row(norm_ffn[1]), r_hi, r_lo, _row_tile(S, 512))
    tok_buf, blk_exp, n_used, dest = _moe_dispatch_indices(idx[:, :TOP_K])
    y = _moe_experts(tok_buf, blk_exp, n_used, u, moe_w_gate[0].astype(BF16), moe_w_up[0], moe_w_down[0], 512)
    out = _moe_combine(dest, h, gate_w, row(norm_final), y, _row_tile(S, 256))
    return out.reshape(B, S, D)
```
